```python
import jax, jax.numpy as jnp
from jax import lax
import numpy as np

D_MODEL = 1024
BATCH = 16
SEQ = 2048
DEPTH = 2

GRID_W = 64
CTX_LEN = 256
N_MIXERS = 2
HEAD_DIM = 64
DA_HEADS = D_MODEL // (2 * HEAD_DIM)
NA_HEADS = D_MODEL // HEAD_DIM
NA_WIN_ROWS = 8
NA_WIN_COLS = 16
D_FF = ((8 * D_MODEL + 3 * 256 - 1) // (3 * 256)) * 256
ROPE_THETA = 10000.0
Q_BLOCK = 128
EPS = 1e-6

kernel_name = "hybrid_diffattn_natten_prefix_dit"


def rmsnorm(x, g):
    xf = x.astype(jnp.float32)
    y = xf * lax.rsqrt(jnp.mean(xf * xf, axis=-1, keepdims=True) + EPS)
    return (y * g.astype(jnp.float32)).astype(x.dtype)


def adaln(cond, w, b):
    mod = jax.nn.silu(cond) @ w + b
    return jnp.split(mod, 6, axis=-1)


def modulate(xn, shift, scale):
    return xn * (1 + scale) + shift


def swiglu(h, w_gate_up, w_down):
    gu = h @ w_gate_up
    g, u = jnp.split(gu, 2, axis=-1)
    return (jax.nn.silu(g) * u) @ w_down


def axial_rope_tables(n):
    t = jnp.arange(n)
    row = (t // GRID_W).astype(jnp.float32)
    col = (t % GRID_W).astype(jnp.float32)
    half = HEAD_DIM // 2
    freqs = 1.0 / (ROPE_THETA ** (jnp.arange(0, half, 2, dtype=jnp.float32) / half))
    ar = row[:, None] * freqs
    ac = col[:, None] * freqs
    ang = jnp.concatenate([ar, ar, ac, ac], axis=-1)
    return jnp.cos(ang), jnp.sin(ang)


def rotate_half(x):
    x1, x2 = jnp.split(x, 2, axis=-1)
    return jnp.concatenate([-x2, x1], axis=-1)


def apply_rope(t, cos, sin):
    half = t.shape[-1] // 2
    rot = jnp.concatenate([rotate_half(t[..., :half]), rotate_half(t[..., half:])], axis=-1)
    shape = (t.shape[1],) + (1,) * (t.ndim - 3) + (t.shape[-1],)
    return t * cos.reshape(shape).astype(t.dtype) + rot * sin.reshape(shape).astype(t.dtype)


def diff_attention(h_lat, h_ctx, wqkv, lq1, lk1, lq2, lk2, subln, wo, lam_init, need_ctx_out):
    B, N, _ = h_lat.shape
    L = h_ctx.shape[1]
    scale = HEAD_DIM ** -0.5

    def proj(h):
        n = h.shape[1]
        q, k, v = jnp.split(h @ wqkv, 3, axis=-1)
        return (q.reshape(B, n, DA_HEADS, 2, HEAD_DIM),
                k.reshape(B, n, DA_HEADS, 2, HEAD_DIM),
                v.reshape(B, n, DA_HEADS, 2 * HEAD_DIM))

    q_l, k_l, v_l = proj(h_lat)
    q_c, k_c, v_c = proj(h_ctx)
    cos, sin = axial_rope_tables(N)
    q_l = apply_rope(q_l, cos, sin)
    k_l = apply_rope(k_l, cos, sin)

    f32 = jnp.float32
    lam = (jnp.exp(jnp.sum(lq1.astype(f32) * lk1.astype(f32)))
           - jnp.exp(jnp.sum(lq2.astype(f32) * lk2.astype(f32))) + lam_init)

    def attend(q_blk, k, v):
        s = jnp.einsum('bqhmd,bkhmd->bhmqk', q_blk, k).astype(f32) * scale
        p = jax.nn.softmax(s, axis=-1)
        p = p[:, :, 0] - lam * p[:, :, 1]
        return jnp.einsum('bhqk,bkhe->bqhe', p.astype(v.dtype), v)

    k_all = jnp.concatenate([k_c, k_l], axis=1)
    v_all = jnp.concatenate([v_c, v_l], axis=1)
    nblk = N // Q_BLOCK
    q_blocks = q_l.reshape(B, nblk, Q_BLOCK, DA_HEADS, 2, HEAD_DIM).transpose(1, 0, 2, 3, 4, 5)
    o_l = lax.map(lambda qb: attend(qb, k_all, v_all), q_blocks)
    o_l = o_l.transpose(1, 0, 2, 3, 4).reshape(B, N, DA_HEADS, 2 * HEAD_DIM)

    def finish(o, n):
        o = rmsnorm(o, subln) * (1.0 - lam_init)
        return o.reshape(B, n, D_MODEL) @ wo

    y_l = finish(o_l, N)
    y_c = finish(attend(q_c, k_c, v_c), L) if need_ctx_out else None
    return y_l, y_c


def neighbourhood_attention(h_lat, h_ctx, wqkv, rpb, wo, need_ctx_out):
    B, N, _ = h_lat.shape
    L = h_ctx.shape[1]
    rows = N // GRID_W
    wr = min(NA_WIN_ROWS, rows)
    scale = HEAD_DIM ** -0.5
    f32 = jnp.float32

    def proj(h):
        n = h.shape[1]
        q, k, v = jnp.split(h @ wqkv, 3, axis=-1)
        return tuple(t.reshape(B, n, NA_HEADS, HEAD_DIM) for t in (q, k, v))

    q_l, k_l, v_l = proj(h_lat)
    q_c, k_c, v_c = proj(h_ctx)
    kg = k_l.reshape(B, rows, GRID_W, NA_HEADS, HEAD_DIM)
    vg = v_l.reshape(B, rows, GRID_W, NA_HEADS, HEAD_DIM)
    qg = q_l.reshape(B, rows, GRID_W, NA_HEADS, HEAD_DIM).transpose(1, 0, 2, 3, 4)

    qc = np.arange(GRID_W)
    col_start = np.clip(qc - NA_WIN_COLS // 2, 0, GRID_W - NA_WIN_COLS)
    col_mask = (qc[None, :] >= col_start[:, None]) & (qc[None, :] < col_start[:, None] + NA_WIN_COLS)
    dc_idx = np.clip(qc[None, :] - qc[:, None] + NA_WIN_COLS - 1, 0, 2 * NA_WIN_COLS - 2)
    rpb_cols = rpb[:, :, dc_idx]
    col_mask = jnp.asarray(col_mask)[None, None, :, None, :]

    def row_block(args):
        q_row, r = args
        rs = jnp.clip(r - NA_WIN_ROWS // 2, 0, rows - wr)
        kb = lax.dynamic_slice_in_dim(kg, rs, wr, axis=1)
        vb = lax.dynamic_slice_in_dim(vg, rs, wr, axis=1)
        dr_idx = rs + jnp.arange(wr) - r + NA_WIN_ROWS - 1
        bias = rpb_cols[:, dr_idx].transpose(0, 2, 1, 3)
        s_band = jnp.einsum('bqhd,brkhd->bhqrk', q_row, kb).astype(f32) * scale + bias.astype(f32)
        s_band = jnp.where(col_mask, s_band, -jnp.inf)
        s_ctx = jnp.einsum('bqhd,bchd->bhqc', q_row, k_c).astype(f32) * scale
        s = jnp.concatenate([s_band.reshape(B, NA_HEADS, GRID_W, wr * GRID_W), s_ctx], axis=-1)
        p = jax.nn.softmax(s, axis=-1).astype(v_c.dtype)
        p_band = p[..., :wr * GRID_W].reshape(B, NA_HEADS, GRID_W, wr, GRID_W)
        p_ctx = p[..., wr * GRID_W:]
        return (jnp.einsum('bhqrk,brkhd->bqhd', p_band, vb)
                + jnp.einsum('bhqc,bchd->bqhd', p_ctx, v_c))

    o_l = lax.map(row_block, (qg, jnp.arange(rows)))
    y_l = o_l.transpose(1, 0, 2, 3, 4).reshape(B, N, D_MODEL) @ wo

    y_c = None
    if need_ctx_out:
        s = jnp.einsum('bqhd,bkhd->bhqk', q_c, k_c).astype(f32) * scale
        p = jax.nn.softmax(s, axis=-1).astype(v_c.dtype)
        y_c = jnp.einsum('bhqk,bkhd->bqhd', p, v_c).reshape(B, L, D_MODEL) @ wo
    return y_l, y_c


def setup_inputs(seed: int = 0) -> dict:
    key = jax.random.key(seed)
    ks = jax.random.split(key, 24)
    D, F = D_MODEL, D_FF
    n_a = (DEPTH + 1) // 2
    n_b = DEPTH // 2
    nrm = jax.random.normal
    f32 = jnp.float32
    return {
        "x": nrm(ks[0], (BATCH, SEQ, D), f32),
        "c": nrm(ks[1], (BATCH, D), f32),
        "ctx": nrm(ks[2], (BATCH, CTX_LEN, D), f32),
        "c_ctx": nrm(ks[3], (D,), f32),
        "ada_w": nrm(ks[4], (DEPTH, D, 6 * D), f32) * D ** -0.5,
        "ada_b": nrm(ks[5], (DEPTH, 6 * D), f32) * 0.01,
        "norm_mix": 1.0 + 0.02 * nrm(ks[6], (DEPTH, D), f32),
        "norm_ffn": 1.0 + 0.02 * nrm(ks[7], (DEPTH, D), f32),
        "da_wqkv": nrm(ks[8], (n_a, D, 3 * D), f32) * D ** -0.5,
        "da_lambda_q1": nrm(ks[9], (n_a, HEAD_DIM), f32) * 0.1,
        "da_lambda_k1": nrm(ks[10], (n_a, HEAD_DIM), f32) * 0.1,
        "da_lambda_q2": nrm(ks[11], (n_a, HEAD_DIM), f32) * 0.1,
        "da_lambda_k2": nrm(ks[12], (n_a, HEAD_DIM), f32) * 0.1,
        "da_subln": 1.0 + 0.02 * nrm(ks[13], (n_a, 2 * HEAD_DIM), f32),
        "da_wo": nrm(ks[14], (n_a, D, D), f32) * D ** -0.5,
        "na_wqkv": nrm(ks[15], (n_b, D, 3 * D), f32) * D ** -0.5,
        "na_rpb": nrm(ks[16], (n_b, NA_HEADS, 2 * NA_WIN_ROWS - 1, 2 * NA_WIN_COLS - 1), f32) * 0.02,
        "na_wo": nrm(ks[17], (n_b, D, D), f32) * D ** -0.5,
        "ffn_w_gate_up": nrm(ks[18], (DEPTH, D, 2 * F), f32) * D ** -0.5,
        "ffn_w_down": nrm(ks[19], (DEPTH, F, D), f32) * F ** -0.5,
        "norm_final": 1.0 + 0.02 * nrm(ks[20], (D,), f32),
    }


def reference(x, c, ctx, c_ctx, ada_w, ada_b, norm_mix, norm_ffn,
              da_wqkv, da_lambda_q1, da_lambda_k1, da_lambda_q2, da_lambda_k2, da_subln, da_wo,
              na_wqkv, na_rpb, na_wo, ffn_w_gate_up, ffn_w_down, norm_final):
    h, hc = x, ctx
    for i in range(DEPTH):
        last = i == DEPTH - 1
        sh_m, sc_m, g_m, sh_f, sc_f, g_f = (t[:, None, :] for t in adaln(c, ada_w[i], ada_b[i]))
        csh_m, csc_m, cg_m, csh_f, csc_f, cg_f = adaln(c_ctx, ada_w[i], ada_b[i])

        xn = modulate(rmsnorm(h, norm_mix[i]), sh_m, sc_m)
        xc = modulate(rmsnorm(hc, norm_mix[i]), csh_m, csc_m)
        j = i // N_MIXERS
        if i % N_MIXERS == 0:
            lam_init = 0.8 - 0.6 * float(np.exp(-0.3 * i))
            y, yc = diff_attention(xn, xc, da_wqkv[j], da_lambda_q1[j], da_lambda_k1[j],
                                   da_lambda_q2[j], da_lambda_k2[j], da_subln[j], da_wo[j],
                                   lam_init, not last)
        else:
            y, yc = neighbourhood_attention(xn, xc, na_wqkv[j], na_rpb[j], na_wo[j], not last)

        h = h + g_m * y
        h = h + g_f * swiglu(modulate(rmsnorm(h, norm_ffn[i]), sh_f, sc_f),
                             ffn_w_gate_up[i], ffn_w_down[i])
        if not last:
            hc = hc + cg_m * yc
            hc = hc + cg_f * swiglu(modulate(rmsnorm(hc, norm_ffn[i]), csh_f, csc_f),
                                    ffn_w_gate_up[i], ffn_w_down[i])
    return rmsnorm(h, norm_final)
```

```python
import functools

import numpy as np
import jax
import jax.numpy as jnp
from jax import lax
from jax.experimental import pallas as pl
from jax.experimental.pallas import tpu as pltpu

D_MODEL = 1024
BATCH = 16
SEQ = 2048
DEPTH = 2
GRID_W = 64
CTX_LEN = 256
HEAD_DIM = 64
NA_WIN_ROWS = 8
NA_WIN_COLS = 16
D_FF = 2816
ROPE_THETA = 10000.0
EPS = 1e-6

LANES = 128
N_LANE_TILES = D_MODEL // LANES
COND_ROWS = 24
CTX_MOD_ROW = BATCH
VMEM_LIMIT = 56 * 1024 * 1024

NA_Q_ROWS = 4
NA_K_ROWS = 12
NA_TQ = NA_Q_ROWS * GRID_W
NA_BLOCKS = SEQ // NA_TQ
NA_KBLK = NA_K_ROWS // NA_Q_ROWS

F32 = jnp.float32
BF16 = jnp.bfloat16


def _dot(a, b):
    return jnp.dot(a, b, preferred_element_type=F32)


def _dot_nt(a, b):
    return lax.dot_general(a, b, (((1,), (1,)), ((), ())), preferred_element_type=F32)


def _adaln_kernel(cond_ref, w_ref, b_ref, out_ref):
    cnd = cond_ref[...]
    a = cnd * jax.nn.sigmoid(cnd)
    w = w_ref[0]
    a_hi = a.astype(BF16)
    a_lo = (a - a_hi.astype(F32)).astype(BF16)
    w_hi = w.astype(BF16)
    w_lo = (w - w_hi.astype(F32)).astype(BF16)
    acc = _dot(a_hi, w_hi) + _dot(a_lo, w_hi) + _dot(a_hi, w_lo)
    out_ref[0] = acc + b_ref[0]


def _adaln(cond, ada_w, ada_b):
    n_out = ada_w.shape[-1]
    tn = 1024
    return pl.pallas_call(
        _adaln_kernel,
        grid=(DEPTH, n_out // tn),
        in_specs=[
            pl.BlockSpec((COND_ROWS, D_MODEL), lambda i, j: (0, 0)),
            pl.BlockSpec((1, D_MODEL, tn), lambda i, j: (i, 0, j)),
            pl.BlockSpec((1, 1, tn), lambda i, j: (i, 0, j)),
        ],
        out_specs=pl.BlockSpec((1, COND_ROWS, tn), lambda i, j: (i, 0, j)),
        out_shape=jax.ShapeDtypeStruct((DEPTH, COND_ROWS, n_out), F32),
        compiler_params=pltpu.CompilerParams(vmem_limit_bytes=VMEM_LIMIT),
        name="adaln",
    )(cond, ada_w, ada_b.reshape(DEPTH, 1, n_out))


def _rmsnorm_mod(h, g, shift, scale):
    y = h * lax.rsqrt(jnp.mean(h * h, axis=-1, keepdims=True) + EPS) * g
    return y * (1.0 + scale) + shift


def _qkv_kernel(*refs, rope, want_q):
    if rope:
        h_ref, mod_ref, g_ref, w_ref, cos_ref, sa_ref, sb_ref = refs[:7]
        outs = refs[7:]
    else:
        h_ref, mod_ref, g_ref, w_ref = refs[:4]
        outs = refs[4:]
    xn = _rmsnorm_mod(h_ref[0], g_ref[...], mod_ref[0, 0:1, :], mod_ref[0, 1:2, :]).astype(BF16)

    def rotate(t):
        return (t * cos_ref[...] + pltpu.roll(t, LANES - 16, axis=1) * sa_ref[...]
                + pltpu.roll(t, 16, axis=1) * sb_ref[...])

    parts = ("q", "k", "v") if want_q else ("k", "v")
    for name, o_ref in zip(parts, outs):
        col = {"q": 0, "k": D_MODEL, "v": 2 * D_MODEL}[name]
        acc = _dot(xn, w_ref[:, col:col + D_MODEL])
        if name == "q":
            acc = acc * (HEAD_DIM ** -0.5)
        if rope and name != "v":
            for c in range(N_LANE_TILES):
                sl = slice(c * LANES, (c + 1) * LANES)
                o_ref[0, :, sl] = rotate(acc[:, sl]).astype(BF16)
        else:
            o_ref[0] = acc.astype(BF16)


def _qkv(h, mod, g, w, rope_tabs, *, tm, want_q=True):
    G, R, _ = h.shape
    rope = rope_tabs is not None
    row_spec = pl.BlockSpec((1, tm, D_MODEL), lambda b, t: (b, t, 0))
    in_specs = [
        row_spec,
        pl.BlockSpec((1, 6, D_MODEL), lambda b, t: (b, 0, 0)),
        pl.BlockSpec((1, D_MODEL), lambda b, t: (0, 0)),
        pl.BlockSpec((D_MODEL, 3 * D_MODEL), lambda b, t: (0, 0)),
    ]
    args = [h, mod, g, w]
    if rope:
        in_specs += [pl.BlockSpec((tm, LANES), lambda b, t: (t, 0))] * 3
        args += list(rope_tabs)
    n_out = 3 if want_q else 2
    return pl.pallas_call(
        functools.partial(_qkv_kernel, rope=rope, want_q=want_q),
        grid=(G, R // tm),
        in_specs=in_specs,
        out_specs=[row_spec] * n_out,
        out_shape=[jax.ShapeDtypeStruct((G, R, D_MODEL), BF16)] * n_out,
        compiler_params=pltpu.CompilerParams(vmem_limit_bytes=VMEM_LIMIT),
        name="qkv_rope" if rope else "qkv",
    )(*args)


def _rope_tables():
    t = jnp.arange(SEQ)
    row = (t // GRID_W).astype(F32)
    col = (t % GRID_W).astype(F32)
    half = HEAD_DIM // 2
    freqs = 1.0 / (ROPE_THETA ** (jnp.arange(0, half, 2, dtype=F32) / half))
    ar = row[:, None] * freqs
    ac = col[:, None] * freqs
    ang = jnp.concatenate([ar, ar, ac, ac], axis=-1)
    cos, sin = jnp.cos(ang), jnp.sin(ang)
    first = (np.arange(HEAD_DIM) % half) < half // 2
    sa = jnp.where(first, -sin, 0.0)
    sb = jnp.where(first, 0.0, sin)
    rep = LANES // HEAD_DIM
    return tuple(jnp.tile(x, (1, rep)) for x in (cos, sa, sb))


def _stack_halves(q):
    lane = lax.broadcasted_iota(jnp.int32, q.shape, 1)
    zero = jnp.zeros_like(q)
    return jnp.concatenate([jnp.where(lane < HEAD_DIM, q, zero),
                            jnp.where(lane >= HEAD_DIM, q, zero)], axis=0)


def _softmax_parts(s_parts):
    m = s_parts[0].max(axis=-1, keepdims=True)
    for s in s_parts[1:]:
        m = jnp.maximum(m, s.max(axis=-1, keepdims=True))
    e_parts = [jnp.exp(s - m) for s in s_parts]
    l = e_parts[0].sum(axis=-1, keepdims=True)
    for e in e_parts[1:]:
        l = l + e.sum(axis=-1, keepdims=True)
    return e_parts, 1.0 / l


def _diff_attn_kernel(*refs, nseg, lam_init):
    lam_ref, subln_ref, q_ref = refs[:3]
    k_refs = refs[3:3 + nseg]
    v_refs = refs[3 + nseg:3 + 2 * nseg]
    o_ref = refs[3 + 2 * nseg]
    tq = q_ref.shape[1]

    lp = lam_ref[...]
    lam = (jnp.exp(jnp.sum(lp[0:1] * lp[1:2], axis=-1, keepdims=True))
           - jnp.exp(jnp.sum(lp[2:3] * lp[3:4], axis=-1, keepdims=True)) + lam_init)

    qq = _stack_halves(q_ref[0])
    e_parts, inv_l = _softmax_parts([_dot_nt(qq, k_ref[0]) for k_ref in k_refs])
    w1 = inv_l[:tq]
    w2 = inv_l[tq:] * lam
    o = None
    for e, v_ref in zip(e_parts, v_refs):
        p = (e[:tq] * w1 - e[tq:] * w2).astype(BF16)
        pv = _dot(p, v_ref[0])
        o = pv if o is None else o + pv
    on = o * lax.rsqrt(jnp.mean(o * o, axis=-1, keepdims=True) + EPS) * subln_ref[...]
    o_ref[0] = (on * (1.0 - lam_init)).astype(BF16)


def _diff_attn(lam_params, subln, q, ks, vs, *, tq, lam_init):
    B, N, _ = q.shape
    nseg = len(ks)
    q_spec = pl.BlockSpec((1, tq, LANES), lambda b, h, i: (b, i, h))
    kv_specs = [pl.BlockSpec((1, k.shape[1], LANES), lambda b, h, i: (b, 0, h)) for k in ks]
    return pl.pallas_call(
        functools.partial(_diff_attn_kernel, nseg=nseg, lam_init=lam_init),
        grid=(B, N_LANE_TILES, N // tq),
        in_specs=[
            pl.BlockSpec((4, HEAD_DIM), lambda b, h, i: (0, 0)),
            pl.BlockSpec((1, LANES), lambda b, h, i: (0, 0)),
            q_spec,
        ] + kv_specs + kv_specs,
        out_specs=q_spec,
        out_shape=jax.ShapeDtypeStruct((B, N, D_MODEL), BF16),
        compiler_params=pltpu.CompilerParams(vmem_limit_bytes=VMEM_LIMIT),
        name="diff_attn_%dseg" % nseg,
    )(lam_params, subln, q, *ks, *vs)


def _na_window_start(j):
    return jnp.clip(j - 1, 0, NA_BLOCKS - NA_KBLK)


def _na_bias_type(j):
    return jnp.where(j == 0, 0, jnp.where(j == NA_BLOCKS - 1, 2, 1))


def _na_bias_indices():
    rows = SEQ // GRID_W
    a = np.arange(NA_TQ) // GRID_W
    qc = np.arange(NA_TQ) % GRID_W
    c = np.arange(NA_K_ROWS * GRID_W) // GRID_W
    kc = np.arange(NA_K_ROWS * GRID_W) % GRID_W
    per_block = []
    for j in range(NA_BLOCKS):
        r = NA_Q_ROWS * j + a
        kr = NA_Q_ROWS * int(np.clip(j - 1, 0, NA_BLOCKS - NA_KBLK)) + c
        rs = np.clip(r - NA_WIN_ROWS // 2, 0, rows - NA_WIN_ROWS)
        cs = np.clip(qc - NA_WIN_COLS // 2, 0, GRID_W - NA_WIN_COLS)
        valid = ((kr[None, :] >= rs[:, None]) & (kr[None, :] < rs[:, None] + NA_WIN_ROWS)
                 & (kc[None, :] >= cs[:, None]) & (kc[None, :] < cs[:, None] + NA_WIN_COLS))
        dr = np.clip(kr[None, :] - r[:, None] + NA_WIN_ROWS - 1, 0, 2 * NA_WIN_ROWS - 2)
        dc = np.clip(kc[None, :] - qc[:, None] + NA_WIN_COLS - 1, 0, 2 * NA_WIN_COLS - 2)
        per_block.append((valid, dr, dc))
    for j in range(2, NA_BLOCKS - 1):
        assert all(np.array_equal(x, y) for x, y in zip(per_block[1], per_block[j]))
    picks = [per_block[0], per_block[1], per_block[NA_BLOCKS - 1]]
    return tuple(np.stack([p[i] for p in picks]) for i in range(3))


def _na_bias_table(rpb):
    valid, dr, dc = _na_bias_indices()
    tab = rpb[:, dr, dc]
    tab = jnp.where(valid[None], tab, -jnp.inf)
    tab = tab.transpose(1, 0, 2, 3)
    return tab.reshape(3, N_LANE_TILES, 2, NA_TQ, NA_K_ROWS * GRID_W)


def _nbr_attn_kernel(q_ref, k0_ref, k1_ref, k2_ref, kc_ref, v0_ref, v1_ref, v2_ref, vc_ref,
                     bias_ref, o_ref):
    qq = _stack_halves(q_ref[0])
    bias = bias_ref[0, 0].reshape(2 * NA_TQ, NA_K_ROWS * GRID_W)
    s_parts = []
    for d, k_ref in enumerate((k0_ref, k1_ref, k2_ref)):
        s_parts.append(_dot_nt(qq, k_ref[0]) + bias[:, d * NA_TQ:(d + 1) * NA_TQ])
    s_parts.append(_dot_nt(qq, kc_ref[0]))
    e_parts, inv_l = _softmax_parts(s_parts)
    o2 = None
    for e, v_ref in zip(e_parts, (v0_ref, v1_ref, v2_ref, vc_ref)):
        pv = _dot((e * inv_l).astype(BF16), v_ref[0])
        o2 = pv if o2 is None else o2 + pv
    lane = lax.broadcasted_iota(jnp.int32, (NA_TQ, LANES), 1)
    o_ref[0] = jnp.where(lane < HEAD_DIM, o2[:NA_TQ], o2[NA_TQ:]).astype(BF16)


def _nbr_attn(q, k, v, kc, vc, bias_tab):
    B = q.shape[0]
    q_spec = pl.BlockSpec((1, NA_TQ, LANES), lambda hp, j, b: (b, j, hp))

    def win_spec(d):
        return pl.BlockSpec((1, NA_TQ, LANES), lambda hp, j, b: (b, _na_window_start(j) + d, hp))

    ctx_spec = pl.BlockSpec((1, CTX_LEN, LANES), lambda hp, j, b: (b, 0, hp))
    bias_spec = pl.BlockSpec((1, 1, 2, NA_TQ, NA_K_ROWS * GRID_W),
                             lambda hp, j, b: (_na_bias_type(j), hp, 0, 0, 0))
    wins = [win_spec(d) for d in range(NA_KBLK)]
    return pl.pallas_call(
        _nbr_attn_kernel,
        grid=(N_LANE_TILES, NA_BLOCKS, B),
        in_specs=[q_spec] + wins + [ctx_spec] + wins + [ctx_spec, bias_spec],
        out_specs=q_spec,
        out_shape=jax.ShapeDtypeStruct(q.shape, BF16),
        compiler_params=pltpu.CompilerParams(vmem_limit_bytes=VMEM_LIMIT),
        name="nbr_attn",
    )(q, k, k, k, kc, v, v, v, vc, bias_tab)


FFN_CHUNKS = ((0, 1024), (1024, 1024), (2048, 768))


def _out_ffn_kernel(*refs, final):
    o_ref, h_ref, mod_ref, g_ref, wo_ref, wgu_ref, wd_ref = refs[:7]
    if final:
        gfin_ref, out_ref = refs[7:]
    else:
        (out_ref,) = refs[7:]
    h1 = h_ref[0] + mod_ref[0, 2:3, :] * _dot(o_ref[0], wo_ref[...])
    xn = _rmsnorm_mod(h1, g_ref[...], mod_ref[0, 3:4, :], mod_ref[0, 4:5, :]).astype(BF16)
    acc = None
    for start, size in FFN_CHUNKS:
        gate = _dot(xn, wgu_ref[:, start:start + size])
        up = _dot(xn, wgu_ref[:, D_FF + start:D_FF + start + size])
        act = (gate * jax.nn.sigmoid(gate) * up).astype(BF16)
        part = _dot(act, wd_ref[start:start + size, :])
        acc = part if acc is None else acc + part
    h2 = h1 + mod_ref[0, 5:6, :] * acc
    if final:
        h2 = h2 * lax.rsqrt(jnp.mean(h2 * h2, axis=-1, keepdims=True) + EPS) * gfin_ref[...]
    out_ref[0] = h2


def _out_ffn(o, h, mod, g, wo, wgu, wd, gfin, *, tm):
    G, R, _ = h.shape
    final = gfin is not None
    row_spec = pl.BlockSpec((1, tm, D_MODEL), lambda b, t: (b, t, 0))

    def resident(shape):
        return pl.BlockSpec(shape, lambda b, t: (0, 0), pipeline_mode=pl.Buffered(1))

    in_specs = [
        row_spec,
        row_spec,
        pl.BlockSpec((1, 6, D_MODEL), lambda b, t: (b, 0, 0)),
        pl.BlockSpec((1, D_MODEL), lambda b, t: (0, 0)),
        resident((D_MODEL, D_MODEL)),
        resident((D_MODEL, 2 * D_FF)),
        resident((D_FF, D_MODEL)),
    ]
    args = [o, h, mod, g, wo, wgu, wd]
    if final:
        in_specs.append(pl.BlockSpec((1, D_MODEL), lambda b, t: (0, 0)))
        args.append(gfin)
    return pl.pallas_call(
        functools.partial(_out_ffn_kernel, final=final),
        grid=(G, R // tm),
        in_specs=in_specs,
        out_specs=row_spec,
        out_shape=jax.ShapeDtypeStruct(h.shape, F32),
        compiler_params=pltpu.CompilerParams(vmem_limit_bytes=VMEM_LIMIT),
        name="out_ffn_final" if final else "out_ffn",
    )(*args)


def kernel(x, c, ctx, c_ctx, ada_w, ada_b, norm_mix, norm_ffn, da_wqkv, da_lambda_q1, da_lambda_k1,
           da_lambda_q2, da_lambda_k2, da_subln, da_wo, na_wqkv, na_rpb, na_wo, ffn_w_gate_up,
           ffn_w_down, norm_final):
    B, N, D = x.shape
    L = ctx.shape[1]
    TM = 512

    cond = jnp.zeros((COND_ROWS, D), F32).at[:B].set(c).at[CTX_MOD_ROW].set(c_ctx)
    mods = _adaln(cond, ada_w, ada_b).reshape(DEPTH, COND_ROWS, 6, D)

    h = x
    hc = ctx.reshape(1, B * L, D)
    rope_tabs = _rope_tables()

    mod_l = mods[0, :B]
    mod_c = mods[0, CTX_MOD_ROW:CTX_MOD_ROW + 1]
    g_mix = norm_mix[0].reshape(1, D)
    g_ffn = norm_ffn[0].reshape(1, D)
    wqkv = da_wqkv[0].astype(BF16)
    q_l, k_l, v_l = _qkv(h, mod_l, g_mix, wqkv, rope_tabs, tm=TM)
    q_c, k_c, v_c = (t.reshape(B, L, D) for t in _qkv(hc, mod_c, g_mix, wqkv, None, tm=TM))
    lam_params = jnp.stack([da_lambda_q1[0], da_lambda_k1[0], da_lambda_q2[0], da_lambda_k2[0]])
    subln = da_subln[0].reshape(1, LANES)
    lam_init = 0.8 - 0.6 * float(np.exp(-0.3 * 0))
    o_l = _diff_attn(lam_params, subln, q_l, [k_c, k_l], [v_c, v_l], tq=256, lam_init=lam_init)
    o_c = _diff_attn(lam_params, subln, q_c, [k_c], [v_c], tq=L, lam_init=lam_init)
    wo = da_wo[0].astype(BF16)
    wgu = ffn_w_gate_up[0].astype(BF16)
    wd = ffn_w_down[0].astype(BF16)
    h = _out_ffn(o_l, h, mod_l, g_ffn, wo, wgu, wd, None, tm=TM)
    hc = _out_ffn(o_c.reshape(1, B * L, D), hc, mod_c, g_ffn, wo, wgu, wd, None, tm=TM)

    mod_l = mods[1, :B]
    mod_c = mods[1, CTX_MOD_ROW:CTX_MOD_ROW + 1]
    g_mix = norm_mix[1].reshape(1, D)
    g_ffn = norm_ffn[1].reshape(1, D)
    wqkv = na_wqkv[0].astype(BF16)
    q_l, k_l, v_l = _qkv(h, mod_l, g_mix, wqkv, None, tm=TM)
    k_c, v_c = (t.reshape(B, L, D) for t in _qkv(hc, mod_c, g_mix, wqkv, None, tm=TM, want_q=False))
    o_l = _nbr_attn(q_l, k_l, v_l, k_c, v_c, _na_bias_table(na_rpb[0]))
    return _out_ffn(o_l, h, mod_l, g_ffn, na_wo[0].astype(BF16), ffn_w_gate_up[1].astype(BF16),
                    ffn_w_down[1].astype(BF16), norm_final.reshape(1, D), tm=TM)
```

```python
import functools

import numpy as np
import jax
import jax.numpy as jnp
from jax import lax
from jax.experimental import pallas as pl
from jax.experimental.pallas import tpu as pltpu

D_MODEL = 1024
BATCH = 16
SEQ = 2048
DEPTH = 2
GRID_W = 64
CTX_LEN = 256
HEAD_DIM = 64
NA_WIN_ROWS = 8
NA_WIN_COLS = 16
D_FF = 2816
ROPE_THETA = 10000.0
EPS = 1e-6

LANES = 128
N_LANE_TILES = D_MODEL // LANES
COND_ROWS = 24
CTX_MOD_ROW = BATCH
VMEM_LIMIT = 56 * 1024 * 1024

NA_Q_ROWS = 4
NA_K_ROWS = 12
NA_TQ = NA_Q_ROWS * GRID_W
NA_BLOCKS = SEQ // NA_TQ
NA_KBLK = NA_K_ROWS // NA_Q_ROWS

F32 = jnp.float32
BF16 = jnp.bfloat16


def _dot(a, b):
    return jnp.dot(a, b, preferred_element_type=F32)


def _dot_nt(a, b):
    return lax.dot_general(a, b, (((1,), (1,)), ((), ())), preferred_element_type=F32)


def _adaln_kernel(cond_ref, w_ref, b_ref, out_ref):
    cnd = cond_ref[...]
    a = cnd * jax.nn.sigmoid(cnd)
    w = w_ref[0]
    a_hi = a.astype(BF16)
    a_lo = (a - a_hi.astype(F32)).astype(BF16)
    w_hi = w.astype(BF16)
    w_lo = (w - w_hi.astype(F32)).astype(BF16)
    acc = _dot(a_hi, w_hi) + _dot(a_lo, w_hi) + _dot(a_hi, w_lo)
    out_ref[0] = acc + b_ref[0]


def _adaln(cond, ada_w, ada_b):
    n_out = ada_w.shape[-1]
    tn = 1024
    return pl.pallas_call(
        _adaln_kernel,
        grid=(DEPTH, n_out // tn),
        in_specs=[
            pl.BlockSpec((COND_ROWS, D_MODEL), lambda i, j: (0, 0)),
            pl.BlockSpec((1, D_MODEL, tn), lambda i, j: (i, 0, j)),
            pl.BlockSpec((1, 1, tn), lambda i, j: (i, 0, j)),
        ],
        out_specs=pl.BlockSpec((1, COND_ROWS, tn), lambda i, j: (i, 0, j)),
        out_shape=jax.ShapeDtypeStruct((DEPTH, COND_ROWS, n_out), F32),
        compiler_params=pltpu.CompilerParams(vmem_limit_bytes=VMEM_LIMIT),
        name="adaln",
    )(cond, ada_w, ada_b.reshape(DEPTH, 1, n_out))


def _rmsnorm_mod(h, g, shift, scale):
    y = h * lax.rsqrt(jnp.mean(h * h, axis=-1, keepdims=True) + EPS) * g
    return y * (1.0 + scale) + shift


def _qkv_kernel(*refs, rope, want_q):
    if rope:
        h_ref, mod_ref, g_ref, w_ref, cos_ref, sa_ref, sb_ref = refs[:7]
        outs = refs[7:]
    else:
        h_ref, mod_ref, g_ref, w_ref = refs[:4]
        outs = refs[4:]
    xn = _rmsnorm_mod(h_ref[0], g_ref[...], mod_ref[0, 0:1, :], mod_ref[0, 1:2, :]).astype(BF16)

    def rotate(t):
        return (t * cos_ref[...] + pltpu.roll(t, LANES - 16, axis=1) * sa_ref[...]
                + pltpu.roll(t, 16, axis=1) * sb_ref[...])

    parts = ("q", "k", "v") if want_q else ("k", "v")
    for name, o_ref in zip(parts, outs):
        col = {"q": 0, "k": D_MODEL, "v": 2 * D_MODEL}[name]
        acc = _dot(xn, w_ref[:, col:col + D_MODEL])
        if name == "q":
            acc = acc * (HEAD_DIM ** -0.5)
        if rope and name != "v":
            for c in range(N_LANE_TILES):
                sl = slice(c * LANES, (c + 1) * LANES)
                o_ref[0, :, sl] = rotate(acc[:, sl]).astype(BF16)
        else:
            o_ref[0] = acc.astype(BF16)


def _qkv(h, mod, g, w, rope_tabs, *, tm, want_q=True):
    G, R, _ = h.shape
    rope = rope_tabs is not None
    row_spec = pl.BlockSpec((1, tm, D_MODEL), lambda b, t: (b, t, 0))
    in_specs = [
        row_spec,
        pl.BlockSpec((1, 6, D_MODEL), lambda b, t: (b, 0, 0)),
        pl.BlockSpec((1, D_MODEL), lambda b, t: (0, 0)),
        pl.BlockSpec((D_MODEL, 3 * D_MODEL), lambda b, t: (0, 0)),
    ]
    args = [h, mod, g, w]
    if rope:
        in_specs += [pl.BlockSpec((tm, LANES), lambda b, t: (t, 0))] * 3
        args += list(rope_tabs)
    n_out = 3 if want_q else 2
    return pl.pallas_call(
        functools.partial(_qkv_kernel, rope=rope, want_q=want_q),
        grid=(G, R // tm),
        in_specs=in_specs,
        out_specs=[row_spec] * n_out,
        out_shape=[jax.ShapeDtypeStruct((G, R, D_MODEL), BF16)] * n_out,
        compiler_params=pltpu.CompilerParams(vmem_limit_bytes=VMEM_LIMIT),
        name="qkv_rope" if rope else "qkv",
    )(*args)


def _rope_tables():
    t = jnp.arange(SEQ)
    row = (t // GRID_W).astype(F32)
    col = (t % GRID_W).astype(F32)
    half = HEAD_DIM // 2
    freqs = 1.0 / (ROPE_THETA ** (jnp.arange(0, half, 2, dtype=F32) / half))
    ar = row[:, None] * freqs
    ac = col[:, None] * freqs
    ang = jnp.concatenate([ar, ar, ac, ac], axis=-1)
    cos, sin = jnp.cos(ang), jnp.sin(ang)
    first = (np.arange(HEAD_DIM) % half) < half // 2
    sa = jnp.where(first, -sin, 0.0)
    sb = jnp.where(first, 0.0, sin)
    rep = LANES // HEAD_DIM
    return tuple(jnp.tile(x, (1, rep)) for x in (cos, sa, sb))


def _stack_halves(q):
    lane = lax.broadcasted_iota(jnp.int32, q.shape, 1)
    zero = jnp.zeros_like(q)
    return jnp.concatenate([jnp.where(lane < HEAD_DIM, q, zero),
                            jnp.where(lane >= HEAD_DIM, q, zero)], axis=0)


def _softmax_parts(s_parts):
    m = s_parts[0].max(axis=-1, keepdims=True)
    for s in s_parts[1:]:
        m = jnp.maximum(m, s.max(axis=-1, keepdims=True))
    e_parts = [jnp.exp(s - m) for s in s_parts]
    l = e_parts[0].sum(axis=-1, keepdims=True)
    for e in e_parts[1:]:
        l = l + e.sum(axis=-1, keepdims=True)
    return e_parts, 1.0 / l


def _diff_attn_kernel(*refs, nseg, lam_init):
    lam_ref, subln_ref, q_ref = refs[:3]
    k_refs = refs[3:3 + nseg]
    v_refs = refs[3 + nseg:3 + 2 * nseg]
    o_ref = refs[3 + 2 * nseg]
    tq = q_ref.shape[1]

    lp = lam_ref[...]
    lam = (jnp.exp(jnp.sum(lp[0:1] * lp[1:2], axis=-1, keepdims=True))
           - jnp.exp(jnp.sum(lp[2:3] * lp[3:4], axis=-1, keepdims=True)) + lam_init)

    qq = _stack_halves(q_ref[0])
    e_parts, inv_l = _softmax_parts([_dot_nt(qq, k_ref[0]) for k_ref in k_refs])
    w1 = inv_l[:tq]
    w2 = inv_l[tq:] * lam
    o = None
    for e, v_ref in zip(e_parts, v_refs):
        p = (e[:tq] * w1 - e[tq:] * w2).astype(BF16)
        pv = _dot(p, v_ref[0])
        o = pv if o is None else o + pv
    on = o * lax.rsqrt(jnp.mean(o * o, axis=-1, keepdims=True) + EPS) * subln_ref[...]
    o_ref[0] = (on * (1.0 - lam_init)).astype(BF16)


def _diff_attn(lam_params, subln, q, ks, vs, *, tq, lam_init):
    B, N, _ = q.shape
    nseg = len(ks)
    q_spec = pl.BlockSpec((1, tq, LANES), lambda b, h, i: (b, i, h))
    kv_specs = [pl.BlockSpec((1, k.shape[1], LANES), lambda b, h, i: (b, 0, h)) for k in ks]
    return pl.pallas_call(
        functools.partial(_diff_attn_kernel, nseg=nseg, lam_init=lam_init),
        grid=(B, N_LANE_TILES, N // tq),
        in_specs=[
            pl.BlockSpec((4, HEAD_DIM), lambda b, h, i: (0, 0)),
            pl.BlockSpec((1, LANES), lambda b, h, i: (0, 0)),
            q_spec,
        ] + kv_specs + kv_specs,
        out_specs=q_spec,
        out_shape=jax.ShapeDtypeStruct((B, N, D_MODEL), BF16),
        compiler_params=pltpu.CompilerParams(vmem_limit_bytes=VMEM_LIMIT),
        name="diff_attn_%dseg" % nseg,
    )(lam_params, subln, q, *ks, *vs)


def _na_window_start(j):
    return jnp.clip(j - 1, 0, NA_BLOCKS - NA_KBLK)


def _na_bias_type(j):
    return jnp.where(j == 0, 0, jnp.where(j == NA_BLOCKS - 1, 2, 1))


def _na_bias_indices():
    rows = SEQ // GRID_W
    a = np.arange(NA_TQ) // GRID_W
    qc = np.arange(NA_TQ) % GRID_W
    c = np.arange(NA_K_ROWS * GRID_W) // GRID_W
    kc = np.arange(NA_K_ROWS * GRID_W) % GRID_W
    per_block = []
    for j in range(NA_BLOCKS):
        r = NA_Q_ROWS * j + a
        kr = NA_Q_ROWS * int(np.clip(j - 1, 0, NA_BLOCKS - NA_KBLK)) + c
        rs = np.clip(r - NA_WIN_ROWS // 2, 0, rows - NA_WIN_ROWS)
        cs = np.clip(qc - NA_WIN_COLS // 2, 0, GRID_W - NA_WIN_COLS)
        valid = ((kr[None, :] >= rs[:, None]) & (kr[None, :] < rs[:, None] + NA_WIN_ROWS)
                 & (kc[None, :] >= cs[:, None]) & (kc[None, :] < cs[:, None] + NA_WIN_COLS))
        dr = np.clip(kr[None, :] - r[:, None] + NA_WIN_ROWS - 1, 0, 2 * NA_WIN_ROWS - 2)
        dc = np.clip(kc[None, :] - qc[:, None] + NA_WIN_COLS - 1, 0, 2 * NA_WIN_COLS - 2)
        per_block.append((valid, dr, dc))
    for j in range(2, NA_BLOCKS - 1):
        assert all(np.array_equal(x, y) for x, y in zip(per_block[1], per_block[j]))
    picks = [per_block[0], per_block[1], per_block[NA_BLOCKS - 1]]
    return tuple(np.stack([p[i] for p in picks]) for i in range(3))


def _na_bias_table(rpb):
    valid, dr, dc = _na_bias_indices()
    n_dr = 2 * NA_WIN_ROWS - 1
    dc_blk = dc[0, :GRID_W, :GRID_W]
    col_ok = valid[0, :GRID_W, :GRID_W]
    onehot = (dc_blk[None] == np.arange(2 * NA_WIN_COLS - 1)[:, None, None]).astype(np.float32)
    toep = jnp.einsum('hdj,jqk->hdqk', rpb, jnp.asarray(onehot), precision=lax.Precision.HIGHEST)
    toep = jnp.where(jnp.asarray(col_ok), toep, -jnp.inf)
    masked = jnp.full(toep.shape[:1] + toep.shape[2:], -jnp.inf, F32)
    kinds = []
    for t in range(3):
        rows = []
        for a in range(NA_Q_ROWS):
            blocks = []
            for c in range(NA_K_ROWS):
                q0, k0 = a * GRID_W, c * GRID_W
                sub_valid = valid[t, q0:q0 + GRID_W, k0:k0 + GRID_W]
                if sub_valid.any():
                    assert np.array_equal(sub_valid, col_ok)
                    d = int(dr[t, q0, k0])
                    assert 0 <= d < n_dr and np.all(dr[t, q0:q0 + GRID_W, k0:k0 + GRID_W] == d)
                    blocks.append(toep[:, d])
                else:
                    blocks.append(masked)
            rows.append(jnp.concatenate(blocks, axis=-1))
        kinds.append(jnp.concatenate(rows, axis=-2))
    tab = jnp.stack(kinds)
    return tab.reshape(3, N_LANE_TILES, 2, NA_TQ, NA_K_ROWS * GRID_W)


def _nbr_attn_kernel(q_ref, k0_ref, k1_ref, k2_ref, kc_ref, v0_ref, v1_ref, v2_ref, vc_ref,
                     bias_ref, o_ref):
    qq = _stack_halves(q_ref[0])
    bias = bias_ref[0, 0].reshape(2 * NA_TQ, NA_K_ROWS * GRID_W)
    s_parts = []
    for d, k_ref in enumerate((k0_ref, k1_ref, k2_ref)):
        s_parts.append(_dot_nt(qq, k_ref[0]) + bias[:, d * NA_TQ:(d + 1) * NA_TQ])
    s_parts.append(_dot_nt(qq, kc_ref[0]))
    e_parts, inv_l = _softmax_parts(s_parts)
    o2 = None
    for e, v_ref in zip(e_parts, (v0_ref, v1_ref, v2_ref, vc_ref)):
        pv = _dot((e * inv_l).astype(BF16), v_ref[0])
        o2 = pv if o2 is None else o2 + pv
    lane = lax.broadcasted_iota(jnp.int32, (NA_TQ, LANES), 1)
    o_ref[0] = jnp.where(lane < HEAD_DIM, o2[:NA_TQ], o2[NA_TQ:]).astype(BF16)


def _nbr_attn(q, k, v, kc, vc, bias_tab):
    B = q.shape[0]
    q_spec = pl.BlockSpec((1, NA_TQ, LANES), lambda hp, j, b: (b, j, hp))

    def win_spec(d):
        return pl.BlockSpec((1, NA_TQ, LANES), lambda hp, j, b: (b, _na_window_start(j) + d, hp))

    ctx_spec = pl.BlockSpec((1, CTX_LEN, LANES), lambda hp, j, b: (b, 0, hp))
    bias_spec = pl.BlockSpec((1, 1, 2, NA_TQ, NA_K_ROWS * GRID_W),
                             lambda hp, j, b: (_na_bias_type(j), hp, 0, 0, 0))
    wins = [win_spec(d) for d in range(NA_KBLK)]
    return pl.pallas_call(
        _nbr_attn_kernel,
        grid=(N_LANE_TILES, NA_BLOCKS, B),
        in_specs=[q_spec] + wins + [ctx_spec] + wins + [ctx_spec, bias_spec],
        out_specs=q_spec,
        out_shape=jax.ShapeDtypeStruct(q.shape, BF16),
        compiler_params=pltpu.CompilerParams(vmem_limit_bytes=VMEM_LIMIT),
        name="nbr_attn",
    )(q, k, k, k, kc, v, v, v, vc, bias_tab)


FFN_CHUNKS = ((0, 1024), (1024, 1024), (2048, 768))


def _out_ffn_kernel(*refs, final):
    o_ref, h_ref, mod_ref, g_ref, wo_ref, wgu_ref, wd_ref = refs[:7]
    if final:
        gfin_ref, out_ref = refs[7:]
    else:
        (out_ref,) = refs[7:]
    h1 = h_ref[0] + mod_ref[0, 2:3, :] * _dot(o_ref[0], wo_ref[...])
    xn = _rmsnorm_mod(h1, g_ref[...], mod_ref[0, 3:4, :], mod_ref[0, 4:5, :]).astype(BF16)
    acc = None
    for start, size in FFN_CHUNKS:
        gate = _dot(xn, wgu_ref[:, start:start + size])
        up = _dot(xn, wgu_ref[:, D_FF + start:D_FF + start + size])
        act = (gate * jax.nn.sigmoid(gate) * up).astype(BF16)
        part = _dot(act, wd_ref[start:start + size, :])
        acc = part if acc is None else acc + part
    h2 = h1 + mod_ref[0, 5:6, :] * acc
    if final:
        h2 = h2 * lax.rsqrt(jnp.mean(h2 * h2, axis=-1, keepdims=True) + EPS) * gfin_ref[...]
    out_ref[0] = h2


def _out_ffn(o, h, mod, g, wo, wgu, wd, gfin, *, tm):
    G, R, _ = h.shape
    final = gfin is not None
    row_spec = pl.BlockSpec((1, tm, D_MODEL), lambda b, t: (b, t, 0))

    def resident(shape):
        return pl.BlockSpec(shape, lambda b, t: (0, 0), pipeline_mode=pl.Buffered(1))

    in_specs = [
        row_spec,
        row_spec,
        pl.BlockSpec((1, 6, D_MODEL), lambda b, t: (b, 0, 0)),
        pl.BlockSpec((1, D_MODEL), lambda b, t: (0, 0)),
        resident((D_MODEL, D_MODEL)),
        resident((D_MODEL, 2 * D_FF)),
        resident((D_FF, D_MODEL)),
    ]
    args = [o, h, mod, g, wo, wgu, wd]
    if final:
        in_specs.append(pl.BlockSpec((1, D_MODEL), lambda b, t: (0, 0)))
        args.append(gfin)
    return pl.pallas_call(
        functools.partial(_out_ffn_kernel, final=final),
        grid=(G, R // tm),
        in_specs=in_specs,
        out_specs=row_spec,
        out_shape=jax.ShapeDtypeStruct(h.shape, F32),
        compiler_params=pltpu.CompilerParams(vmem_limit_bytes=VMEM_LIMIT),
        name="out_ffn_final" if final else "out_ffn",
    )(*args)


def kernel(x, c, ctx, c_ctx, ada_w, ada_b, norm_mix, norm_ffn, da_wqkv, da_lambda_q1, da_lambda_k1,
           da_lambda_q2, da_lambda_k2, da_subln, da_wo, na_wqkv, na_rpb, na_wo, ffn_w_gate_up,
           ffn_w_down, norm_final):
    B, N, D = x.shape
    L = ctx.shape[1]
    TM = 512

    cond = jnp.zeros((COND_ROWS, D), F32).at[:B].set(c).at[CTX_MOD_ROW].set(c_ctx)
    mods = _adaln(cond, ada_w, ada_b).reshape(DEPTH, COND_ROWS, 6, D)

    h = x
    hc = ctx.reshape(1, B * L, D)
    rope_tabs = _rope_tables()

    mod_l = mods[0, :B]
    mod_c = mods[0, CTX_MOD_ROW:CTX_MOD_ROW + 1]
    g_mix = norm_mix[0].reshape(1, D)
    g_ffn = norm_ffn[0].reshape(1, D)
    wqkv = da_wqkv[0].astype(BF16)
    q_l, k_l, v_l = _qkv(h, mod_l, g_mix, wqkv, rope_tabs, tm=TM)
    q_c, k_c, v_c = (t.reshape(B, L, D) for t in _qkv(hc, mod_c, g_mix, wqkv, None, tm=TM))
    lam_params = jnp.stack([da_lambda_q1[0], da_lambda_k1[0], da_lambda_q2[0], da_lambda_k2[0]])
    subln = da_subln[0].reshape(1, LANES)
    lam_init = 0.8 - 0.6 * float(np.exp(-0.3 * 0))
    o_l = _diff_attn(lam_params, subln, q_l, [k_c, k_l], [v_c, v_l], tq=256, lam_init=lam_init)
    o_c = _diff_attn(lam_params, subln, q_c, [k_c], [v_c], tq=L, lam_init=lam_init)
    wo = da_wo[0].astype(BF16)
    wgu = ffn_w_gate_up[0].astype(BF16)
    wd = ffn_w_down[0].astype(BF16)
    h = _out_ffn(o_l, h, mod_l, g_ffn, wo, wgu, wd, None, tm=TM)
    hc = _out_ffn(o_c.reshape(1, B * L, D), hc, mod_c, g_ffn, wo, wgu, wd, None, tm=TM)

    mod_l = mods[1, :B]
    mod_c = mods[1, CTX_MOD_ROW:CTX_MOD_ROW + 1]
    g_mix = norm_mix[1].reshape(1, D)
    g_ffn = norm_ffn[1].reshape(1, D)
    wqkv = na_wqkv[0].astype(BF16)
    q_l, k_l, v_l = _qkv(h, mod_l, g_mix, wqkv, None, tm=TM)
    k_c, v_c = (t.reshape(B, L, D) for t in _qkv(hc, mod_c, g_mix, wqkv, None, tm=TM, want_q=False))
    o_l = _nbr_attn(q_l, k_l, v_l, k_c, v_c, _na_bias_table(na_rpb[0]))
    return _out_ffn(o_l, h, mod_l, g_ffn, na_wo[0].astype(BF16), ffn_w_gate_up[1].astype(BF16),
                    ffn_w_down[1].astype(BF16), norm_final.reshape(1, D), tm=TM)
```

```python
import functools

import numpy as np
import jax
import jax.numpy as jnp
from jax import lax
from jax.experimental import pallas as pl
from jax.experimental.pallas import tpu as pltpu

D_MODEL = 1024
BATCH = 16
SEQ = 2048
DEPTH = 2
GRID_W = 64
CTX_LEN = 256
HEAD_DIM = 64
NA_WIN_ROWS = 8
NA_WIN_COLS = 16
D_FF = 2816
ROPE_THETA = 10000.0
EPS = 1e-6
LOG2E = 1.4426950408889634
Q_SCALE = HEAD_DIM ** -0.5 * LOG2E

LANES = 128
N_LANE_TILES = D_MODEL // LANES
COND_ROWS = 24
CTX_MOD_ROW = BATCH
VMEM_LIMIT = 56 * 1024 * 1024

NA_Q_ROWS = 4
NA_K_ROWS = 12
NA_TQ = NA_Q_ROWS * GRID_W
NA_BLOCKS = SEQ // NA_TQ
NA_KBLK = NA_K_ROWS // NA_Q_ROWS
NA_BLOCKS_PER_ITER = 8
DA_TQ = 256
DA_BLOCKS_PER_ITER = 8

F32 = jnp.float32
BF16 = jnp.bfloat16


def _dot(a, b):
    return jnp.dot(a, b, preferred_element_type=F32)


def _dot_nt(a, b):
    return lax.dot_general(a, b, (((1,), (1,)), ((), ())), preferred_element_type=F32)


def _adaln_kernel(cond_ref, w_ref, b_ref, out_ref):
    cnd = cond_ref[...]
    a = cnd * jax.nn.sigmoid(cnd)
    w = w_ref[0]
    a_hi = a.astype(BF16)
    a_lo = (a - a_hi.astype(F32)).astype(BF16)
    w_hi = w.astype(BF16)
    w_lo = (w - w_hi.astype(F32)).astype(BF16)
    acc = _dot(a_hi, w_hi) + _dot(a_lo, w_hi) + _dot(a_hi, w_lo)
    out_ref[0] = acc + b_ref[0]


def _adaln(cond, ada_w, ada_b):
    n_out = ada_w.shape[-1]
    tn = 1024
    return pl.pallas_call(
        _adaln_kernel,
        grid=(DEPTH, n_out // tn),
        in_specs=[
            pl.BlockSpec((COND_ROWS, D_MODEL), lambda i, j: (0, 0)),
            pl.BlockSpec((1, D_MODEL, tn), lambda i, j: (i, 0, j)),
            pl.BlockSpec((1, 1, tn), lambda i, j: (i, 0, j)),
        ],
        out_specs=pl.BlockSpec((1, COND_ROWS, tn), lambda i, j: (i, 0, j)),
        out_shape=jax.ShapeDtypeStruct((DEPTH, COND_ROWS, n_out), F32),
        compiler_params=pltpu.CompilerParams(vmem_limit_bytes=VMEM_LIMIT),
        name="adaln",
    )(cond, ada_w, ada_b.reshape(DEPTH, 1, n_out))


def _rmsnorm_mod(h, g, shift, scale):
    y = h * lax.rsqrt(jnp.mean(h * h, axis=-1, keepdims=True) + EPS) * g
    return y * (1.0 + scale) + shift


def _qkv_kernel(*refs, rope, want_q):
    if rope:
        h_ref, mod_ref, g_ref, w_ref, cos_ref, sa_ref, sb_ref = refs[:7]
        outs = refs[7:]
    else:
        h_ref, mod_ref, g_ref, w_ref = refs[:4]
        outs = refs[4:]
    xn = _rmsnorm_mod(h_ref[0], g_ref[...], mod_ref[0, 0:1, :], mod_ref[0, 1:2, :]).astype(BF16)

    def rotate(t):
        return (t * cos_ref[...] + pltpu.roll(t, LANES - 16, axis=1) * sa_ref[...]
                + pltpu.roll(t, 16, axis=1) * sb_ref[...])

    parts = ("q", "k", "v") if want_q else ("k", "v")
    for name, o_ref in zip(parts, outs):
        col = {"q": 0, "k": D_MODEL, "v": 2 * D_MODEL}[name]
        acc = _dot(xn, w_ref[:, col:col + D_MODEL])
        if name == "q":
            acc = acc * Q_SCALE
        if rope and name != "v":
            for c in range(N_LANE_TILES):
                sl = slice(c * LANES, (c + 1) * LANES)
                o_ref[0, :, sl] = rotate(acc[:, sl]).astype(BF16)
        else:
            o_ref[0] = acc.astype(BF16)


def _qkv(h, mod, g, w, rope_tabs, *, tm, want_q=True):
    G, R, _ = h.shape
    rope = rope_tabs is not None
    row_spec = pl.BlockSpec((1, tm, D_MODEL), lambda b, t: (b, t, 0))
    in_specs = [
        row_spec,
        pl.BlockSpec((1, 6, D_MODEL), lambda b, t: (b, 0, 0)),
        pl.BlockSpec((1, D_MODEL), lambda b, t: (0, 0)),
        pl.BlockSpec((D_MODEL, 3 * D_MODEL), lambda b, t: (0, 0)),
    ]
    args = [h, mod, g, w]
    if rope:
        in_specs += [pl.BlockSpec((tm, LANES), lambda b, t: (t, 0))] * 3
        args += list(rope_tabs)
    n_out = 3 if want_q else 2
    return pl.pallas_call(
        functools.partial(_qkv_kernel, rope=rope, want_q=want_q),
        grid=(G, R // tm),
        in_specs=in_specs,
        out_specs=[row_spec] * n_out,
        out_shape=[jax.ShapeDtypeStruct((G, R, D_MODEL), BF16)] * n_out,
        compiler_params=pltpu.CompilerParams(vmem_limit_bytes=VMEM_LIMIT),
        name="qkv_rope" if rope else "qkv",
    )(*args)


def _rope_tables():
    t = jnp.arange(SEQ)
    row = (t // GRID_W).astype(F32)
    col = (t % GRID_W).astype(F32)
    half = HEAD_DIM // 2
    freqs = 1.0 / (ROPE_THETA ** (jnp.arange(0, half, 2, dtype=F32) / half))
    ar = row[:, None] * freqs
    ac = col[:, None] * freqs
    ang = jnp.concatenate([ar, ar, ac, ac], axis=-1)
    cos, sin = jnp.cos(ang), jnp.sin(ang)
    first = (np.arange(HEAD_DIM) % half) < half // 2
    sa = jnp.where(first, -sin, 0.0)
    sb = jnp.where(first, 0.0, sin)
    rep = LANES // HEAD_DIM
    return tuple(jnp.tile(x, (1, rep)) for x in (cos, sa, sb))


def _stack_halves(q):
    lane = lax.broadcasted_iota(jnp.int32, q.shape, 1)
    zero = jnp.zeros_like(q)
    return jnp.concatenate([jnp.where(lane < HEAD_DIM, q, zero),
                            jnp.where(lane >= HEAD_DIM, q, zero)], axis=0)


def _softmax_exp2_parts(s_parts):
    m = s_parts[0].max(axis=-1, keepdims=True)
    for s in s_parts[1:]:
        m = jnp.maximum(m, s.max(axis=-1, keepdims=True))
    e_parts = [jnp.exp2(s - m) for s in s_parts]
    l = e_parts[0].sum(axis=-1, keepdims=True)
    for e in e_parts[1:]:
        l = l + e.sum(axis=-1, keepdims=True)
    return e_parts, l


def _pv_parts(e_parts, v_parts):
    acc = None
    for e, v in zip(e_parts, v_parts):
        pv = _dot(e.astype(BF16), v)
        acc = pv if acc is None else acc + pv
    return acc


def _for_blocks(n_blocks, per_iter, fn):
    if per_iter == n_blocks:
        for j in range(n_blocks):
            fn(j)
        return

    def body(it, carry):
        for sub in range(per_iter):
            fn(it * per_iter + sub)
        return carry

    lax.fori_loop(0, n_blocks // per_iter, body, 0)


def _diff_attn_kernel(*refs, nseg, lam_init, tq, per_iter):
    lam_ref, subln_ref, q_ref = refs[:3]
    k_refs = refs[3:3 + nseg]
    v_refs = refs[3 + nseg:3 + 2 * nseg]
    o_ref = refs[3 + 2 * nseg]

    lp = lam_ref[...]
    lam = (jnp.exp(jnp.sum(lp[0:1] * lp[1:2], axis=-1, keepdims=True))
           - jnp.exp(jnp.sum(lp[2:3] * lp[3:4], axis=-1, keepdims=True)) + lam_init)
    subln = subln_ref[...] * (1.0 - lam_init)

    def block(j):
        row0 = j * tq if isinstance(j, int) else pl.multiple_of(j * tq, tq)
        qq = _stack_halves(q_ref[0, pl.ds(row0, tq), :])
        e_parts, l = _softmax_exp2_parts([_dot_nt(qq, k_ref[0]) for k_ref in k_refs])
        acc = _pv_parts(e_parts, [v_ref[0] for v_ref in v_refs])
        w = 1.0 / l
        o = acc[:tq] * w[:tq] - acc[tq:] * (w[tq:] * lam)
        on = o * lax.rsqrt(jnp.mean(o * o, axis=-1, keepdims=True) + EPS) * subln
        o_ref[0, pl.ds(row0, tq), :] = on.astype(BF16)

    _for_blocks(q_ref.shape[1] // tq, per_iter, block)


def _diff_attn(lam_params, subln, q, ks, vs, *, tq, per_iter, lam_init):
    B, N, _ = q.shape
    nseg = len(ks)

    def head_spec(n):
        return pl.BlockSpec((1, n, LANES), lambda b, h: (b, 0, h))

    kv_specs = [head_spec(k.shape[1]) for k in ks]
    return pl.pallas_call(
        functools.partial(_diff_attn_kernel, nseg=nseg, lam_init=lam_init, tq=tq,
                          per_iter=per_iter),
        grid=(B, N_LANE_TILES),
        in_specs=[
            pl.BlockSpec((4, HEAD_DIM), lambda b, h: (0, 0)),
            pl.BlockSpec((1, LANES), lambda b, h: (0, 0)),
            head_spec(N),
        ] + kv_specs + kv_specs,
        out_specs=head_spec(N),
        out_shape=jax.ShapeDtypeStruct((B, N, D_MODEL), BF16),
        compiler_params=pltpu.CompilerParams(vmem_limit_bytes=VMEM_LIMIT),
        name="diff_attn_%dseg" % nseg,
    )(lam_params, subln, q, *ks, *vs)


def _na_window_start(j):
    if isinstance(j, int):
        return min(max(j - 1, 0), NA_BLOCKS - NA_KBLK)
    return jnp.clip(j - 1, 0, NA_BLOCKS - NA_KBLK)


def _na_bias_type(j):
    if isinstance(j, int):
        return 0 if j == 0 else (2 if j == NA_BLOCKS - 1 else 1)
    return jnp.where(j == 0, 0, jnp.where(j == NA_BLOCKS - 1, 2, 1))


def _na_bias_indices():
    rows = SEQ // GRID_W
    a = np.arange(NA_TQ) // GRID_W
    qc = np.arange(NA_TQ) % GRID_W
    c = np.arange(NA_K_ROWS * GRID_W) // GRID_W
    kc = np.arange(NA_K_ROWS * GRID_W) % GRID_W
    per_block = []
    for j in range(NA_BLOCKS):
        r = NA_Q_ROWS * j + a
        kr = NA_Q_ROWS * int(np.clip(j - 1, 0, NA_BLOCKS - NA_KBLK)) + c
        rs = np.clip(r - NA_WIN_ROWS // 2, 0, rows - NA_WIN_ROWS)
        cs = np.clip(qc - NA_WIN_COLS // 2, 0, GRID_W - NA_WIN_COLS)
        valid = ((kr[None, :] >= rs[:, None]) & (kr[None, :] < rs[:, None] + NA_WIN_ROWS)
                 & (kc[None, :] >= cs[:, None]) & (kc[None, :] < cs[:, None] + NA_WIN_COLS))
        dr = np.clip(kr[None, :] - r[:, None] + NA_WIN_ROWS - 1, 0, 2 * NA_WIN_ROWS - 2)
        dc = np.clip(kc[None, :] - qc[:, None] + NA_WIN_COLS - 1, 0, 2 * NA_WIN_COLS - 2)
        per_block.append((valid, dr, dc))
    for j in range(2, NA_BLOCKS - 1):
        assert all(np.array_equal(x, y) for x, y in zip(per_block[1], per_block[j]))
    picks = [per_block[0], per_block[1], per_block[NA_BLOCKS - 1]]
    return tuple(np.stack([p[i] for p in picks]) for i in range(3))


def _na_bias_table(rpb):
    valid, dr, dc = _na_bias_indices()
    n_dr = 2 * NA_WIN_ROWS - 1
    dc_blk = dc[0, :GRID_W, :GRID_W]
    col_ok = valid[0, :GRID_W, :GRID_W]
    onehot = (dc_blk[None] == np.arange(2 * NA_WIN_COLS - 1)[:, None, None]).astype(np.float32)
    toep = jnp.einsum('hdj,jqk->hdqk', rpb, jnp.asarray(onehot), precision=lax.Precision.HIGHEST)
    toep = jnp.where(jnp.asarray(col_ok), toep, -jnp.inf)
    masked = jnp.full(toep.shape[:1] + toep.shape[2:], -jnp.inf, F32)
    kinds = []
    for t in range(3):
        rows = []
        for a in range(NA_Q_ROWS):
            blocks = []
            for c in range(NA_K_ROWS):
                q0, k0 = a * GRID_W, c * GRID_W
                sub_valid = valid[t, q0:q0 + GRID_W, k0:k0 + GRID_W]
                if sub_valid.any():
                    assert np.array_equal(sub_valid, col_ok)
                    d = int(dr[t, q0, k0])
                    assert 0 <= d < n_dr and np.all(dr[t, q0:q0 + GRID_W, k0:k0 + GRID_W] == d)
                    blocks.append(toep[:, d])
                else:
                    blocks.append(masked)
            rows.append(jnp.concatenate(blocks, axis=-1))
        kinds.append(jnp.concatenate(rows, axis=-2))
    tab = jnp.stack(kinds) * LOG2E
    return tab.reshape(3, N_LANE_TILES, 2, NA_TQ, NA_K_ROWS * GRID_W)


def _nbr_attn_kernel(q_ref, k_ref, v_ref, kc_ref, vc_ref, bias_ref, o_ref):
    n_win = NA_K_ROWS * GRID_W
    lane = lax.broadcasted_iota(jnp.int32, (NA_TQ, LANES), 1)

    def block(j):
        row0 = j * NA_TQ
        win0 = _na_window_start(j) * NA_TQ
        if not isinstance(j, int):
            row0, win0 = pl.multiple_of(row0, NA_TQ), pl.multiple_of(win0, NA_TQ)
        qq = _stack_halves(q_ref[0, pl.ds(row0, NA_TQ), :])
        bias = bias_ref[_na_bias_type(j), 0].reshape(2 * NA_TQ, n_win)
        s_band = _dot_nt(qq, k_ref[0, pl.ds(win0, n_win), :]) + bias
        s_ctx = _dot_nt(qq, kc_ref[0])
        e_parts, l = _softmax_exp2_parts([s_band, s_ctx])
        o2 = _pv_parts(e_parts, [v_ref[0, pl.ds(win0, n_win), :], vc_ref[0]]) * (1.0 / l)
        o_ref[0, pl.ds(row0, NA_TQ), :] = jnp.where(
            lane < HEAD_DIM, o2[:NA_TQ], o2[NA_TQ:]).astype(BF16)

    _for_blocks(NA_BLOCKS, NA_BLOCKS_PER_ITER, block)


def _nbr_attn(q, k, v, kc, vc, bias_tab):
    B, N, _ = q.shape

    def head_spec(n):
        return pl.BlockSpec((1, n, LANES), lambda hp, b: (b, 0, hp))

    bias_spec = pl.BlockSpec((3, 1, 2, NA_TQ, NA_K_ROWS * GRID_W), lambda hp, b: (0, hp, 0, 0, 0))
    return pl.pallas_call(
        _nbr_attn_kernel,
        grid=(N_LANE_TILES, B),
        in_specs=[head_spec(N), head_spec(N), head_spec(N), head_spec(CTX_LEN), head_spec(CTX_LEN),
                  bias_spec],
        out_specs=head_spec(N),
        out_shape=jax.ShapeDtypeStruct(q.shape, BF16),
        compiler_params=pltpu.CompilerParams(vmem_limit_bytes=VMEM_LIMIT),
        name="nbr_attn",
    )(q, k, v, kc, vc, bias_tab)


FFN_CHUNKS = ((0, 1024), (1024, 1024), (2048, 768))


def _out_ffn_kernel(*refs, final):
    o_ref, h_ref, mod_ref, g_ref, wo_ref, wgu_ref, wd_ref = refs[:7]
    if final:
        gfin_ref, out_ref = refs[7:]
    else:
        (out_ref,) = refs[7:]
    h1 = h_ref[0] + mod_ref[0, 2:3, :] * _dot(o_ref[0], wo_ref[...])
    xn = _rmsnorm_mod(h1, g_ref[...], mod_ref[0, 3:4, :], mod_ref[0, 4:5, :]).astype(BF16)
    acc = None
    for start, size in FFN_CHUNKS:
        gate = _dot(xn, wgu_ref[:, start:start + size])
        up = _dot(xn, wgu_ref[:, D_FF + start:D_FF + start + size])
        act = (gate * jax.nn.sigmoid(gate) * up).astype(BF16)
        part = _dot(act, wd_ref[start:start + size, :])
        acc = part if acc is None else acc + part
    h2 = h1 + mod_ref[0, 5:6, :] * acc
    if final:
        h2 = h2 * lax.rsqrt(jnp.mean(h2 * h2, axis=-1, keepdims=True) + EPS) * gfin_ref[...]
    out_ref[0] = h2


def _out_ffn(o, h, mod, g, wo, wgu, wd, gfin, *, tm):
    G, R, _ = h.shape
    final = gfin is not None
    row_spec = pl.BlockSpec((1, tm, D_MODEL), lambda b, t: (b, t, 0))

    def resident(shape):
        return pl.BlockSpec(shape, lambda b, t: (0, 0), pipeline_mode=pl.Buffered(1))

    in_specs = [
        row_spec,
        row_spec,
        pl.BlockSpec((1, 6, D_MODEL), lambda b, t: (b, 0, 0)),
        pl.BlockSpec((1, D_MODEL), lambda b, t: (0, 0)),
        resident((D_MODEL, D_MODEL)),
        resident((D_MODEL, 2 * D_FF)),
        resident((D_FF, D_MODEL)),
    ]
    args = [o, h, mod, g, wo, wgu, wd]
    if final:
        in_specs.append(pl.BlockSpec((1, D_MODEL), lambda b, t: (0, 0)))
        args.append(gfin)
    return pl.pallas_call(
        functools.partial(_out_ffn_kernel, final=final),
        grid=(G, R // tm),
        in_specs=in_specs,
        out_specs=row_spec,
        out_shape=jax.ShapeDtypeStruct(h.shape, F32),
        compiler_params=pltpu.CompilerParams(vmem_limit_bytes=VMEM_LIMIT),
        name="out_ffn_final" if final else "out_ffn",
    )(*args)


def kernel(x, c, ctx, c_ctx, ada_w, ada_b, norm_mix, norm_ffn, da_wqkv, da_lambda_q1, da_lambda_k1,
           da_lambda_q2, da_lambda_k2, da_subln, da_wo, na_wqkv, na_rpb, na_wo, ffn_w_gate_up,
           ffn_w_down, norm_final):
    B, N, D = x.shape
    L = ctx.shape[1]
    TM = 512

    cond = jnp.zeros((COND_ROWS, D), F32).at[:B].set(c).at[CTX_MOD_ROW].set(c_ctx)
    mods = _adaln(cond, ada_w, ada_b).reshape(DEPTH, COND_ROWS, 6, D)

    h = x
    hc = ctx.reshape(1, B * L, D)
    rope_tabs = _rope_tables()

    mod_l = mods[0, :B]
    mod_c = mods[0, CTX_MOD_ROW:CTX_MOD_ROW + 1]
    g_mix = norm_mix[0].reshape(1, D)
    g_ffn = norm_ffn[0].reshape(1, D)
    wqkv = da_wqkv[0].astype(BF16)
    q_l, k_l, v_l = _qkv(h, mod_l, g_mix, wqkv, rope_tabs, tm=TM)
    q_c, k_c, v_c = (t.reshape(B, L, D) for t in _qkv(hc, mod_c, g_mix, wqkv, None, tm=TM))
    lam_params = jnp.stack([da_lambda_q1[0], da_lambda_k1[0], da_lambda_q2[0], da_lambda_k2[0]])
    subln = da_subln[0].reshape(1, LANES)
    lam_init = 0.8 - 0.6 * float(np.exp(-0.3 * 0))
    o_l = _diff_attn(lam_params, subln, q_l, [k_c, k_l], [v_c, v_l], tq=DA_TQ,
                     per_iter=DA_BLOCKS_PER_ITER, lam_init=lam_init)
    o_c = _diff_attn(lam_params, subln, q_c, [k_c], [v_c], tq=L // 2, per_iter=2, lam_init=lam_init)
    wo = da_wo[0].astype(BF16)
    wgu = ffn_w_gate_up[0].astype(BF16)
    wd = ffn_w_down[0].astype(BF16)
    h = _out_ffn(o_l, h, mod_l, g_ffn, wo, wgu, wd, None, tm=TM)
    hc = _out_ffn(o_c.reshape(1, B * L, D), hc, mod_c, g_ffn, wo, wgu, wd, None, tm=TM)

    mod_l = mods[1, :B]
    mod_c = mods[1, CTX_MOD_ROW:CTX_MOD_ROW + 1]
    g_mix = norm_mix[1].reshape(1, D)
    g_ffn = norm_ffn[1].reshape(1, D)
    wqkv = na_wqkv[0].astype(BF16)
    q_l, k_l, v_l = _qkv(h, mod_l, g_mix, wqkv, None, tm=TM)
    k_c, v_c = (t.reshape(B, L, D) for t in _qkv(hc, mod_c, g_mix, wqkv, None, tm=TM, want_q=False))
    o_l = _nbr_attn(q_l, k_l, v_l, k_c, v_c, _na_bias_table(na_rpb[0]))
    return _out_ffn(o_l, h, mod_l, g_ffn, na_wo[0].astype(BF16), ffn_w_gate_up[1].astype(BF16),
                    ffn_w_down[1].astype(BF16), norm_final.reshape(1, D), tm=TM)
```

```python
import functools

import numpy as np
import jax
import jax.numpy as jnp
from jax import lax
from jax.experimental import pallas as pl
from jax.experimental.pallas import tpu as pltpu

D_MODEL = 1024
BATCH = 16
SEQ = 2048
DEPTH = 2
GRID_W = 64
CTX_LEN = 256
HEAD_DIM = 64
NA_WIN_ROWS = 8
NA_WIN_COLS = 16
D_FF = 2816
ROPE_THETA = 10000.0
EPS = 1e-6
LOG2E = 1.4426950408889634
Q_SCALE = HEAD_DIM ** -0.5 * LOG2E

LANES = 128
N_LANE_TILES = D_MODEL // LANES
COND_ROWS = 24
CTX_MOD_ROW = BATCH
VMEM_LIMIT = 56 * 1024 * 1024

NA_Q_ROWS = 4
NA_K_ROWS = 12
NA_TQ = NA_Q_ROWS * GRID_W
NA_BLOCKS = SEQ // NA_TQ
NA_KBLK = NA_K_ROWS // NA_Q_ROWS
DA_BLOCK_ROWS = (128,) + (256,) * 7 + (128,)

F32 = jnp.float32
BF16 = jnp.bfloat16


def _dot(a, b):
    return jnp.dot(a, b, preferred_element_type=F32)


def _dot_nt(a, b):
    return lax.dot_general(a, b, (((1,), (1,)), ((), ())), preferred_element_type=F32)


def _adaln_kernel(cond_ref, w_ref, b_ref, out_ref):
    cnd = cond_ref[...]
    a = cnd * jax.nn.sigmoid(cnd)
    w = w_ref[0]
    a_hi = a.astype(BF16)
    a_lo = (a - a_hi.astype(F32)).astype(BF16)
    w_hi = w.astype(BF16)
    w_lo = (w - w_hi.astype(F32)).astype(BF16)
    acc = _dot(a_hi, w_hi) + _dot(a_lo, w_hi) + _dot(a_hi, w_lo)
    out_ref[0] = acc + b_ref[0]


def _adaln(cond, ada_w, ada_b):
    n_out = ada_w.shape[-1]
    tn = 1024
    return pl.pallas_call(
        _adaln_kernel,
        grid=(DEPTH, n_out // tn),
        in_specs=[
            pl.BlockSpec((COND_ROWS, D_MODEL), lambda i, j: (0, 0)),
            pl.BlockSpec((1, D_MODEL, tn), lambda i, j: (i, 0, j)),
            pl.BlockSpec((1, 1, tn), lambda i, j: (i, 0, j)),
        ],
        out_specs=pl.BlockSpec((1, COND_ROWS, tn), lambda i, j: (i, 0, j)),
        out_shape=jax.ShapeDtypeStruct((DEPTH, COND_ROWS, n_out), F32),
        compiler_params=pltpu.CompilerParams(vmem_limit_bytes=VMEM_LIMIT),
        name="adaln",
    )(cond, ada_w, ada_b.reshape(DEPTH, 1, n_out))


def _rmsnorm_mod(h, g, shift, scale):
    y = h * lax.rsqrt(jnp.mean(h * h, axis=-1, keepdims=True) + EPS) * g
    return y * (1.0 + scale) + shift


def _qkv_kernel(*refs, rope, want_q):
    if rope:
        h_ref, mod_ref, g_ref, w_ref, cos_ref, sa_ref, sb_ref = refs[:7]
        outs = refs[7:]
    else:
        h_ref, mod_ref, g_ref, w_ref = refs[:4]
        outs = refs[4:]
    xn = _rmsnorm_mod(h_ref[0], g_ref[...], mod_ref[0, 0:1, :], mod_ref[0, 1:2, :]).astype(BF16)

    def rotate(t):
        return (t * cos_ref[...] + pltpu.roll(t, LANES - 16, axis=1) * sa_ref[...]
                + pltpu.roll(t, 16, axis=1) * sb_ref[...])

    parts = ("q", "k", "v") if want_q else ("k", "v")
    for name, o_ref in zip(parts, outs):
        col = {"q": 0, "k": D_MODEL, "v": 2 * D_MODEL}[name]
        acc = _dot(xn, w_ref[:, col:col + D_MODEL])
        if name == "q":
            acc = acc * Q_SCALE
        if rope and name != "v":
            for c in range(N_LANE_TILES):
                sl = slice(c * LANES, (c + 1) * LANES)
                o_ref[0, :, sl] = rotate(acc[:, sl]).astype(BF16)
        else:
            o_ref[0] = acc.astype(BF16)


def _qkv(h, mod, g, w, rope_tabs, *, tm, want_q=True):
    G, R, _ = h.shape
    rope = rope_tabs is not None
    row_spec = pl.BlockSpec((1, tm, D_MODEL), lambda b, t: (b, t, 0))
    in_specs = [
        row_spec,
        pl.BlockSpec((1, 6, D_MODEL), lambda b, t: (b, 0, 0)),
        pl.BlockSpec((1, D_MODEL), lambda b, t: (0, 0)),
        pl.BlockSpec((D_MODEL, 3 * D_MODEL), lambda b, t: (0, 0)),
    ]
    args = [h, mod, g, w]
    if rope:
        in_specs += [pl.BlockSpec((tm, LANES), lambda b, t: (t, 0))] * 3
        args += list(rope_tabs)
    n_out = 3 if want_q else 2
    return pl.pallas_call(
        functools.partial(_qkv_kernel, rope=rope, want_q=want_q),
        grid=(G, R // tm),
        in_specs=in_specs,
        out_specs=[row_spec] * n_out,
        out_shape=[jax.ShapeDtypeStruct((G, R, D_MODEL), BF16)] * n_out,
        compiler_params=pltpu.CompilerParams(vmem_limit_bytes=VMEM_LIMIT),
        name="qkv_rope" if rope else "qkv",
    )(*args)


def _rope_tables():
    t = jnp.arange(SEQ)
    row = (t // GRID_W).astype(F32)
    col = (t % GRID_W).astype(F32)
    half = HEAD_DIM // 2
    freqs = 1.0 / (ROPE_THETA ** (jnp.arange(0, half, 2, dtype=F32) / half))
    ar = row[:, None] * freqs
    ac = col[:, None] * freqs
    ang = jnp.concatenate([ar, ar, ac, ac], axis=-1)
    cos, sin = jnp.cos(ang), jnp.sin(ang)
    first = (np.arange(HEAD_DIM) % half) < half // 2
    sa = jnp.where(first, -sin, 0.0)
    sb = jnp.where(first, 0.0, sin)
    rep = LANES // HEAD_DIM
    return tuple(jnp.tile(x, (1, rep)) for x in (cos, sa, sb))


def _stack_halves(q):
    lane = lax.broadcasted_iota(jnp.int32, q.shape, 1)
    zero = jnp.zeros_like(q)
    return jnp.concatenate([jnp.where(lane < HEAD_DIM, q, zero),
                            jnp.where(lane >= HEAD_DIM, q, zero)], axis=0)


def _softmax_exp2_parts(s_parts):
    m = s_parts[0].max(axis=-1, keepdims=True)
    for s in s_parts[1:]:
        m = jnp.maximum(m, s.max(axis=-1, keepdims=True))
    e_parts = [jnp.exp2(s - m) for s in s_parts]
    l = e_parts[0].sum(axis=-1, keepdims=True)
    for e in e_parts[1:]:
        l = l + e.sum(axis=-1, keepdims=True)
    return e_parts, l


def _pv_parts(e_parts, v_parts):
    acc = None
    for e, v in zip(e_parts, v_parts):
        pv = _dot(e.astype(BF16), v)
        acc = pv if acc is None else acc + pv
    return acc


def _diff_attn_kernel(*refs, nseg, lam_init, block_rows, heads):
    lam_ref, subln_ref, q_ref = refs[:3]
    k_refs = refs[3:3 + nseg]
    v_refs = refs[3 + nseg:3 + 2 * nseg]
    o_ref = refs[3 + 2 * nseg]

    lp = lam_ref[...]
    lam = (jnp.exp(jnp.sum(lp[0:1] * lp[1:2], axis=-1, keepdims=True))
           - jnp.exp(jnp.sum(lp[2:3] * lp[3:4], axis=-1, keepdims=True)) + lam_init)
    subln = subln_ref[...] * (1.0 - lam_init)

    def block(head, row0, tq):
        cols = slice(head * LANES, (head + 1) * LANES)
        qq = _stack_halves(q_ref[0, row0:row0 + tq, cols])
        e_parts, l = _softmax_exp2_parts([_dot_nt(qq, k_ref[0, :, cols]) for k_ref in k_refs])
        acc = _pv_parts(e_parts, [v_ref[0, :, cols] for v_ref in v_refs])
        w = 1.0 / l
        o = acc[:tq] * w[:tq] - acc[tq:] * (w[tq:] * lam)
        on = o * lax.rsqrt(jnp.mean(o * o, axis=-1, keepdims=True) + EPS) * subln
        o_ref[0, row0:row0 + tq, cols] = on.astype(BF16)

    for head in range(heads):
        row0 = 0
        for tq in block_rows:
            block(head, row0, tq)
            row0 += tq


def _diff_attn(lam_params, subln, q, ks, vs, *, block_rows, heads, lam_init):
    B, N, _ = q.shape
    nseg = len(ks)
    assert sum(block_rows) == N

    def head_spec(n):
        return pl.BlockSpec((1, n, heads * LANES), lambda b, h: (b, 0, h))

    kv_specs = [head_spec(k.shape[1]) for k in ks]
    return pl.pallas_call(
        functools.partial(_diff_attn_kernel, nseg=nseg, lam_init=lam_init, block_rows=block_rows,
                          heads=heads),
        grid=(B, N_LANE_TILES // heads),
        in_specs=[
            pl.BlockSpec((4, HEAD_DIM), lambda b, h: (0, 0)),
            pl.BlockSpec((1, LANES), lambda b, h: (0, 0)),
            head_spec(N),
        ] + kv_specs + kv_specs,
        out_specs=head_spec(N),
        out_shape=jax.ShapeDtypeStruct((B, N, D_MODEL), BF16),
        compiler_params=pltpu.CompilerParams(vmem_limit_bytes=VMEM_LIMIT),
        name="diff_attn_%dseg" % nseg,
    )(lam_params, subln, q, *ks, *vs)


def _na_window_start(j):
    return min(max(j - 1, 0), NA_BLOCKS - NA_KBLK)


def _na_bias_type(j):
    return 0 if j == 0 else (2 if j == NA_BLOCKS - 1 else 1)


def _na_block_layout():
    rows = SEQ // GRID_W
    per_block = []
    for j in range(NA_BLOCKS):
        blk = []
        for a in range(NA_Q_ROWS):
            r = NA_Q_ROWS * j + a
            rs = min(max(r - NA_WIN_ROWS // 2, 0), rows - NA_WIN_ROWS)
            row = []
            for u in range(NA_K_ROWS // 2):
                kr = NA_Q_ROWS * _na_window_start(j) + 2 * u
                ok = tuple(rs <= k < rs + NA_WIN_ROWS for k in (kr, kr + 1))
                row.append((kr - r + NA_WIN_ROWS - 1,) + ok)
            blk.append(tuple(row))
        per_block.append(tuple(blk))
    assert all(per_block[j] == per_block[1] for j in range(2, NA_BLOCKS - 1))
    return (per_block[0], per_block[1], per_block[NA_BLOCKS - 1])


NA_LAYOUT = _na_block_layout()
NA_N_DR = 2 * NA_WIN_ROWS - 1
NA_N_PAIR = NA_N_DR + 1


def _na_pair_table(rpb):
    qc = np.arange(GRID_W)
    cs = np.clip(qc - NA_WIN_COLS // 2, 0, GRID_W - NA_WIN_COLS)
    col_ok = (qc[None, :] >= cs[:, None]) & (qc[None, :] < cs[:, None] + NA_WIN_COLS)
    dc = np.clip(qc[None, :] - qc[:, None] + NA_WIN_COLS - 1, 0, 2 * NA_WIN_COLS - 2)
    onehot = (dc[None] == np.arange(2 * NA_WIN_COLS - 1)[:, None, None]).astype(np.float32)
    toep = jnp.einsum('hdj,jqk->hdqk', rpb, jnp.asarray(onehot), precision=lax.Precision.HIGHEST)
    toep = jnp.where(jnp.asarray(col_ok), toep * LOG2E, -jnp.inf)
    padded = jnp.pad(toep, ((0, 0), (1, 1), (0, 0), (0, 0)), constant_values=-jnp.inf)
    pairs = jnp.concatenate([padded[:, :NA_N_PAIR], padded[:, 1:]], axis=-1)
    return pairs.reshape(N_LANE_TILES, 2, NA_N_PAIR, GRID_W, LANES)


def _na_fill_bias(pair_ref, bias_scr):
    lane = lax.broadcasted_iota(jnp.int32, (GRID_W, LANES), 1)
    neg = jnp.full((GRID_W, LANES), -jnp.inf, F32)
    for kind, layout in enumerate(NA_LAYOUT):
        for head in range(2):
            for a, row in enumerate(layout):
                r0 = head * NA_TQ + a * GRID_W
                for u, (d, left_ok, right_ok) in enumerate(row):
                    slab = neg
                    if left_ok or right_ok:
                        assert not left_ok or 0 <= d < NA_N_DR
                        assert not right_ok or 0 <= d + 1 < NA_N_DR
                        slab = pair_ref[0, head, d + 1]
                        if not left_ok:
                            slab = jnp.where(lane < HEAD_DIM, neg, slab)
                        if not right_ok:
                            slab = jnp.where(lane < HEAD_DIM, slab, neg)
                    bias_scr[kind, r0:r0 + GRID_W, u * LANES:(u + 1) * LANES] = slab


def _nbr_attn_kernel(q_ref, k_ref, v_ref, kc_ref, vc_ref, pair_ref, o_ref, bias_scr):
    n_win = NA_K_ROWS * GRID_W
    lane = lax.broadcasted_iota(jnp.int32, (NA_TQ, LANES), 1)

    @pl.when(pl.program_id(1) == 0)
    def _():
        _na_fill_bias(pair_ref, bias_scr)

    for j in range(NA_BLOCKS):
        rows = slice(j * NA_TQ, (j + 1) * NA_TQ)
        win0 = _na_window_start(j) * NA_TQ
        win = slice(win0, win0 + n_win)
        qq = _stack_halves(q_ref[0, rows, :])
        s_band = _dot_nt(qq, k_ref[0, win, :]) + bias_scr[_na_bias_type(j)]
        s_ctx = _dot_nt(qq, kc_ref[0])
        e_parts, l = _softmax_exp2_parts([s_band, s_ctx])
        o2 = _pv_parts(e_parts, [v_ref[0, win, :], vc_ref[0]]) * (1.0 / l)
        o_ref[0, rows, :] = jnp.where(lane < HEAD_DIM, o2[:NA_TQ], o2[NA_TQ:]).astype(BF16)


def _nbr_attn(q, k, v, kc, vc, pair_tab):
    B, N, _ = q.shape

    def head_spec(n):
        return pl.BlockSpec((1, n, LANES), lambda hp, b: (b, 0, hp))

    pair_spec = pl.BlockSpec((1, 2, NA_N_PAIR, GRID_W, LANES), lambda hp, b: (hp, 0, 0, 0, 0))
    return pl.pallas_call(
        _nbr_attn_kernel,
        grid=(N_LANE_TILES, B),
        in_specs=[head_spec(N), head_spec(N), head_spec(N), head_spec(CTX_LEN), head_spec(CTX_LEN),
                  pair_spec],
        out_specs=head_spec(N),
        out_shape=jax.ShapeDtypeStruct(q.shape, BF16),
        scratch_shapes=[pltpu.VMEM((3, 2 * NA_TQ, NA_K_ROWS * GRID_W), F32)],
        compiler_params=pltpu.CompilerParams(vmem_limit_bytes=VMEM_LIMIT,
                                             dimension_semantics=("arbitrary", "arbitrary")),
        name="nbr_attn",
    )(q, k, v, kc, vc, pair_tab)


FFN_CHUNKS = ((0, 1024), (1024, 1024), (2048, 768))


def _out_ffn_kernel(*refs, final):
    o_ref, h_ref, mod_ref, g_ref, wo_ref, wgu_ref, wd_ref = refs[:7]
    if final:
        gfin_ref, out_ref = refs[7:]
    else:
        (out_ref,) = refs[7:]
    h1 = h_ref[0] + mod_ref[0, 2:3, :] * _dot(o_ref[0], wo_ref[...])
    xn = _rmsnorm_mod(h1, g_ref[...], mod_ref[0, 3:4, :], mod_ref[0, 4:5, :]).astype(BF16)
    acc = None
    for start, size in FFN_CHUNKS:
        gate = _dot(xn, wgu_ref[:, start:start + size])
        up = _dot(xn, wgu_ref[:, D_FF + start:D_FF + start + size])
        act = (gate * jax.nn.sigmoid(gate) * up).astype(BF16)
        part = _dot(act, wd_ref[start:start + size, :])
        acc = part if acc is None else acc + part
    h2 = h1 + mod_ref[0, 5:6, :] * acc
    if final:
        h2 = h2 * lax.rsqrt(jnp.mean(h2 * h2, axis=-1, keepdims=True) + EPS) * gfin_ref[...]
    out_ref[0] = h2


def _out_ffn(o, h, mod, g, wo, wgu, wd, gfin, *, tm):
    G, R, _ = h.shape
    final = gfin is not None
    row_spec = pl.BlockSpec((1, tm, D_MODEL), lambda b, t: (b, t, 0))

    def resident(shape):
        return pl.BlockSpec(shape, lambda b, t: (0, 0), pipeline_mode=pl.Buffered(1))

    in_specs = [
        row_spec,
        row_spec,
        pl.BlockSpec((1, 6, D_MODEL), lambda b, t: (b, 0, 0)),
        pl.BlockSpec((1, D_MODEL), lambda b, t: (0, 0)),
        resident((D_MODEL, D_MODEL)),
        resident((D_MODEL, 2 * D_FF)),
        resident((D_FF, D_MODEL)),
    ]
    args = [o, h, mod, g, wo, wgu, wd]
    if final:
        in_specs.append(pl.BlockSpec((1, D_MODEL), lambda b, t: (0, 0)))
        args.append(gfin)
    return pl.pallas_call(
        functools.partial(_out_ffn_kernel, final=final),
        grid=(G, R // tm),
        in_specs=in_specs,
        out_specs=row_spec,
        out_shape=jax.ShapeDtypeStruct(h.shape, F32),
        compiler_params=pltpu.CompilerParams(vmem_limit_bytes=VMEM_LIMIT),
        name="out_ffn_final" if final else "out_ffn",
    )(*args)


def kernel(x, c, ctx, c_ctx, ada_w, ada_b, norm_mix, norm_ffn, da_wqkv, da_lambda_q1, da_lambda_k1,
           da_lambda_q2, da_lambda_k2, da_subln, da_wo, na_wqkv, na_rpb, na_wo, ffn_w_gate_up,
           ffn_w_down, norm_final):
    B, N, D = x.shape
    L = ctx.shape[1]
    TM = 512

    cond = jnp.zeros((COND_ROWS, D), F32).at[:B].set(c).at[CTX_MOD_ROW].set(c_ctx)
    mods = _adaln(cond, ada_w, ada_b).reshape(DEPTH, COND_ROWS, 6, D)

    h = x
    hc = ctx.reshape(1, B * L, D)
    rope_tabs = _rope_tables()

    mod_l = mods[0, :B]
    mod_c = mods[0, CTX_MOD_ROW:CTX_MOD_ROW + 1]
    g_mix = norm_mix[0].reshape(1, D)
    g_ffn = norm_ffn[0].reshape(1, D)
    wqkv = da_wqkv[0].astype(BF16)
    q_l, k_l, v_l = _qkv(h, mod_l, g_mix, wqkv, rope_tabs, tm=TM)
    q_c, k_c, v_c = (t.reshape(B, L, D) for t in _qkv(hc, mod_c, g_mix, wqkv, None, tm=TM))
    lam_params = jnp.stack([da_lambda_q1[0], da_lambda_k1[0], da_lambda_q2[0], da_lambda_k2[0]])
    subln = da_subln[0].reshape(1, LANES)
    lam_init = 0.8 - 0.6 * float(np.exp(-0.3 * 0))
    o_l = _diff_attn(lam_params, subln, q_l, [k_c, k_l], [v_c, v_l], block_rows=DA_BLOCK_ROWS,
                     heads=1, lam_init=lam_init)
    o_c = _diff_attn(lam_params, subln, q_c, [k_c], [v_c], block_rows=(L,), heads=N_LANE_TILES,
                     lam_init=lam_init)
    wo = da_wo[0].astype(BF16)
    wgu = ffn_w_gate_up[0].astype(BF16)
    wd = ffn_w_down[0].astype(BF16)
    h = _out_ffn(o_l, h, mod_l, g_ffn, wo, wgu, wd, None, tm=TM)
    hc = _out_ffn(o_c.reshape(1, B * L, D), hc, mod_c, g_ffn, wo, wgu, wd, None, tm=TM)

    mod_l = mods[1, :B]
    mod_c = mods[1, CTX_MOD_ROW:CTX_MOD_ROW + 1]
    g_mix = norm_mix[1].reshape(1, D)
    g_ffn = norm_ffn[1].reshape(1, D)
    wqkv = na_wqkv[0].astype(BF16)
    q_l, k_l, v_l = _qkv(h, mod_l, g_mix, wqkv, None, tm=TM)
    k_c, v_c = (t.reshape(B, L, D) for t in _qkv(hc, mod_c, g_mix, wqkv, None, tm=TM, want_q=False))
    o_l = _nbr_attn(q_l, k_l, v_l, k_c, v_c, _na_pair_table(na_rpb[0]))
    return _out_ffn(o_l, h, mod_l, g_ffn, na_wo[0].astype(BF16), ffn_w_gate_up[1].astype(BF16),
                    ffn_w_down[1].astype(BF16), norm_final.reshape(1, D), tm=TM)
```

```python
import functools

import numpy as np
import jax
import jax.numpy as jnp
from jax import lax
from jax.experimental import pallas as pl
from jax.experimental.pallas import tpu as pltpu

D_MODEL = 1024
BATCH = 16
SEQ = 2048
DEPTH = 2
GRID_W = 64
CTX_LEN = 256
HEAD_DIM = 64
NA_WIN_ROWS = 8
NA_WIN_COLS = 16
D_FF = 2816
ROPE_THETA = 10000.0
EPS = 1e-6
LOG2E = 1.4426950408889634
Q_SCALE = HEAD_DIM ** -0.5 * LOG2E

LANES = 128
N_LANE_TILES = D_MODEL // LANES
COND_ROWS = 24
CTX_MOD_ROW = BATCH
VMEM_LIMIT = 56 * 1024 * 1024

NA_Q_ROWS = 4
NA_K_ROWS = 12
NA_TQ = NA_Q_ROWS * GRID_W
NA_BLOCKS = SEQ // NA_TQ
NA_KBLK = NA_K_ROWS // NA_Q_ROWS
DA_BLOCK_ROWS = (128,) + (256,) * 7 + (128,)
F32 = jnp.float32
BF16 = jnp.bfloat16


def _dot(a, b):
    return jnp.dot(a, b, preferred_element_type=F32)


def _dot_nt(a, b):
    return lax.dot_general(a, b, (((1,), (1,)), ((), ())), preferred_element_type=F32)


def _adaln_kernel(cond_ref, w_ref, b_ref, out_ref):
    cnd = cond_ref[...]
    a = cnd * jax.nn.sigmoid(cnd)
    w = w_ref[0]
    a_hi = a.astype(BF16)
    a_lo = (a - a_hi.astype(F32)).astype(BF16)
    w_hi = w.astype(BF16)
    w_lo = (w - w_hi.astype(F32)).astype(BF16)
    acc = _dot(a_hi, w_hi) + _dot(a_lo, w_hi) + _dot(a_hi, w_lo)
    out_ref[0] = acc + b_ref[0]


def _adaln(cond, ada_w, ada_b):
    n_out = ada_w.shape[-1]
    tn = 1024
    return pl.pallas_call(
        _adaln_kernel,
        grid=(DEPTH, n_out // tn),
        in_specs=[
            pl.BlockSpec((COND_ROWS, D_MODEL), lambda i, j: (0, 0)),
            pl.BlockSpec((1, D_MODEL, tn), lambda i, j: (i, 0, j)),
            pl.BlockSpec((1, 1, tn), lambda i, j: (i, 0, j)),
        ],
        out_specs=pl.BlockSpec((1, COND_ROWS, tn), lambda i, j: (i, 0, j)),
        out_shape=jax.ShapeDtypeStruct((DEPTH, COND_ROWS, n_out), F32),
        compiler_params=pltpu.CompilerParams(vmem_limit_bytes=VMEM_LIMIT),
        name="adaln",
    )(cond, ada_w, ada_b.reshape(DEPTH, 1, n_out))


def _rmsnorm_mod(h, g, shift, scale):
    y = h * lax.rsqrt(jnp.mean(h * h, axis=-1, keepdims=True) + EPS) * g
    return y * (1.0 + scale) + shift


def _qkv_kernel(*refs, rope, want_q):
    if rope:
        h_ref, mod_ref, g_ref, w_ref, cos_ref, sa_ref, sb_ref = refs[:7]
        outs = refs[7:]
    else:
        h_ref, mod_ref, g_ref, w_ref = refs[:4]
        outs = refs[4:]
    xn = _rmsnorm_mod(h_ref[0], g_ref[...], mod_ref[0, 0:1, :], mod_ref[0, 1:2, :]).astype(BF16)

    def rotate(t):
        return (t * cos_ref[...] + pltpu.roll(t, LANES - 16, axis=1) * sa_ref[...]
                + pltpu.roll(t, 16, axis=1) * sb_ref[...])

    parts = ("q", "k", "v") if want_q else ("k", "v")
    for name, o_ref in zip(parts, outs):
        col = {"q": 0, "k": D_MODEL, "v": 2 * D_MODEL}[name]
        acc = _dot(xn, w_ref[:, col:col + D_MODEL])
        if name == "q":
            acc = acc * Q_SCALE
        if rope and name != "v":
            for c in range(N_LANE_TILES):
                sl = slice(c * LANES, (c + 1) * LANES)
                o_ref[0, :, sl] = rotate(acc[:, sl]).astype(BF16)
        else:
            o_ref[0] = acc.astype(BF16)


def _qkv(h, mod, g, w, rope_tabs, *, tm, want_q=True):
    G, R, _ = h.shape
    rope = rope_tabs is not None
    row_spec = pl.BlockSpec((1, tm, D_MODEL), lambda b, t: (b, t, 0))
    in_specs = [
        row_spec,
        pl.BlockSpec((1, 6, D_MODEL), lambda b, t: (b, 0, 0)),
        pl.BlockSpec((1, D_MODEL), lambda b, t: (0, 0)),
        pl.BlockSpec((D_MODEL, 3 * D_MODEL), lambda b, t: (0, 0)),
    ]
    args = [h, mod, g, w]
    if rope:
        in_specs += [pl.BlockSpec((tm, LANES), lambda b, t: (t, 0))] * 3
        args += list(rope_tabs)
    n_out = 3 if want_q else 2
    return pl.pallas_call(
        functools.partial(_qkv_kernel, rope=rope, want_q=want_q),
        grid=(G, R // tm),
        in_specs=in_specs,
        out_specs=[row_spec] * n_out,
        out_shape=[jax.ShapeDtypeStruct((G, R, D_MODEL), BF16)] * n_out,
        compiler_params=pltpu.CompilerParams(vmem_limit_bytes=VMEM_LIMIT),
        name="qkv_rope" if rope else "qkv",
    )(*args)


def _rope_tables():
    t = jnp.arange(SEQ)
    row = (t // GRID_W).astype(F32)
    col = (t % GRID_W).astype(F32)
    half = HEAD_DIM // 2
    freqs = 1.0 / (ROPE_THETA ** (jnp.arange(0, half, 2, dtype=F32) / half))
    ar = row[:, None] * freqs
    ac = col[:, None] * freqs
    ang = jnp.concatenate([ar, ar, ac, ac], axis=-1)
    cos, sin = jnp.cos(ang), jnp.sin(ang)
    first = (np.arange(HEAD_DIM) % half) < half // 2
    sa = jnp.where(first, -sin, 0.0)
    sb = jnp.where(first, 0.0, sin)
    rep = LANES // HEAD_DIM
    return tuple(jnp.tile(x, (1, rep)) for x in (cos, sa, sb))


def _stack_halves(q):
    lane = lax.broadcasted_iota(jnp.int32, q.shape, 1)
    zero = jnp.zeros_like(q)
    return jnp.concatenate([jnp.where(lane < HEAD_DIM, q, zero),
                            jnp.where(lane >= HEAD_DIM, q, zero)], axis=0)


def _with_ones(v):
    return jnp.concatenate([v, jnp.ones(v.shape, v.dtype)], axis=1)


def _pv_parts(e_parts, v_parts):
    acc = None
    for e, v in zip(e_parts, v_parts):
        pv = _dot(e.astype(BF16), v)
        acc = pv if acc is None else acc + pv
    return acc


def _diff_attn_kernel(*refs, nseg, lam_init, block_rows, heads):
    lam_ref, subln_ref, q_ref = refs[:3]
    k_refs = refs[3:3 + nseg]
    v_refs = refs[3 + nseg:3 + 2 * nseg]
    o_ref = refs[3 + 2 * nseg]

    lp = lam_ref[...]
    lam = (jnp.exp(jnp.sum(lp[0:1] * lp[1:2], axis=-1, keepdims=True))
           - jnp.exp(jnp.sum(lp[2:3] * lp[3:4], axis=-1, keepdims=True)) + lam_init)
    subln = subln_ref[...] * (1.0 - lam_init)
    v_aug = [[_with_ones(v_ref[0, :, head * LANES:(head + 1) * LANES]) for v_ref in v_refs]
             for head in range(heads)]

    def block(head, row0, tq):
        cols = slice(head * LANES, (head + 1) * LANES)
        qq = _stack_halves(q_ref[0, row0:row0 + tq, cols])
        s_parts = [_dot_nt(qq, k_ref[0, :, cols]) for k_ref in k_refs]
        m = s_parts[0].max(axis=-1, keepdims=True)
        for s in s_parts[1:]:
            m = jnp.maximum(m, s.max(axis=-1, keepdims=True))
        acc = _pv_parts([jnp.exp2(s - m) for s in s_parts], v_aug[head])
        w = 1.0 / acc[:, LANES:]
        o = acc[:tq, :LANES] * w[:tq] - acc[tq:, :LANES] * (w[tq:] * lam)
        on = o * lax.rsqrt(jnp.mean(o * o, axis=-1, keepdims=True) + EPS) * subln
        o_ref[0, row0:row0 + tq, cols] = on.astype(BF16)

    for head in range(heads):
        row0 = 0
        for tq in block_rows:
            block(head, row0, tq)
            row0 += tq


def _diff_attn(lam_params, subln, q, ks, vs, *, block_rows, heads, lam_init):
    B, N, _ = q.shape
    nseg = len(ks)
    assert sum(block_rows) == N

    def head_spec(n):
        return pl.BlockSpec((1, n, heads * LANES), lambda b, h: (b, 0, h))

    kv_specs = [head_spec(k.shape[1]) for k in ks]
    return pl.pallas_call(
        functools.partial(_diff_attn_kernel, nseg=nseg, lam_init=lam_init, block_rows=block_rows,
                          heads=heads),
        grid=(B, N_LANE_TILES // heads),
        in_specs=[
            pl.BlockSpec((4, HEAD_DIM), lambda b, h: (0, 0)),
            pl.BlockSpec((1, LANES), lambda b, h: (0, 0)),
            head_spec(N),
        ] + kv_specs + kv_specs,
        out_specs=head_spec(N),
        out_shape=jax.ShapeDtypeStruct((B, N, D_MODEL), BF16),
        compiler_params=pltpu.CompilerParams(vmem_limit_bytes=VMEM_LIMIT),
        name="diff_attn_%dseg" % nseg,
    )(lam_params, subln, q, *ks, *vs)


def _na_window_start(j):
    return min(max(j - 1, 0), NA_BLOCKS - NA_KBLK)


def _na_bias_type(j):
    return 0 if j == 0 else (2 if j == NA_BLOCKS - 1 else 1)


def _na_block_layout():
    rows = SEQ // GRID_W
    per_block = []
    for j in range(NA_BLOCKS):
        blk = []
        for a in range(NA_Q_ROWS):
            r = NA_Q_ROWS * j + a
            rs = min(max(r - NA_WIN_ROWS // 2, 0), rows - NA_WIN_ROWS)
            row = []
            for u in range(NA_K_ROWS // 2):
                kr = NA_Q_ROWS * _na_window_start(j) + 2 * u
                ok = tuple(rs <= k < rs + NA_WIN_ROWS for k in (kr, kr + 1))
                row.append((kr - r + NA_WIN_ROWS - 1,) + ok)
            blk.append(tuple(row))
        per_block.append(tuple(blk))
    assert all(per_block[j] == per_block[1] for j in range(2, NA_BLOCKS - 1))
    return (per_block[0], per_block[1], per_block[NA_BLOCKS - 1])


NA_LAYOUT = _na_block_layout()
NA_N_DR = 2 * NA_WIN_ROWS - 1
NA_N_PAIR = NA_N_DR + 1


def _na_pair_table(rpb):
    qc = np.arange(GRID_W)
    cs = np.clip(qc - NA_WIN_COLS // 2, 0, GRID_W - NA_WIN_COLS)
    col_ok = (qc[None, :] >= cs[:, None]) & (qc[None, :] < cs[:, None] + NA_WIN_COLS)
    dc = np.clip(qc[None, :] - qc[:, None] + NA_WIN_COLS - 1, 0, 2 * NA_WIN_COLS - 2)
    onehot = (dc[None] == np.arange(2 * NA_WIN_COLS - 1)[:, None, None]).astype(np.float32)
    toep = jnp.einsum('hdj,jqk->hdqk', rpb, jnp.asarray(onehot), precision=lax.Precision.HIGHEST)
    toep = jnp.where(jnp.asarray(col_ok), toep * LOG2E, -jnp.inf)
    padded = jnp.pad(toep, ((0, 0), (1, 1), (0, 0), (0, 0)), constant_values=-jnp.inf)
    pairs = jnp.concatenate([padded[:, :NA_N_PAIR], padded[:, 1:]], axis=-1)
    return pairs.reshape(N_LANE_TILES, 2, NA_N_PAIR, GRID_W, LANES)


def _na_fill_bias(pair_ref, bias_scr):
    lane = lax.broadcasted_iota(jnp.int32, (GRID_W, LANES), 1)
    neg = jnp.full((GRID_W, LANES), -jnp.inf, F32)
    for kind, layout in enumerate(NA_LAYOUT):
        for head in range(2):
            for a, row in enumerate(layout):
                r0 = head * NA_TQ + a * GRID_W
                for u, (d, left_ok, right_ok) in enumerate(row):
                    slab = neg
                    if left_ok or right_ok:
                        assert not left_ok or 0 <= d < NA_N_DR
                        assert not right_ok or 0 <= d + 1 < NA_N_DR
                        slab = pair_ref[0, head, d + 1]
                        if not left_ok:
                            slab = jnp.where(lane < HEAD_DIM, neg, slab)
                        if not right_ok:
                            slab = jnp.where(lane < HEAD_DIM, slab, neg)
                    bias_scr[kind, r0:r0 + GRID_W, u * LANES:(u + 1) * LANES] = slab


def _nbr_attn_kernel(q_ref, k_ref, v_ref, kc_ref, vc_ref, pair_ref, o_ref, bias_scr):
    n_win = NA_K_ROWS * GRID_W
    lane = lax.broadcasted_iota(jnp.int32, (NA_TQ, LANES), 1)

    @pl.when(pl.program_id(1) == 0)
    def _():
        _na_fill_bias(pair_ref, bias_scr)

    v_aug = _with_ones(v_ref[0])
    vc_aug = _with_ones(vc_ref[0])
    for j in range(NA_BLOCKS):
        rows = slice(j * NA_TQ, (j + 1) * NA_TQ)
        win0 = _na_window_start(j) * NA_TQ
        win = slice(win0, win0 + n_win)
        qq = _stack_halves(q_ref[0, rows, :])
        s_band = _dot_nt(qq, k_ref[0, win, :]) + bias_scr[_na_bias_type(j)]
        s_ctx = _dot_nt(qq, kc_ref[0])
        m = jnp.maximum(s_band.max(axis=-1, keepdims=True), s_ctx.max(axis=-1, keepdims=True))
        acc = _pv_parts([jnp.exp2(s_band - m), jnp.exp2(s_ctx - m)], [v_aug[win], vc_aug])
        o2 = acc[:, :LANES] / acc[:, LANES:]
        o_ref[0, rows, :] = jnp.where(lane < HEAD_DIM, o2[:NA_TQ], o2[NA_TQ:]).astype(BF16)


def _nbr_attn(q, k, v, kc, vc, pair_tab):
    B, N, _ = q.shape

    def head_spec(n):
        return pl.BlockSpec((1, n, LANES), lambda hp, b: (b, 0, hp))

    pair_spec = pl.BlockSpec((1, 2, NA_N_PAIR, GRID_W, LANES), lambda hp, b: (hp, 0, 0, 0, 0))
    return pl.pallas_call(
        _nbr_attn_kernel,
        grid=(N_LANE_TILES, B),
        in_specs=[head_spec(N), head_spec(N), head_spec(N), head_spec(CTX_LEN), head_spec(CTX_LEN),
                  pair_spec],
        out_specs=head_spec(N),
        out_shape=jax.ShapeDtypeStruct(q.shape, BF16),
        scratch_shapes=[pltpu.VMEM((3, 2 * NA_TQ, NA_K_ROWS * GRID_W), F32)],
        compiler_params=pltpu.CompilerParams(vmem_limit_bytes=VMEM_LIMIT,
                                             dimension_semantics=("arbitrary", "arbitrary")),
        name="nbr_attn",
    )(q, k, v, kc, vc, pair_tab)


FFN_CHUNKS = ((0, 1024), (1024, 1024), (2048, 768))


def _out_ffn_kernel(*refs, final):
    o_ref, h_ref, mod_ref, g_ref, wo_ref, wgu_ref, wd_ref = refs[:7]
    if final:
        gfin_ref, out_ref = refs[7:]
    else:
        (out_ref,) = refs[7:]
    h1 = h_ref[0] + mod_ref[0, 2:3, :] * _dot(o_ref[0], wo_ref[...])
    xn = _rmsnorm_mod(h1, g_ref[...], mod_ref[0, 3:4, :], mod_ref[0, 4:5, :]).astype(BF16)
    acc = None
    for start, size in FFN_CHUNKS:
        gate = _dot(xn, wgu_ref[:, start:start + size])
        up = _dot(xn, wgu_ref[:, D_FF + start:D_FF + start + size])
        act = (gate * jax.nn.sigmoid(gate) * up).astype(BF16)
        part = _dot(act, wd_ref[start:start + size, :])
        acc = part if acc is None else acc + part
    h2 = h1 + mod_ref[0, 5:6, :] * acc
    if final:
        h2 = h2 * lax.rsqrt(jnp.mean(h2 * h2, axis=-1, keepdims=True) + EPS) * gfin_ref[...]
    out_ref[0] = h2


def _out_ffn(o, h, mod, g, wo, wgu, wd, gfin, *, tm):
    G, R, _ = h.shape
    final = gfin is not None
    row_spec = pl.BlockSpec((1, tm, D_MODEL), lambda b, t: (b, t, 0))

    def resident(shape):
        return pl.BlockSpec(shape, lambda b, t: (0, 0), pipeline_mode=pl.Buffered(1))

    in_specs = [
        row_spec,
        row_spec,
        pl.BlockSpec((1, 6, D_MODEL), lambda b, t: (b, 0, 0)),
        pl.BlockSpec((1, D_MODEL), lambda b, t: (0, 0)),
        resident((D_MODEL, D_MODEL)),
        resident((D_MODEL, 2 * D_FF)),
        resident((D_FF, D_MODEL)),
    ]
    args = [o, h, mod, g, wo, wgu, wd]
    if final:
        in_specs.append(pl.BlockSpec((1, D_MODEL), lambda b, t: (0, 0)))
        args.append(gfin)
    return pl.pallas_call(
        functools.partial(_out_ffn_kernel, final=final),
        grid=(G, R // tm),
        in_specs=in_specs,
        out_specs=row_spec,
        out_shape=jax.ShapeDtypeStruct(h.shape, F32),
        compiler_params=pltpu.CompilerParams(vmem_limit_bytes=VMEM_LIMIT),
        name="out_ffn_final" if final else "out_ffn",
    )(*args)


def kernel(x, c, ctx, c_ctx, ada_w, ada_b, norm_mix, norm_ffn, da_wqkv, da_lambda_q1, da_lambda_k1,
           da_lambda_q2, da_lambda_k2, da_subln, da_wo, na_wqkv, na_rpb, na_wo, ffn_w_gate_up,
           ffn_w_down, norm_final):
    B, N, D = x.shape
    L = ctx.shape[1]
    TM = 1024

    cond = jnp.zeros((COND_ROWS, D), F32).at[:B].set(c).at[CTX_MOD_ROW].set(c_ctx)
    mods = _adaln(cond, ada_w, ada_b).reshape(DEPTH, COND_ROWS, 6, D)

    h = x
    hc = ctx.reshape(1, B * L, D)
    rope_tabs = _rope_tables()

    mod_l = mods[0, :B]
    mod_c = mods[0, CTX_MOD_ROW:CTX_MOD_ROW + 1]
    g_mix = norm_mix[0].reshape(1, D)
    g_ffn = norm_ffn[0].reshape(1, D)
    wqkv = da_wqkv[0].astype(BF16)
    q_l, k_l, v_l = _qkv(h, mod_l, g_mix, wqkv, rope_tabs, tm=TM)
    q_c, k_c, v_c = (t.reshape(B, L, D) for t in _qkv(hc, mod_c, g_mix, wqkv, None, tm=TM))
    lam_params = jnp.stack([da_lambda_q1[0], da_lambda_k1[0], da_lambda_q2[0], da_lambda_k2[0]])
    subln = da_subln[0].reshape(1, LANES)
    lam_init = 0.8 - 0.6 * float(np.exp(-0.3 * 0))
    o_l = _diff_attn(lam_params, subln, q_l, [k_c, k_l], [v_c, v_l], block_rows=DA_BLOCK_ROWS,
                     heads=1, lam_init=lam_init)
    o_c = _diff_attn(lam_params, subln, q_c, [k_c], [v_c], block_rows=(L,), heads=N_LANE_TILES,
                     lam_init=lam_init)
    wo = da_wo[0].astype(BF16)
    wgu = ffn_w_gate_up[0].astype(BF16)
    wd = ffn_w_down[0].astype(BF16)
    h = _out_ffn(o_l, h, mod_l, g_ffn, wo, wgu, wd, None, tm=TM)
    hc = _out_ffn(o_c.reshape(1, B * L, D), hc, mod_c, g_ffn, wo, wgu, wd, None, tm=TM)

    mod_l = mods[1, :B]
    mod_c = mods[1, CTX_MOD_ROW:CTX_MOD_ROW + 1]
    g_mix = norm_mix[1].reshape(1, D)
    g_ffn = norm_ffn[1].reshape(1, D)
    wqkv = na_wqkv[0].astype(BF16)
    q_l, k_l, v_l = _qkv(h, mod_l, g_mix, wqkv, None, tm=TM)
    k_c, v_c = (t.reshape(B, L, D) for t in _qkv(hc, mod_c, g_mix, wqkv, None, tm=TM, want_q=False))
    o_l = _nbr_attn(q_l, k_l, v_l, k_c, v_c, _na_pair_table(na_rpb[0]))
    return _out_ffn(o_l, h, mod_l, g_ffn, na_wo[0].astype(BF16), ffn_w_gate_up[1].astype(BF16),
                    ffn_w_down[1].astype(BF16), norm_final.reshape(1, D), tm=TM)
```

```python
import functools

import numpy as np
import jax
import jax.numpy as jnp
from jax import lax
from jax.experimental import pallas as pl
from jax.experimental.pallas import tpu as pltpu

D_MODEL = 1024
BATCH = 16
SEQ = 2048
DEPTH = 2
GRID_W = 64
CTX_LEN = 256
HEAD_DIM = 64
NA_WIN_ROWS = 8
NA_WIN_COLS = 16
D_FF = 2816
ROPE_THETA = 10000.0
EPS = 1e-6
LOG2E = 1.4426950408889634
Q_SCALE = HEAD_DIM ** -0.5 * LOG2E

LANES = 128
N_LANE_TILES = D_MODEL // LANES
COND_ROWS = 24
CTX_MOD_ROW = BATCH
VMEM_LIMIT = 56 * 1024 * 1024

NA_Q_ROWS = 4
NA_K_ROWS = 12
NA_TQ = NA_Q_ROWS * GRID_W
NA_BLOCKS = SEQ // NA_TQ
NA_KBLK = NA_K_ROWS // NA_Q_ROWS
NA_BATCH_PER_STEP = 2
DA_BLOCK_ROWS = (128,) + (256,) * 7 + (128,)
F32 = jnp.float32
BF16 = jnp.bfloat16


def _dot(a, b):
    return jnp.dot(a, b, preferred_element_type=F32)


def _dot_nt(a, b):
    return lax.dot_general(a, b, (((1,), (1,)), ((), ())), preferred_element_type=F32)


def _adaln_kernel(cond_ref, w_ref, b_ref, out_ref):
    cnd = cond_ref[...]
    a = cnd * jax.nn.sigmoid(cnd)
    w = w_ref[0]
    a_hi = a.astype(BF16)
    a_lo = (a - a_hi.astype(F32)).astype(BF16)
    w_hi = w.astype(BF16)
    w_lo = (w - w_hi.astype(F32)).astype(BF16)
    acc = _dot(a_hi, w_hi) + _dot(a_lo, w_hi) + _dot(a_hi, w_lo)
    out_ref[0] = acc + b_ref[0]


def _adaln(cond, ada_w, ada_b):
    n_out = ada_w.shape[-1]
    tn = 1024
    return pl.pallas_call(
        _adaln_kernel,
        grid=(DEPTH, n_out // tn),
        in_specs=[
            pl.BlockSpec((COND_ROWS, D_MODEL), lambda i, j: (0, 0)),
            pl.BlockSpec((1, D_MODEL, tn), lambda i, j: (i, 0, j)),
            pl.BlockSpec((1, 1, tn), lambda i, j: (i, 0, j)),
        ],
        out_specs=pl.BlockSpec((1, COND_ROWS, tn), lambda i, j: (i, 0, j)),
        out_shape=jax.ShapeDtypeStruct((DEPTH, COND_ROWS, n_out), F32),
        compiler_params=pltpu.CompilerParams(vmem_limit_bytes=VMEM_LIMIT),
        name="adaln",
    )(cond, ada_w, ada_b.reshape(DEPTH, 1, n_out))


def _rmsnorm_mod(h, g, shift, scale):
    y = h * lax.rsqrt(jnp.mean(h * h, axis=-1, keepdims=True) + EPS) * g
    return y * (1.0 + scale) + shift


def _qkv_kernel(*refs, rope, want_q):
    if rope:
        h_ref, mod_ref, g_ref, w_ref, cos_ref, sa_ref, sb_ref = refs[:7]
        outs = refs[7:]
    else:
        h_ref, mod_ref, g_ref, w_ref = refs[:4]
        outs = refs[4:]
    xn = _rmsnorm_mod(h_ref[0], g_ref[...], mod_ref[0, 0:1, :], mod_ref[0, 1:2, :]).astype(BF16)

    def rotate(t):
        return (t * cos_ref[...] + pltpu.roll(t, LANES - 16, axis=1) * sa_ref[...]
                + pltpu.roll(t, 16, axis=1) * sb_ref[...])

    parts = ("q", "k", "v") if want_q else ("k", "v")
    for name, o_ref in zip(parts, outs):
        col = {"q": 0, "k": D_MODEL, "v": 2 * D_MODEL}[name]
        acc = _dot(xn, w_ref[:, col:col + D_MODEL])
        if name == "q":
            acc = acc * Q_SCALE
        if rope and name != "v":
            for c in range(N_LANE_TILES):
                sl = slice(c * LANES, (c + 1) * LANES)
                o_ref[0, :, sl] = rotate(acc[:, sl]).astype(BF16)
        else:
            o_ref[0] = acc.astype(BF16)


def _qkv(h, mod, g, w, rope_tabs, *, tm, want_q=True):
    G, R, _ = h.shape
    rope = rope_tabs is not None
    row_spec = pl.BlockSpec((1, tm, D_MODEL), lambda b, t: (b, t, 0))
    in_specs = [
        row_spec,
        pl.BlockSpec((1, 6, D_MODEL), lambda b, t: (b, 0, 0)),
        pl.BlockSpec((1, D_MODEL), lambda b, t: (0, 0)),
        pl.BlockSpec((D_MODEL, 3 * D_MODEL), lambda b, t: (0, 0)),
    ]
    args = [h, mod, g, w]
    if rope:
        in_specs += [pl.BlockSpec((tm, LANES), lambda b, t: (t, 0))] * 3
        args += list(rope_tabs)
    n_out = 3 if want_q else 2
    return pl.pallas_call(
        functools.partial(_qkv_kernel, rope=rope, want_q=want_q),
        grid=(G, R // tm),
        in_specs=in_specs,
        out_specs=[row_spec] * n_out,
        out_shape=[jax.ShapeDtypeStruct((G, R, D_MODEL), BF16)] * n_out,
        compiler_params=pltpu.CompilerParams(vmem_limit_bytes=VMEM_LIMIT),
        name="qkv_rope" if rope else "qkv",
    )(*args)


def _rope_tables():
    t = jnp.arange(SEQ)
    row = (t // GRID_W).astype(F32)
    col = (t % GRID_W).astype(F32)
    half = HEAD_DIM // 2
    freqs = 1.0 / (ROPE_THETA ** (jnp.arange(0, half, 2, dtype=F32) / half))
    ar = row[:, None] * freqs
    ac = col[:, None] * freqs
    ang = jnp.concatenate([ar, ar, ac, ac], axis=-1)
    cos, sin = jnp.cos(ang), jnp.sin(ang)
    first = (np.arange(HEAD_DIM) % half) < half // 2
    sa = jnp.where(first, -sin, 0.0)
    sb = jnp.where(first, 0.0, sin)
    rep = LANES // HEAD_DIM
    return tuple(jnp.tile(x, (1, rep)) for x in (cos, sa, sb))


def _stack_halves(q):
    lane = lax.broadcasted_iota(jnp.int32, q.shape, 1)
    zero = jnp.zeros_like(q)
    return jnp.concatenate([jnp.where(lane < HEAD_DIM, q, zero),
                            jnp.where(lane >= HEAD_DIM, q, zero)], axis=0)


def _with_ones(v):
    return jnp.concatenate([v, jnp.ones(v.shape, v.dtype)], axis=1)


def _pv_parts(e_parts, v_parts):
    acc = None
    for e, v in zip(e_parts, v_parts):
        pv = _dot(e.astype(BF16), v)
        acc = pv if acc is None else acc + pv
    return acc


def _diff_attn_kernel(*refs, nseg, lam_init, block_rows, heads):
    lam_ref, subln_ref, q_ref = refs[:3]
    k_refs = refs[3:3 + nseg]
    v_refs = refs[3 + nseg:3 + 2 * nseg]
    o_ref = refs[3 + 2 * nseg]

    lp = lam_ref[...]
    lam = (jnp.exp(jnp.sum(lp[0:1] * lp[1:2], axis=-1, keepdims=True))
           - jnp.exp(jnp.sum(lp[2:3] * lp[3:4], axis=-1, keepdims=True)) + lam_init)
    subln = subln_ref[...] * (1.0 - lam_init)
    v_aug = [[_with_ones(v_ref[0, :, head * LANES:(head + 1) * LANES]) for v_ref in v_refs]
             for head in range(heads)]

    def block(head, row0, tq):
        cols = slice(head * LANES, (head + 1) * LANES)
        qq = _stack_halves(q_ref[0, row0:row0 + tq, cols])
        s_parts = [_dot_nt(qq, k_ref[0, :, cols]) for k_ref in k_refs]
        m = s_parts[0].max(axis=-1, keepdims=True)
        for s in s_parts[1:]:
            m = jnp.maximum(m, s.max(axis=-1, keepdims=True))
        acc = _pv_parts([jnp.exp2(s - m) for s in s_parts], v_aug[head])
        w = 1.0 / acc[:, LANES:]
        o = acc[:tq, :LANES] * w[:tq] - acc[tq:, :LANES] * (w[tq:] * lam)
        on = o * lax.rsqrt(jnp.mean(o * o, axis=-1, keepdims=True) + EPS) * subln
        o_ref[0, row0:row0 + tq, cols] = on.astype(BF16)

    for head in range(heads):
        row0 = 0
        for tq in block_rows:
            block(head, row0, tq)
            row0 += tq


def _diff_attn(lam_params, subln, q, ks, vs, *, block_rows, heads, lam_init):
    B, N, _ = q.shape
    nseg = len(ks)
    assert sum(block_rows) == N

    def head_spec(n):
        return pl.BlockSpec((1, n, heads * LANES), lambda b, h: (b, 0, h))

    kv_specs = [head_spec(k.shape[1]) for k in ks]
    return pl.pallas_call(
        functools.partial(_diff_attn_kernel, nseg=nseg, lam_init=lam_init, block_rows=block_rows,
                          heads=heads),
        grid=(B, N_LANE_TILES // heads),
        in_specs=[
            pl.BlockSpec((4, HEAD_DIM), lambda b, h: (0, 0)),
            pl.BlockSpec((1, LANES), lambda b, h: (0, 0)),
            head_spec(N),
        ] + kv_specs + kv_specs,
        out_specs=head_spec(N),
        out_shape=jax.ShapeDtypeStruct((B, N, D_MODEL), BF16),
        compiler_params=pltpu.CompilerParams(vmem_limit_bytes=VMEM_LIMIT),
        name="diff_attn_%dseg" % nseg,
    )(lam_params, subln, q, *ks, *vs)


def _na_window_start(j):
    return min(max(j - 1, 0), NA_BLOCKS - NA_KBLK)


def _na_bias_type(j):
    return 0 if j == 0 else (2 if j == NA_BLOCKS - 1 else 1)


def _na_block_layout():
    rows = SEQ // GRID_W
    per_block = []
    for j in range(NA_BLOCKS):
        blk = []
        for a in range(NA_Q_ROWS):
            r = NA_Q_ROWS * j + a
            rs = min(max(r - NA_WIN_ROWS // 2, 0), rows - NA_WIN_ROWS)
            row = []
            for u in range(NA_K_ROWS // 2):
                kr = NA_Q_ROWS * _na_window_start(j) + 2 * u
                ok = tuple(rs <= k < rs + NA_WIN_ROWS for k in (kr, kr + 1))
                row.append((kr - r + NA_WIN_ROWS - 1,) + ok)
            blk.append(tuple(row))
        per_block.append(tuple(blk))
    assert all(per_block[j] == per_block[1] for j in range(2, NA_BLOCKS - 1))
    return (per_block[0], per_block[1], per_block[NA_BLOCKS - 1])


NA_LAYOUT = _na_block_layout()
NA_N_DR = 2 * NA_WIN_ROWS - 1
NA_N_PAIR = NA_N_DR + 1


def _na_pair_table(rpb):
    qc = np.arange(GRID_W)
    cs = np.clip(qc - NA_WIN_COLS // 2, 0, GRID_W - NA_WIN_COLS)
    col_ok = (qc[None, :] >= cs[:, None]) & (qc[None, :] < cs[:, None] + NA_WIN_COLS)
    dc = np.clip(qc[None, :] - qc[:, None] + NA_WIN_COLS - 1, 0, 2 * NA_WIN_COLS - 2)
    onehot = (dc[None] == np.arange(2 * NA_WIN_COLS - 1)[:, None, None]).astype(np.float32)
    toep = jnp.einsum('hdj,jqk->hdqk', rpb, jnp.asarray(onehot), precision=lax.Precision.HIGHEST)
    toep = jnp.where(jnp.asarray(col_ok), toep * LOG2E, -jnp.inf)
    padded = jnp.pad(toep, ((0, 0), (1, 1), (0, 0), (0, 0)), constant_values=-jnp.inf)
    pairs = jnp.concatenate([padded[:, :NA_N_PAIR], padded[:, 1:]], axis=-1)
    return pairs.reshape(N_LANE_TILES, 2, NA_N_PAIR, GRID_W, LANES)


def _na_fill_bias(pair_ref, bias_scr):
    lane = lax.broadcasted_iota(jnp.int32, (GRID_W, LANES), 1)
    neg = jnp.full((GRID_W, LANES), -jnp.inf, F32)
    for kind, layout in enumerate(NA_LAYOUT):
        for head in range(2):
            for a, row in enumerate(layout):
                r0 = head * NA_TQ + a * GRID_W
                for u, (d, left_ok, right_ok) in enumerate(row):
                    slab = neg
                    if left_ok or right_ok:
                        assert not left_ok or 0 <= d < NA_N_DR
                        assert not right_ok or 0 <= d + 1 < NA_N_DR
                        slab = pair_ref[0, head, d + 1]
                        if not left_ok:
                            slab = jnp.where(lane < HEAD_DIM, neg, slab)
                        if not right_ok:
                            slab = jnp.where(lane < HEAD_DIM, slab, neg)
                    bias_scr[kind, r0:r0 + GRID_W, u * LANES:(u + 1) * LANES] = slab


def _nbr_attn_kernel(q_ref, k_ref, v_ref, kc_ref, vc_ref, pair_ref, o_ref, bias_scr):
    n_win = NA_K_ROWS * GRID_W
    lane = lax.broadcasted_iota(jnp.int32, (NA_TQ, LANES), 1)

    @pl.when(pl.program_id(1) == 0)
    def _():
        _na_fill_bias(pair_ref, bias_scr)

    for bb in range(q_ref.shape[0]):
        v_aug = _with_ones(v_ref[bb])
        vc_aug = _with_ones(vc_ref[bb])
        for j in range(NA_BLOCKS):
            rows = slice(j * NA_TQ, (j + 1) * NA_TQ)
            win0 = _na_window_start(j) * NA_TQ
            win = slice(win0, win0 + n_win)
            qq = _stack_halves(q_ref[bb, rows, :])
            s_band = _dot_nt(qq, k_ref[bb, win, :]) + bias_scr[_na_bias_type(j)]
            s_ctx = _dot_nt(qq, kc_ref[bb])
            m = jnp.maximum(s_band.max(axis=-1, keepdims=True), s_ctx.max(axis=-1, keepdims=True))
            acc = _pv_parts([jnp.exp2(s_band - m), jnp.exp2(s_ctx - m)], [v_aug[win], vc_aug])
            o2 = acc[:, :LANES] / acc[:, LANES:]
            o_ref[bb, rows, :] = jnp.where(lane < HEAD_DIM, o2[:NA_TQ], o2[NA_TQ:]).astype(BF16)


def _nbr_attn(q, k, v, kc, vc, pair_tab):
    B, N, _ = q.shape

    def head_spec(n):
        return pl.BlockSpec((NA_BATCH_PER_STEP, n, LANES), lambda hp, b: (b, 0, hp))

    pair_spec = pl.BlockSpec((1, 2, NA_N_PAIR, GRID_W, LANES), lambda hp, b: (hp, 0, 0, 0, 0))
    return pl.pallas_call(
        _nbr_attn_kernel,
        grid=(N_LANE_TILES, B // NA_BATCH_PER_STEP),
        in_specs=[head_spec(N), head_spec(N), head_spec(N), head_spec(CTX_LEN), head_spec(CTX_LEN),
                  pair_spec],
        out_specs=head_spec(N),
        out_shape=jax.ShapeDtypeStruct(q.shape, BF16),
        scratch_shapes=[pltpu.VMEM((3, 2 * NA_TQ, NA_K_ROWS * GRID_W), F32)],
        compiler_params=pltpu.CompilerParams(vmem_limit_bytes=VMEM_LIMIT,
                                             dimension_semantics=("arbitrary", "arbitrary")),
        name="nbr_attn",
    )(q, k, v, kc, vc, pair_tab)


FFN_CHUNKS = ((0, 1024), (1024, 1024), (2048, 768))

def _out_ffn_kernel(*refs, final):
    o_ref, h_ref, mod_ref, g_ref, wo_ref, wgu_ref, wd_ref = refs[:7]
    if final:
        gfin_ref, out_ref = refs[7:]
    else:
        (out_ref,) = refs[7:]
    h1 = h_ref[0] + mod_ref[0, 2:3, :] * _dot(o_ref[0], wo_ref[...])
    xn = _rmsnorm_mod(h1, g_ref[...], mod_ref[0, 3:4, :], mod_ref[0, 4:5, :]).astype(BF16)
    acc = None
    for start, size in FFN_CHUNKS:
        gate = _dot(xn, wgu_ref[:, start:start + size])
        up = _dot(xn, wgu_ref[:, D_FF + start:D_FF + start + size])
        act = (gate * jax.nn.sigmoid(gate) * up).astype(BF16)
        part = _dot(act, wd_ref[start:start + size, :])
        acc = part if acc is None else acc + part
    h2 = h1 + mod_ref[0, 5:6, :] * acc
    if final:
        h2 = h2 * lax.rsqrt(jnp.mean(h2 * h2, axis=-1, keepdims=True) + EPS) * gfin_ref[...]
    out_ref[0] = h2


def _out_ffn(o, h, mod, g, wo, wgu, wd, gfin, *, tm):
    G, R, _ = h.shape
    final = gfin is not None
    row_spec = pl.BlockSpec((1, tm, D_MODEL), lambda b, t: (b, t, 0))

    def resident(shape):
        return pl.BlockSpec(shape, lambda b, t: (0, 0), pipeline_mode=pl.Buffered(1))

    in_specs = [
        row_spec,
        row_spec,
        pl.BlockSpec((1, 6, D_MODEL), lambda b, t: (b, 0, 0)),
        pl.BlockSpec((1, D_MODEL), lambda b, t: (0, 0)),
        resident((D_MODEL, D_MODEL)),
        resident((D_MODEL, 2 * D_FF)),
        resident((D_FF, D_MODEL)),
    ]
    args = [o, h, mod, g, wo, wgu, wd]
    if final:
        in_specs.append(pl.BlockSpec((1, D_MODEL), lambda b, t: (0, 0)))
        args.append(gfin)
    return pl.pallas_call(
        functools.partial(_out_ffn_kernel, final=final),
        grid=(G, R // tm),
        in_specs=in_specs,
        out_specs=row_spec,
        out_shape=jax.ShapeDtypeStruct(h.shape, F32),
        compiler_params=pltpu.CompilerParams(vmem_limit_bytes=VMEM_LIMIT),
        name="out_ffn_final" if final else "out_ffn",
    )(*args)


def kernel(x, c, ctx, c_ctx, ada_w, ada_b, norm_mix, norm_ffn, da_wqkv, da_lambda_q1, da_lambda_k1,
           da_lambda_q2, da_lambda_k2, da_subln, da_wo, na_wqkv, na_rpb, na_wo, ffn_w_gate_up,
           ffn_w_down, norm_final):
    B, N, D = x.shape
    L = ctx.shape[1]
    TM = 1024

    cond = jnp.zeros((COND_ROWS, D), F32).at[:B].set(c).at[CTX_MOD_ROW].set(c_ctx)
    mods = _adaln(cond, ada_w, ada_b).reshape(DEPTH, COND_ROWS, 6, D)

    h = x
    hc = ctx.reshape(1, B * L, D)
    rope_tabs = _rope_tables()

    mod_l = mods[0, :B]
    mod_c = mods[0, CTX_MOD_ROW:CTX_MOD_ROW + 1]
    g_mix = norm_mix[0].reshape(1, D)
    g_ffn = norm_ffn[0].reshape(1, D)
    wqkv = da_wqkv[0].astype(BF16)
    q_l, k_l, v_l = _qkv(h, mod_l, g_mix, wqkv, rope_tabs, tm=TM)
    q_c, k_c, v_c = (t.reshape(B, L, D) for t in _qkv(hc, mod_c, g_mix, wqkv, None, tm=TM))
    lam_params = jnp.stack([da_lambda_q1[0], da_lambda_k1[0], da_lambda_q2[0], da_lambda_k2[0]])
    subln = da_subln[0].reshape(1, LANES)
    lam_init = 0.8 - 0.6 * float(np.exp(-0.3 * 0))
    o_l = _diff_attn(lam_params, subln, q_l, [k_c, k_l], [v_c, v_l], block_rows=DA_BLOCK_ROWS,
                     heads=1, lam_init=lam_init)
    o_c = _diff_attn(lam_params, subln, q_c, [k_c], [v_c], block_rows=(L,), heads=N_LANE_TILES,
                     lam_init=lam_init)
    wo = da_wo[0].astype(BF16)
    wgu = ffn_w_gate_up[0].astype(BF16)
    wd = ffn_w_down[0].astype(BF16)
    h = _out_ffn(o_l, h, mod_l, g_ffn, wo, wgu, wd, None, tm=TM)
    hc = _out_ffn(o_c.reshape(1, B * L, D), hc, mod_c, g_ffn, wo, wgu, wd, None, tm=TM)

    mod_l = mods[1, :B]
    mod_c = mods[1, CTX_MOD_ROW:CTX_MOD_ROW + 1]
    g_mix = norm_mix[1].reshape(1, D)
    g_ffn = norm_ffn[1].reshape(1, D)
    wqkv = na_wqkv[0].astype(BF16)
    q_l, k_l, v_l = _qkv(h, mod_l, g_mix, wqkv, None, tm=TM)
    k_c, v_c = (t.reshape(B, L, D) for t in _qkv(hc, mod_c, g_mix, wqkv, None, tm=TM, want_q=False))
    o_l = _nbr_attn(q_l, k_l, v_l, k_c, v_c, _na_pair_table(na_rpb[0]))
    return _out_ffn(o_l, h, mod_l, g_ffn, na_wo[0].astype(BF16), ffn_w_gate_up[1].astype(BF16),
                    ffn_w_down[1].astype(BF16), norm_final.reshape(1, D), tm=TM)
```

```python
import functools

import numpy as np
import jax
import jax.numpy as jnp
from jax import lax
from jax.experimental import pallas as pl
from jax.experimental.pallas import tpu as pltpu

D_MODEL = 1024
BATCH = 16
SEQ = 2048
DEPTH = 2
GRID_W = 64
CTX_LEN = 256
HEAD_DIM = 64
NA_WIN_ROWS = 8
NA_WIN_COLS = 16
D_FF = 2816
ROPE_THETA = 10000.0
EPS = 1e-6
LOG2E = 1.4426950408889634
Q_SCALE = HEAD_DIM ** -0.5 * LOG2E

LANES = 128
N_LANE_TILES = D_MODEL // LANES
COND_ROWS = 24
CTX_MOD_ROW = BATCH
VMEM_LIMIT = 56 * 1024 * 1024

NA_Q_ROWS = 4
NA_K_ROWS = 12
NA_TQ = NA_Q_ROWS * GRID_W
NA_BLOCKS = SEQ // NA_TQ
NA_KBLK = NA_K_ROWS // NA_Q_ROWS
NA_BATCH_PER_STEP = 2
DA_BLOCK_ROWS = (128,) + (256,) * 7 + (128,)
F32 = jnp.float32
BF16 = jnp.bfloat16


def _dot(a, b):
    return jnp.dot(a, b, preferred_element_type=F32)


def _dot_nt(a, b):
    return lax.dot_general(a, b, (((1,), (1,)), ((), ())), preferred_element_type=F32)


def _adaln_kernel(cond_ref, w_ref, b_ref, out_ref):
    cnd = cond_ref[...]
    a = cnd * jax.nn.sigmoid(cnd)
    w = w_ref[0]
    a_hi = a.astype(BF16)
    a_lo = (a - a_hi.astype(F32)).astype(BF16)
    w_hi = w.astype(BF16)
    w_lo = (w - w_hi.astype(F32)).astype(BF16)
    acc = _dot(a_hi, w_hi) + _dot(a_lo, w_hi) + _dot(a_hi, w_lo)
    out_ref[0] = acc + b_ref[0]


def _adaln(cond, ada_w, ada_b):
    n_out = ada_w.shape[-1]
    tn = 1024
    return pl.pallas_call(
        _adaln_kernel,
        grid=(DEPTH, n_out // tn),
        in_specs=[
            pl.BlockSpec((COND_ROWS, D_MODEL), lambda i, j: (0, 0)),
            pl.BlockSpec((1, D_MODEL, tn), lambda i, j: (i, 0, j)),
            pl.BlockSpec((1, 1, tn), lambda i, j: (i, 0, j)),
        ],
        out_specs=pl.BlockSpec((1, COND_ROWS, tn), lambda i, j: (i, 0, j)),
        out_shape=jax.ShapeDtypeStruct((DEPTH, COND_ROWS, n_out), F32),
        compiler_params=pltpu.CompilerParams(vmem_limit_bytes=VMEM_LIMIT),
        name="adaln",
    )(cond, ada_w, ada_b.reshape(DEPTH, 1, n_out))


def _rmsnorm_mod(h, g, shift, scale):
    y = h * lax.rsqrt(jnp.mean(h * h, axis=-1, keepdims=True) + EPS) * g
    return y * (1.0 + scale) + shift


def _qkv_kernel(*refs, rope, want_q):
    if rope:
        h_ref, mod_ref, g_ref, w_ref, cos_ref, sa_ref, sb_ref = refs[:7]
        outs = refs[7:]
    else:
        h_ref, mod_ref, g_ref, w_ref = refs[:4]
        outs = refs[4:]
    xn = _rmsnorm_mod(h_ref[0], g_ref[...], mod_ref[0, 0:1, :], mod_ref[0, 1:2, :]).astype(BF16)

    def rotate(t):
        return (t * cos_ref[...] + pltpu.roll(t, LANES - 16, axis=1) * sa_ref[...]
                + pltpu.roll(t, 16, axis=1) * sb_ref[...])

    parts = ("q", "k", "v") if want_q else ("k", "v")
    for name, o_ref in zip(parts, outs):
        col = {"q": 0, "k": D_MODEL, "v": 2 * D_MODEL}[name]
        acc = _dot(xn, w_ref[:, col:col + D_MODEL])
        if name == "q":
            acc = acc * Q_SCALE
        if rope and name != "v":
            for c in range(N_LANE_TILES):
                sl = slice(c * LANES, (c + 1) * LANES)
                o_ref[0, :, sl] = rotate(acc[:, sl]).astype(BF16)
        else:
            o_ref[0] = acc.astype(BF16)


def _qkv(h, mod, g, w, rope_tabs, *, tm, want_q=True):
    G, R, _ = h.shape
    rope = rope_tabs is not None
    row_spec = pl.BlockSpec((1, tm, D_MODEL), lambda b, t: (b, t, 0))
    in_specs = [
        row_spec,
        pl.BlockSpec((1, 6, D_MODEL), lambda b, t: (b, 0, 0)),
        pl.BlockSpec((1, D_MODEL), lambda b, t: (0, 0)),
        pl.BlockSpec((D_MODEL, 3 * D_MODEL), lambda b, t: (0, 0)),
    ]
    args = [h, mod, g, w]
    if rope:
        in_specs += [pl.BlockSpec((tm, LANES), lambda b, t: (t, 0))] * 3
        args += list(rope_tabs)
    n_out = 3 if want_q else 2
    return pl.pallas_call(
        functools.partial(_qkv_kernel, rope=rope, want_q=want_q),
        grid=(G, R // tm),
        in_specs=in_specs,
        out_specs=[row_spec] * n_out,
        out_shape=[jax.ShapeDtypeStruct((G, R, D_MODEL), BF16)] * n_out,
        compiler_params=pltpu.CompilerParams(vmem_limit_bytes=VMEM_LIMIT),
        name="qkv_rope" if rope else "qkv",
    )(*args)


def _rope_tables():
    t = jnp.arange(SEQ)
    row = (t // GRID_W).astype(F32)
    col = (t % GRID_W).astype(F32)
    half = HEAD_DIM // 2
    freqs = 1.0 / (ROPE_THETA ** (jnp.arange(0, half, 2, dtype=F32) / half))
    ar = row[:, None] * freqs
    ac = col[:, None] * freqs
    ang = jnp.concatenate([ar, ar, ac, ac], axis=-1)
    cos, sin = jnp.cos(ang), jnp.sin(ang)
    first = (np.arange(HEAD_DIM) % half) < half // 2
    sa = jnp.where(first, -sin, 0.0)
    sb = jnp.where(first, 0.0, sin)
    rep = LANES // HEAD_DIM
    return tuple(jnp.tile(x, (1, rep)) for x in (cos, sa, sb))


def _stack_halves(q):
    lane = lax.broadcasted_iota(jnp.int32, q.shape, 1)
    zero = jnp.zeros_like(q)
    return jnp.concatenate([jnp.where(lane < HEAD_DIM, q, zero),
                            jnp.where(lane >= HEAD_DIM, q, zero)], axis=0)


def _with_ones(v):
    return jnp.concatenate([v, jnp.ones(v.shape, v.dtype)], axis=1)


def _diff_attn_kernel(*refs, nseg, lam_init, block_rows, heads):
    lam_ref, subln_ref, q_ref = refs[:3]
    k_refs = refs[3:3 + nseg]
    v_refs = refs[3 + nseg:3 + 2 * nseg]
    o_ref = refs[3 + 2 * nseg]

    lp = lam_ref[...]
    lam = (jnp.exp(jnp.sum(lp[0:1] * lp[1:2], axis=-1, keepdims=True))
           - jnp.exp(jnp.sum(lp[2:3] * lp[3:4], axis=-1, keepdims=True)) + lam_init)
    subln = subln_ref[...] * (1.0 - lam_init)

    def head_cols(ref_list, head):
        return jnp.concatenate([r[0, :, head * LANES:(head + 1) * LANES] for r in ref_list], axis=0)

    keys = [head_cols(k_refs, head) for head in range(heads)]
    v_aug = [_with_ones(head_cols(v_refs, head)) for head in range(heads)]

    def block(head, row0, tq):
        cols = slice(head * LANES, (head + 1) * LANES)
        qq = _stack_halves(q_ref[0, row0:row0 + tq, cols])
        s = _dot_nt(qq, keys[head])
        e = jnp.exp2(s - s.max(axis=-1, keepdims=True))
        acc = _dot(e.astype(BF16), v_aug[head])
        w = 1.0 / acc[:, LANES:]
        o = acc[:tq, :LANES] * w[:tq] - acc[tq:, :LANES] * (w[tq:] * lam)
        on = o * lax.rsqrt(jnp.mean(o * o, axis=-1, keepdims=True) + EPS) * subln
        o_ref[0, row0:row0 + tq, cols] = on.astype(BF16)

    for head in range(heads):
        row0 = 0
        for tq in block_rows:
            block(head, row0, tq)
            row0 += tq


def _diff_attn(lam_params, subln, q, ks, vs, *, block_rows, heads, lam_init):
    B, N, _ = q.shape
    nseg = len(ks)
    assert sum(block_rows) == N

    def head_spec(n):
        return pl.BlockSpec((1, n, heads * LANES), lambda b, h: (b, 0, h))

    kv_specs = [head_spec(k.shape[1]) for k in ks]
    return pl.pallas_call(
        functools.partial(_diff_attn_kernel, nseg=nseg, lam_init=lam_init, block_rows=block_rows,
                          heads=heads),
        grid=(B, N_LANE_TILES // heads),
        in_specs=[
            pl.BlockSpec((4, HEAD_DIM), lambda b, h: (0, 0)),
            pl.BlockSpec((1, LANES), lambda b, h: (0, 0)),
            head_spec(N),
        ] + kv_specs + kv_specs,
        out_specs=head_spec(N),
        out_shape=jax.ShapeDtypeStruct((B, N, D_MODEL), BF16),
        compiler_params=pltpu.CompilerParams(vmem_limit_bytes=VMEM_LIMIT),
        name="diff_attn_%dseg" % nseg,
    )(lam_params, subln, q, *ks, *vs)


def _na_window_start(j):
    return min(max(j - 1, 0), NA_BLOCKS - NA_KBLK)


def _na_bias_type(j):
    return 0 if j == 0 else (2 if j == NA_BLOCKS - 1 else 1)


def _na_block_layout():
    rows = SEQ // GRID_W
    per_block = []
    for j in range(NA_BLOCKS):
        blk = []
        for a in range(NA_Q_ROWS):
            r = NA_Q_ROWS * j + a
            rs = min(max(r - NA_WIN_ROWS // 2, 0), rows - NA_WIN_ROWS)
            row = []
            for u in range(NA_K_ROWS // 2):
                kr = NA_Q_ROWS * _na_window_start(j) + 2 * u
                ok = tuple(rs <= k < rs + NA_WIN_ROWS for k in (kr, kr + 1))
                row.append((kr - r + NA_WIN_ROWS - 1,) + ok)
            blk.append(tuple(row))
        per_block.append(tuple(blk))
    assert all(per_block[j] == per_block[1] for j in range(2, NA_BLOCKS - 1))
    return (per_block[0], per_block[1], per_block[NA_BLOCKS - 1])


NA_LAYOUT = _na_block_layout()
NA_N_DR = 2 * NA_WIN_ROWS - 1
NA_N_PAIR = NA_N_DR + 1


def _na_pair_table(rpb):
    qc = np.arange(GRID_W)
    cs = np.clip(qc - NA_WIN_COLS // 2, 0, GRID_W - NA_WIN_COLS)
    col_ok = (qc[None, :] >= cs[:, None]) & (qc[None, :] < cs[:, None] + NA_WIN_COLS)
    dc = np.clip(qc[None, :] - qc[:, None] + NA_WIN_COLS - 1, 0, 2 * NA_WIN_COLS - 2)
    onehot = (dc[None] == np.arange(2 * NA_WIN_COLS - 1)[:, None, None]).astype(np.float32)
    toep = jnp.einsum('hdj,jqk->hdqk', rpb, jnp.asarray(onehot), precision=lax.Precision.HIGHEST)
    toep = jnp.where(jnp.asarray(col_ok), toep * LOG2E, -jnp.inf)
    padded = jnp.pad(toep, ((0, 0), (1, 1), (0, 0), (0, 0)), constant_values=-jnp.inf)
    pairs = jnp.concatenate([padded[:, :NA_N_PAIR], padded[:, 1:]], axis=-1)
    return pairs.reshape(N_LANE_TILES, 2, NA_N_PAIR, GRID_W, LANES)


def _na_fill_bias(pair_ref, bias_scr):
    lane = lax.broadcasted_iota(jnp.int32, (GRID_W, LANES), 1)
    neg = jnp.full((GRID_W, LANES), -jnp.inf, F32)
    for kind, layout in enumerate(NA_LAYOUT):
        for head in range(2):
            for a, row in enumerate(layout):
                r0 = head * NA_TQ + a * GRID_W
                for u, (d, left_ok, right_ok) in enumerate(row):
                    slab = neg
                    if left_ok or right_ok:
                        assert not left_ok or 0 <= d < NA_N_DR
                        assert not right_ok or 0 <= d + 1 < NA_N_DR
                        slab = pair_ref[0, head, d + 1]
                        if not left_ok:
                            slab = jnp.where(lane < HEAD_DIM, neg, slab)
                        if not right_ok:
                            slab = jnp.where(lane < HEAD_DIM, slab, neg)
                    bias_scr[kind, r0:r0 + GRID_W, u * LANES:(u + 1) * LANES] = slab


def _nbr_attn_kernel(q_ref, k_ref, v_ref, kc_ref, vc_ref, pair_ref, o_ref, bias_scr):
    n_win = NA_K_ROWS * GRID_W
    lane = lax.broadcasted_iota(jnp.int32, (NA_TQ, LANES), 1)

    @pl.when(pl.program_id(1) == 0)
    def _():
        _na_fill_bias(pair_ref, bias_scr)

    for bb in range(q_ref.shape[0]):
        v_aug = _with_ones(v_ref[bb])
        vc_aug = _with_ones(vc_ref[bb])
        for j in range(NA_BLOCKS):
            rows = slice(j * NA_TQ, (j + 1) * NA_TQ)
            win0 = _na_window_start(j) * NA_TQ
            win = slice(win0, win0 + n_win)
            qq = _stack_halves(q_ref[bb, rows, :])
            keys = jnp.concatenate([k_ref[bb, win, :], kc_ref[bb]], axis=0)
            s = _dot_nt(qq, keys)
            s_band = s[:, :n_win] + bias_scr[_na_bias_type(j)]
            s_ctx = s[:, n_win:]
            m = jnp.maximum(s_band.max(axis=-1, keepdims=True), s_ctx.max(axis=-1, keepdims=True))
            e = jnp.concatenate([jnp.exp2(s_band - m), jnp.exp2(s_ctx - m)], axis=1).astype(BF16)
            acc = _dot(e, jnp.concatenate([v_aug[win], vc_aug], axis=0))
            o2 = acc[:, :LANES] / acc[:, LANES:]
            o_ref[bb, rows, :] = jnp.where(lane < HEAD_DIM, o2[:NA_TQ], o2[NA_TQ:]).astype(BF16)


def _nbr_attn(q, k, v, kc, vc, pair_tab):
    B, N, _ = q.shape

    def head_spec(n):
        return pl.BlockSpec((NA_BATCH_PER_STEP, n, LANES), lambda hp, b: (b, 0, hp))

    pair_spec = pl.BlockSpec((1, 2, NA_N_PAIR, GRID_W, LANES), lambda hp, b: (hp, 0, 0, 0, 0))
    return pl.pallas_call(
        _nbr_attn_kernel,
        grid=(N_LANE_TILES, B // NA_BATCH_PER_STEP),
        in_specs=[head_spec(N), head_spec(N), head_spec(N), head_spec(CTX_LEN), head_spec(CTX_LEN),
                  pair_spec],
        out_specs=head_spec(N),
        out_shape=jax.ShapeDtypeStruct(q.shape, BF16),
        scratch_shapes=[pltpu.VMEM((3, 2 * NA_TQ, NA_K_ROWS * GRID_W), F32)],
        compiler_params=pltpu.CompilerParams(vmem_limit_bytes=VMEM_LIMIT,
                                             dimension_semantics=("arbitrary", "arbitrary")),
        name="nbr_attn",
    )(q, k, v, kc, vc, pair_tab)


FFN_CHUNKS = ((0, 1024), (1024, 1024), (2048, 768))

def _out_ffn_kernel(*refs, final):
    o_ref, h_ref, mod_ref, g_ref, wo_ref, wgu_ref, wd_ref = refs[:7]
    if final:
        gfin_ref, out_ref = refs[7:]
    else:
        (out_ref,) = refs[7:]
    h1 = h_ref[0] + mod_ref[0, 2:3, :] * _dot(o_ref[0], wo_ref[...])
    xn = _rmsnorm_mod(h1, g_ref[...], mod_ref[0, 3:4, :], mod_ref[0, 4:5, :]).astype(BF16)
    acc = None
    for start, size in FFN_CHUNKS:
        gate = _dot(xn, wgu_ref[:, start:start + size])
        up = _dot(xn, wgu_ref[:, D_FF + start:D_FF + start + size])
        act = (gate * jax.nn.sigmoid(gate) * up).astype(BF16)
        part = _dot(act, wd_ref[start:start + size, :])
        acc = part if acc is None else acc + part
    h2 = h1 + mod_ref[0, 5:6, :] * acc
    if final:
        h2 = h2 * lax.rsqrt(jnp.mean(h2 * h2, axis=-1, keepdims=True) + EPS) * gfin_ref[...]
    out_ref[0] = h2


def _out_ffn(o, h, mod, g, wo, wgu, wd, layer, gfin, *, tm):
    G, R, _ = h.shape
    final = gfin is not None
    row_spec = pl.BlockSpec((1, tm, D_MODEL), lambda b, t: (b, t, 0))

    def resident(shape, index):
        return pl.BlockSpec(shape, lambda b, t: index, pipeline_mode=pl.Buffered(1))

    in_specs = [
        row_spec,
        row_spec,
        pl.BlockSpec((1, 6, D_MODEL), lambda b, t: (b, 0, 0)),
        pl.BlockSpec((1, D_MODEL), lambda b, t: (0, 0)),
        resident((D_MODEL, D_MODEL), (0, 0)),
        resident((None, D_MODEL, 2 * D_FF), (layer, 0, 0)),
        resident((None, D_FF, D_MODEL), (layer, 0, 0)),
    ]
    args = [o, h, mod, g, wo, wgu, wd]
    if final:
        in_specs.append(pl.BlockSpec((1, D_MODEL), lambda b, t: (0, 0)))
        args.append(gfin)
    return pl.pallas_call(
        functools.partial(_out_ffn_kernel, final=final),
        grid=(G, R // tm),
        in_specs=in_specs,
        out_specs=row_spec,
        out_shape=jax.ShapeDtypeStruct(h.shape, F32),
        compiler_params=pltpu.CompilerParams(vmem_limit_bytes=VMEM_LIMIT),
        name="out_ffn_final" if final else "out_ffn",
    )(*args)


def kernel(x, c, ctx, c_ctx, ada_w, ada_b, norm_mix, norm_ffn, da_wqkv, da_lambda_q1, da_lambda_k1,
           da_lambda_q2, da_lambda_k2, da_subln, da_wo, na_wqkv, na_rpb, na_wo, ffn_w_gate_up,
           ffn_w_down, norm_final):
    B, N, D = x.shape
    L = ctx.shape[1]
    TM = 1024

    cond = jnp.zeros((COND_ROWS, D), F32).at[:B].set(c).at[CTX_MOD_ROW].set(c_ctx)
    mods = _adaln(cond, ada_w, ada_b).reshape(DEPTH, COND_ROWS, 6, D)

    h = x
    hc = ctx.reshape(1, B * L, D)
    rope_tabs = _rope_tables()

    mod_l = mods[0, :B]
    mod_c = mods[0, CTX_MOD_ROW:CTX_MOD_ROW + 1]
    g_mix = norm_mix[0].reshape(1, D)
    g_ffn = norm_ffn[0].reshape(1, D)
    wqkv = da_wqkv[0].astype(BF16)
    q_l, k_l, v_l = _qkv(h, mod_l, g_mix, wqkv, rope_tabs, tm=TM)
    q_c, k_c, v_c = (t.reshape(B, L, D) for t in _qkv(hc, mod_c, g_mix, wqkv, None, tm=TM))
    lam_params = jnp.stack([da_lambda_q1[0], da_lambda_k1[0], da_lambda_q2[0], da_lambda_k2[0]])
    subln = da_subln[0].reshape(1, LANES)
    lam_init = 0.8 - 0.6 * float(np.exp(-0.3 * 0))
    o_l = _diff_attn(lam_params, subln, q_l, [k_c, k_l], [v_c, v_l], block_rows=DA_BLOCK_ROWS,
                     heads=1, lam_init=lam_init)
    o_c = _diff_attn(lam_params, subln, q_c, [k_c], [v_c], block_rows=(L,), heads=N_LANE_TILES,
                     lam_init=lam_init)
    wo = da_wo[0].astype(BF16)
    wgu = ffn_w_gate_up.astype(BF16)
    wd = ffn_w_down.astype(BF16)
    h = _out_ffn(o_l, h, mod_l, g_ffn, wo, wgu, wd, 0, None, tm=TM)
    hc = _out_ffn(o_c.reshape(1, B * L, D), hc, mod_c, g_ffn, wo, wgu, wd, 0, None, tm=TM)

    mod_l = mods[1, :B]
    mod_c = mods[1, CTX_MOD_ROW:CTX_MOD_ROW + 1]
    g_mix = norm_mix[1].reshape(1, D)
    g_ffn = norm_ffn[1].reshape(1, D)
    wqkv = na_wqkv[0].astype(BF16)
    q_l, k_l, v_l = _qkv(h, mod_l, g_mix, wqkv, None, tm=TM)
    k_c, v_c = (t.reshape(B, L, D) for t in _qkv(hc, mod_c, g_mix, wqkv, None, tm=TM, want_q=False))
    o_l = _nbr_attn(q_l, k_l, v_l, k_c, v_c, _na_pair_table(na_rpb[0]))
    return _out_ffn(o_l, h, mod_l, g_ffn, na_wo[0].astype(BF16), wgu, wd, 1,
                    norm_final.reshape(1, D), tm=TM)
```

```python
import functools

import numpy as np
import jax
import jax.numpy as jnp
from jax import lax
from jax.experimental import pallas as pl
from jax.experimental.pallas import tpu as pltpu

D_MODEL = 1024
BATCH = 16
SEQ = 2048
DEPTH = 2
GRID_W = 64
CTX_LEN = 256
HEAD_DIM = 64
NA_WIN_ROWS = 8
NA_WIN_COLS = 16
D_FF = 2816
ROPE_THETA = 10000.0
EPS = 1e-6
LOG2E = 1.4426950408889634
Q_SCALE = HEAD_DIM ** -0.5 * LOG2E

LANES = 128
N_LANE_TILES = D_MODEL // LANES
COND_ROWS = 24
CTX_MOD_ROW = BATCH
VMEM_LIMIT = 56 * 1024 * 1024

NA_Q_ROWS = 4
NA_K_ROWS = 12
NA_TQ = NA_Q_ROWS * GRID_W
NA_BLOCKS = SEQ // NA_TQ
NA_KBLK = NA_K_ROWS // NA_Q_ROWS
NA_BATCH_PER_STEP = 4
DA_HEADS_PER_STEP = 2
DA_BLOCK_ROWS = (128,) + (256,) * 7 + (128,)
F32 = jnp.float32
BF16 = jnp.bfloat16


def _dot(a, b):
    return jnp.dot(a, b, preferred_element_type=F32)


def _dot_nt(a, b):
    return lax.dot_general(a, b, (((1,), (1,)), ((), ())), preferred_element_type=F32)


def _adaln_kernel(cond_ref, w_ref, b_ref, out_ref):
    cnd = cond_ref[...]
    a = cnd * jax.nn.sigmoid(cnd)
    w = w_ref[0]
    a_hi = a.astype(BF16)
    a_lo = (a - a_hi.astype(F32)).astype(BF16)
    w_hi = w.astype(BF16)
    w_lo = (w - w_hi.astype(F32)).astype(BF16)
    acc = _dot(a_hi, w_hi) + _dot(a_lo, w_hi) + _dot(a_hi, w_lo)
    out_ref[0] = acc + b_ref[0]


def _adaln(cond, ada_w, ada_b):
    n_out = ada_w.shape[-1]
    tn = 1024
    return pl.pallas_call(
        _adaln_kernel,
        grid=(DEPTH, n_out // tn),
        in_specs=[
            pl.BlockSpec((COND_ROWS, D_MODEL), lambda i, j: (0, 0)),
            pl.BlockSpec((1, D_MODEL, tn), lambda i, j: (i, 0, j)),
            pl.BlockSpec((1, 1, tn), lambda i, j: (i, 0, j)),
        ],
        out_specs=pl.BlockSpec((1, COND_ROWS, tn), lambda i, j: (i, 0, j)),
        out_shape=jax.ShapeDtypeStruct((DEPTH, COND_ROWS, n_out), F32),
        compiler_params=pltpu.CompilerParams(vmem_limit_bytes=VMEM_LIMIT),
        name="adaln",
    )(cond, ada_w, ada_b.reshape(DEPTH, 1, n_out))


def _rmsnorm_mod(h, g, shift, scale):
    y = h * lax.rsqrt(jnp.mean(h * h, axis=-1, keepdims=True) + EPS) * g
    return y * (1.0 + scale) + shift


def _qkv_kernel(*refs, rope, want_q):
    if rope:
        h_ref, mod_ref, g_ref, w_ref, cos_ref, sa_ref, sb_ref = refs[:7]
        outs = refs[7:]
    else:
        h_ref, mod_ref, g_ref, w_ref = refs[:4]
        outs = refs[4:]
    xn = _rmsnorm_mod(h_ref[0], g_ref[...], mod_ref[0, 0:1, :], mod_ref[0, 1:2, :]).astype(BF16)

    def rotate(t):
        return (t * cos_ref[...] + pltpu.roll(t, LANES - 16, axis=1) * sa_ref[...]
                + pltpu.roll(t, 16, axis=1) * sb_ref[...])

    parts = ("q", "k", "v") if want_q else ("k", "v")
    for name, o_ref in zip(parts, outs):
        col = {"q": 0, "k": D_MODEL, "v": 2 * D_MODEL}[name]
        acc = _dot(xn, w_ref[:, col:col + D_MODEL])
        if name == "q":
            acc = acc * Q_SCALE
        if rope and name != "v":
            for c in range(N_LANE_TILES):
                sl = slice(c * LANES, (c + 1) * LANES)
                o_ref[0, :, sl] = rotate(acc[:, sl]).astype(BF16)
        else:
            o_ref[0] = acc.astype(BF16)


def _qkv(h, mod, g, w, rope_tabs, *, tm, want_q=True):
    G, R, _ = h.shape
    rope = rope_tabs is not None
    row_spec = pl.BlockSpec((1, tm, D_MODEL), lambda b, t: (b, t, 0))
    in_specs = [
        row_spec,
        pl.BlockSpec((1, 6, D_MODEL), lambda b, t: (b, 0, 0)),
        pl.BlockSpec((1, D_MODEL), lambda b, t: (0, 0)),
        pl.BlockSpec((D_MODEL, 3 * D_MODEL), lambda b, t: (0, 0)),
    ]
    args = [h, mod, g, w]
    if rope:
        in_specs += [pl.BlockSpec((tm, LANES), lambda b, t: (t, 0))] * 3
        args += list(rope_tabs)
    n_out = 3 if want_q else 2
    return pl.pallas_call(
        functools.partial(_qkv_kernel, rope=rope, want_q=want_q),
        grid=(G, R // tm),
        in_specs=in_specs,
        out_specs=[row_spec] * n_out,
        out_shape=[jax.ShapeDtypeStruct((G, R, D_MODEL), BF16)] * n_out,
        compiler_params=pltpu.CompilerParams(vmem_limit_bytes=VMEM_LIMIT),
        name="qkv_rope" if rope else "qkv",
    )(*args)


def _rope_tables():
    t = jnp.arange(SEQ)
    row = (t // GRID_W).astype(F32)
    col = (t % GRID_W).astype(F32)
    half = HEAD_DIM // 2
    freqs = 1.0 / (ROPE_THETA ** (jnp.arange(0, half, 2, dtype=F32) / half))
    ar = row[:, None] * freqs
    ac = col[:, None] * freqs
    ang = jnp.concatenate([ar, ar, ac, ac], axis=-1)
    cos, sin = jnp.cos(ang), jnp.sin(ang)
    first = (np.arange(HEAD_DIM) % half) < half // 2
    sa = jnp.where(first, -sin, 0.0)
    sb = jnp.where(first, 0.0, sin)
    rep = LANES // HEAD_DIM
    return tuple(jnp.tile(x, (1, rep)) for x in (cos, sa, sb))


def _stack_halves(q):
    lane = lax.broadcasted_iota(jnp.int32, q.shape, 1)
    zero = jnp.zeros_like(q)
    return jnp.concatenate([jnp.where(lane < HEAD_DIM, q, zero),
                            jnp.where(lane >= HEAD_DIM, q, zero)], axis=0)


def _with_ones(v):
    return jnp.concatenate([v, jnp.ones(v.shape, v.dtype)], axis=1)


def _diff_attn_kernel(*refs, nseg, lam_init, block_rows, heads):
    lam_ref, subln_ref, q_ref = refs[:3]
    k_refs = refs[3:3 + nseg]
    v_refs = refs[3 + nseg:3 + 2 * nseg]
    o_ref = refs[3 + 2 * nseg]

    lp = lam_ref[...]
    lam = (jnp.exp(jnp.sum(lp[0:1] * lp[1:2], axis=-1, keepdims=True))
           - jnp.exp(jnp.sum(lp[2:3] * lp[3:4], axis=-1, keepdims=True)) + lam_init)
    subln = subln_ref[...] * (1.0 - lam_init)

    v_aug = [[_with_ones(v_ref[0, :, head * LANES:(head + 1) * LANES]) for v_ref in v_refs]
             for head in range(heads)]

    def block(head, row0, tq):
        cols = slice(head * LANES, (head + 1) * LANES)
        qq = _stack_halves(q_ref[0, row0:row0 + tq, cols])
        s_parts = [_dot_nt(qq, k_ref[0, :, cols]) for k_ref in k_refs]
        m = s_parts[0].max(axis=-1, keepdims=True)
        for s in s_parts[1:]:
            m = jnp.maximum(m, s.max(axis=-1, keepdims=True))
        acc = None
        for s, v in zip(s_parts, v_aug[head]):
            pv = _dot(jnp.exp2(s - m).astype(BF16), v)
            acc = pv if acc is None else acc + pv
        w = 1.0 / acc[:, LANES:]
        o = acc[:tq, :LANES] * w[:tq] - acc[tq:, :LANES] * (w[tq:] * lam)
        on = o * lax.rsqrt(jnp.mean(o * o, axis=-1, keepdims=True) + EPS) * subln
        o_ref[0, row0:row0 + tq, cols] = on.astype(BF16)

    for head in range(heads):
        row0 = 0
        for tq in block_rows:
            block(head, row0, tq)
            row0 += tq


def _diff_attn(lam_params, subln, q, ks, vs, *, block_rows, heads, lam_init):
    B, N, _ = q.shape
    nseg = len(ks)
    assert sum(block_rows) == N

    def head_spec(n):
        return pl.BlockSpec((1, n, heads * LANES), lambda b, h: (b, 0, h))

    kv_specs = [head_spec(k.shape[1]) for k in ks]
    return pl.pallas_call(
        functools.partial(_diff_attn_kernel, nseg=nseg, lam_init=lam_init, block_rows=block_rows,
                          heads=heads),
        grid=(B, N_LANE_TILES // heads),
        in_specs=[
            pl.BlockSpec((4, HEAD_DIM), lambda b, h: (0, 0)),
            pl.BlockSpec((1, LANES), lambda b, h: (0, 0)),
            head_spec(N),
        ] + kv_specs + kv_specs,
        out_specs=head_spec(N),
        out_shape=jax.ShapeDtypeStruct((B, N, D_MODEL), BF16),
        compiler_params=pltpu.CompilerParams(vmem_limit_bytes=VMEM_LIMIT),
        name="diff_attn_%dseg" % nseg,
    )(lam_params, subln, q, *ks, *vs)


def _na_window_start(j):
    return min(max(j - 1, 0), NA_BLOCKS - NA_KBLK)


def _na_bias_type(j):
    return 0 if j == 0 else (2 if j == NA_BLOCKS - 1 else 1)


def _na_block_layout():
    rows = SEQ // GRID_W
    per_block = []
    for j in range(NA_BLOCKS):
        blk = []
        for a in range(NA_Q_ROWS):
            r = NA_Q_ROWS * j + a
            rs = min(max(r - NA_WIN_ROWS // 2, 0), rows - NA_WIN_ROWS)
            row = []
            for u in range(NA_K_ROWS // 2):
                kr = NA_Q_ROWS * _na_window_start(j) + 2 * u
                ok = tuple(rs <= k < rs + NA_WIN_ROWS for k in (kr, kr + 1))
                row.append((kr - r + NA_WIN_ROWS - 1,) + ok)
            blk.append(tuple(row))
        per_block.append(tuple(blk))
    assert all(per_block[j] == per_block[1] for j in range(2, NA_BLOCKS - 1))
    return (per_block[0], per_block[1], per_block[NA_BLOCKS - 1])


NA_LAYOUT = _na_block_layout()
NA_N_DR = 2 * NA_WIN_ROWS - 1
NA_N_PAIR = NA_N_DR + 1


def _na_pair_table(rpb):
    qc = np.arange(GRID_W)
    cs = np.clip(qc - NA_WIN_COLS // 2, 0, GRID_W - NA_WIN_COLS)
    col_ok = (qc[None, :] >= cs[:, None]) & (qc[None, :] < cs[:, None] + NA_WIN_COLS)
    dc = np.clip(qc[None, :] - qc[:, None] + NA_WIN_COLS - 1, 0, 2 * NA_WIN_COLS - 2)
    onehot = (dc[None] == np.arange(2 * NA_WIN_COLS - 1)[:, None, None]).astype(np.float32)
    toep = jnp.einsum('hdj,jqk->hdqk', rpb, jnp.asarray(onehot), precision=lax.Precision.HIGHEST)
    toep = jnp.where(jnp.asarray(col_ok), toep * LOG2E, -jnp.inf)
    padded = jnp.pad(toep, ((0, 0), (1, 1), (0, 0), (0, 0)), constant_values=-jnp.inf)
    pairs = jnp.concatenate([padded[:, :NA_N_PAIR], padded[:, 1:]], axis=-1)
    return pairs.reshape(N_LANE_TILES, 2, NA_N_PAIR, GRID_W, LANES)


def _na_fill_bias(pair_ref, bias_scr):
    lane = lax.broadcasted_iota(jnp.int32, (GRID_W, LANES), 1)
    neg = jnp.full((GRID_W, LANES), -jnp.inf, F32)
    for kind, layout in enumerate(NA_LAYOUT):
        for head in range(2):
            for a, row in enumerate(layout):
                r0 = head * NA_TQ + a * GRID_W
                for u, (d, left_ok, right_ok) in enumerate(row):
                    slab = neg
                    if left_ok or right_ok:
                        assert not left_ok or 0 <= d < NA_N_DR
                        assert not right_ok or 0 <= d + 1 < NA_N_DR
                        slab = pair_ref[0, head, d + 1]
                        if not left_ok:
                            slab = jnp.where(lane < HEAD_DIM, neg, slab)
                        if not right_ok:
                            slab = jnp.where(lane < HEAD_DIM, slab, neg)
                    bias_scr[kind, r0:r0 + GRID_W, u * LANES:(u + 1) * LANES] = slab


def _nbr_attn_kernel(q_ref, k_ref, v_ref, kc_ref, vc_ref, pair_ref, o_ref, bias_scr):
    n_win = NA_K_ROWS * GRID_W
    lane = lax.broadcasted_iota(jnp.int32, (NA_TQ, LANES), 1)

    @pl.when(pl.program_id(1) == 0)
    def _():
        _na_fill_bias(pair_ref, bias_scr)

    for bb in range(q_ref.shape[0]):
        v_aug = _with_ones(v_ref[bb])
        vc_aug = _with_ones(vc_ref[bb])
        for j in range(NA_BLOCKS):
            rows = slice(j * NA_TQ, (j + 1) * NA_TQ)
            win0 = _na_window_start(j) * NA_TQ
            win = slice(win0, win0 + n_win)
            qq = _stack_halves(q_ref[bb, rows, :])
            bias = bias_scr[_na_bias_type(j)]
            t1 = win0 + NA_TQ
            k_a = jnp.concatenate([kc_ref[bb], k_ref[bb, win0:t1, :]], axis=0)
            v_a = jnp.concatenate([vc_aug, v_aug[win0:t1]], axis=0)
            s_a = _dot_nt(qq, k_a)
            s_a = jnp.concatenate([s_a[:, :CTX_LEN], s_a[:, CTX_LEN:] + bias[:, :NA_TQ]], axis=1)
            m_a = s_a.max(axis=-1, keepdims=True)
            acc_a = _dot(jnp.exp2(s_a - m_a).astype(BF16), v_a)
            s_b = _dot_nt(qq, k_ref[bb, t1:win0 + n_win, :]) + bias[:, NA_TQ:]
            m_b = s_b.max(axis=-1, keepdims=True)
            acc_b = _dot(jnp.exp2(s_b - m_b).astype(BF16), v_aug[t1:win0 + n_win])
            m = jnp.maximum(m_a, m_b)
            acc = acc_a * jnp.exp2(m_a - m) + acc_b * jnp.exp2(m_b - m)
            o2 = acc[:, :LANES] / acc[:, LANES:]
            o_ref[bb, rows, :] = jnp.where(lane < HEAD_DIM, o2[:NA_TQ], o2[NA_TQ:]).astype(BF16)


def _nbr_attn(q, k, v, kc, vc, pair_tab):
    B, N, _ = q.shape

    def head_spec(n):
        return pl.BlockSpec((NA_BATCH_PER_STEP, n, LANES), lambda hp, b: (b, 0, hp))

    pair_spec = pl.BlockSpec((1, 2, NA_N_PAIR, GRID_W, LANES), lambda hp, b: (hp, 0, 0, 0, 0))
    return pl.pallas_call(
        _nbr_attn_kernel,
        grid=(N_LANE_TILES, B // NA_BATCH_PER_STEP),
        in_specs=[head_spec(N), head_spec(N), head_spec(N), head_spec(CTX_LEN), head_spec(CTX_LEN),
                  pair_spec],
        out_specs=head_spec(N),
        out_shape=jax.ShapeDtypeStruct(q.shape, BF16),
        scratch_shapes=[pltpu.VMEM((3, 2 * NA_TQ, NA_K_ROWS * GRID_W), F32)],
        compiler_params=pltpu.CompilerParams(vmem_limit_bytes=VMEM_LIMIT,
                                             dimension_semantics=("arbitrary", "arbitrary")),
        name="nbr_attn",
    )(q, k, v, kc, vc, pair_tab)


FFN_CHUNKS = ((0, 1024), (1024, 1024), (2048, 768))

def _out_ffn_kernel(*refs, final):
    o_ref, h_ref, mod_ref, g_ref, wo_ref, wgu_ref, wd_ref = refs[:7]
    if final:
        gfin_ref, out_ref = refs[7:]
    else:
        (out_ref,) = refs[7:]
    h1 = h_ref[0] + mod_ref[0, 2:3, :] * _dot(o_ref[0], wo_ref[...])
    xn = _rmsnorm_mod(h1, g_ref[...], mod_ref[0, 3:4, :], mod_ref[0, 4:5, :]).astype(BF16)
    acc = None
    for start, size in FFN_CHUNKS:
        gate = _dot(xn, wgu_ref[:, start:start + size])
        up = _dot(xn, wgu_ref[:, D_FF + start:D_FF + start + size])
        act = (gate * jax.nn.sigmoid(gate) * up).astype(BF16)
        part = _dot(act, wd_ref[start:start + size, :])
        acc = part if acc is None else acc + part
    h2 = h1 + mod_ref[0, 5:6, :] * acc
    if final:
        h2 = h2 * lax.rsqrt(jnp.mean(h2 * h2, axis=-1, keepdims=True) + EPS) * gfin_ref[...]
    out_ref[0] = h2


def _out_ffn(o, h, mod, g, wo, wgu, wd, layer, gfin, *, tm):
    G, R, _ = h.shape
    final = gfin is not None
    row_spec = pl.BlockSpec((1, tm, D_MODEL), lambda b, t: (b, t, 0))

    def resident(shape, index):
        return pl.BlockSpec(shape, lambda b, t: index, pipeline_mode=pl.Buffered(1))

    in_specs = [
        row_spec,
        row_spec,
        pl.BlockSpec((1, 6, D_MODEL), lambda b, t: (b, 0, 0)),
        pl.BlockSpec((1, D_MODEL), lambda b, t: (0, 0)),
        resident((D_MODEL, D_MODEL), (0, 0)),
        resident((None, D_MODEL, 2 * D_FF), (layer, 0, 0)),
        resident((None, D_FF, D_MODEL), (layer, 0, 0)),
    ]
    args = [o, h, mod, g, wo, wgu, wd]
    if final:
        in_specs.append(pl.BlockSpec((1, D_MODEL), lambda b, t: (0, 0)))
        args.append(gfin)
    return pl.pallas_call(
        functools.partial(_out_ffn_kernel, final=final),
        grid=(G, R // tm),
        in_specs=in_specs,
        out_specs=row_spec,
        out_shape=jax.ShapeDtypeStruct(h.shape, F32),
        compiler_params=pltpu.CompilerParams(vmem_limit_bytes=VMEM_LIMIT),
        name="out_ffn_final" if final else "out_ffn",
    )(*args)


def kernel(x, c, ctx, c_ctx, ada_w, ada_b, norm_mix, norm_ffn, da_wqkv, da_lambda_q1, da_lambda_k1,
           da_lambda_q2, da_lambda_k2, da_subln, da_wo, na_wqkv, na_rpb, na_wo, ffn_w_gate_up,
           ffn_w_down, norm_final):
    B, N, D = x.shape
    L = ctx.shape[1]
    TM = 1024
    assert (B, N, D, L) == (BATCH, SEQ, D_MODEL, CTX_LEN) and ctx.shape == (B, L, D)
    assert ada_w.shape == (DEPTH, D, 6 * D) and ffn_w_gate_up.shape == (DEPTH, D, 2 * D_FF)
    assert da_wqkv.shape == na_wqkv.shape == (1, D, 3 * D)
    assert na_rpb.shape == (1, D // HEAD_DIM, 2 * NA_WIN_ROWS - 1, 2 * NA_WIN_COLS - 1)

    cond = jnp.zeros((COND_ROWS, D), F32).at[:B].set(c).at[CTX_MOD_ROW].set(c_ctx)
    mods = _adaln(cond, ada_w, ada_b).reshape(DEPTH, COND_ROWS, 6, D)

    h = x
    hc = ctx.reshape(1, B * L, D)
    rope_tabs = _rope_tables()

    mod_l = mods[0, :B]
    mod_c = mods[0, CTX_MOD_ROW:CTX_MOD_ROW + 1]
    g_mix = norm_mix[0].reshape(1, D)
    g_ffn = norm_ffn[0].reshape(1, D)
    wqkv = da_wqkv[0].astype(BF16)
    q_l, k_l, v_l = _qkv(h, mod_l, g_mix, wqkv, rope_tabs, tm=TM)
    q_c, k_c, v_c = (t.reshape(B, L, D) for t in _qkv(hc, mod_c, g_mix, wqkv, None, tm=TM))
    lam_params = jnp.stack([da_lambda_q1[0], da_lambda_k1[0], da_lambda_q2[0], da_lambda_k2[0]])
    subln = da_subln[0].reshape(1, LANES)
    lam_init = 0.8 - 0.6 * float(np.exp(-0.3 * 0))
    o_l = _diff_attn(lam_params, subln, q_l, [k_c, k_l], [v_c, v_l], block_rows=DA_BLOCK_ROWS,
                     heads=DA_HEADS_PER_STEP, lam_init=lam_init)
    o_c = _diff_attn(lam_params, subln, q_c, [k_c], [v_c], block_rows=(L,), heads=N_LANE_TILES,
                     lam_init=lam_init)
    wo = da_wo[0].astype(BF16)
    wgu = ffn_w_gate_up.astype(BF16)
    wd = ffn_w_down.astype(BF16)
    h = _out_ffn(o_l, h, mod_l, g_ffn, wo, wgu, wd, 0, None, tm=TM)
    hc = _out_ffn(o_c.reshape(1, B * L, D), hc, mod_c, g_ffn, wo, wgu, wd, 0, None, tm=TM)

    mod_l = mods[1, :B]
    mod_c = mods[1, CTX_MOD_ROW:CTX_MOD_ROW + 1]
    g_mix = norm_mix[1].reshape(1, D)
    g_ffn = norm_ffn[1].reshape(1, D)
    wqkv = na_wqkv[0].astype(BF16)
    q_l, k_l, v_l = _qkv(h, mod_l, g_mix, wqkv, None, tm=TM)
    k_c, v_c = (t.reshape(B, L, D) for t in _qkv(hc, mod_c, g_mix, wqkv, None, tm=TM, want_q=False))
    o_l = _nbr_attn(q_l, k_l, v_l, k_c, v_c, _na_pair_table(na_rpb[0]))
    return _out_ffn(o_l, h, mod_l, g_ffn, na_wo[0].astype(BF16), wgu, wd, 1,
                    norm_final.reshape(1, D), tm=TM)
```

```python
import functools

import numpy as np
import jax
import jax.numpy as jnp
from jax import lax
from jax.experimental import pallas as pl
from jax.experimental.pallas import tpu as pltpu

D_MODEL = 1024
BATCH = 16
SEQ = 2048
DEPTH = 2
GRID_W = 64
CTX_LEN = 256
HEAD_DIM = 64
NA_WIN_ROWS = 8
NA_WIN_COLS = 16
D_FF = 2816
ROPE_THETA = 10000.0
EPS = 1e-6
LOG2E = 1.4426950408889634
Q_SCALE = HEAD_DIM ** -0.5 * LOG2E

LANES = 128
N_LANE_TILES = D_MODEL // LANES
COND_ROWS = 24
CTX_MOD_ROW = BATCH
VMEM_LIMIT = 56 * 1024 * 1024

NA_Q_ROWS = 4
NA_K_ROWS = 12
NA_TQ = NA_Q_ROWS * GRID_W
NA_BLOCKS = SEQ // NA_TQ
NA_KBLK = NA_K_ROWS // NA_Q_ROWS
NA_BATCH_PER_STEP = 4
DA_HEADS_PER_STEP = 2
DA_BLOCK_ROWS = ((128,) + (256,) * 7 + (128,),) * DA_HEADS_PER_STEP
F32 = jnp.float32
BF16 = jnp.bfloat16


def _dot(a, b):
    return jnp.dot(a, b, preferred_element_type=F32)


def _dot_nt(a, b):
    return lax.dot_general(a, b, (((1,), (1,)), ((), ())), preferred_element_type=F32)


def _adaln_kernel(cond_ref, w_ref, b_ref, out_ref):
    cnd = cond_ref[...]
    a = cnd * jax.nn.sigmoid(cnd)
    w = w_ref[0]
    a_hi = a.astype(BF16)
    a_lo = (a - a_hi.astype(F32)).astype(BF16)
    w_hi = w.astype(BF16)
    w_lo = (w - w_hi.astype(F32)).astype(BF16)
    acc = _dot(a_hi, w_hi) + _dot(a_lo, w_hi) + _dot(a_hi, w_lo)
    out_ref[0] = acc + b_ref[0]


def _adaln(cond, ada_w, ada_b):
    n_out = ada_w.shape[-1]
    tn = 1024
    return pl.pallas_call(
        _adaln_kernel,
        grid=(DEPTH, n_out // tn),
        in_specs=[
            pl.BlockSpec((COND_ROWS, D_MODEL), lambda i, j: (0, 0)),
            pl.BlockSpec((1, D_MODEL, tn), lambda i, j: (i, 0, j)),
            pl.BlockSpec((1, 1, tn), lambda i, j: (i, 0, j)),
        ],
        out_specs=pl.BlockSpec((1, COND_ROWS, tn), lambda i, j: (i, 0, j)),
        out_shape=jax.ShapeDtypeStruct((DEPTH, COND_ROWS, n_out), F32),
        compiler_params=pltpu.CompilerParams(vmem_limit_bytes=VMEM_LIMIT),
        name="adaln",
    )(cond, ada_w, ada_b.reshape(DEPTH, 1, n_out))


def _rmsnorm_mod(h, g, shift, scale):
    y = h * lax.rsqrt(jnp.mean(h * h, axis=-1, keepdims=True) + EPS) * g
    return y * (1.0 + scale) + shift


def _qkv_kernel(*refs, rope, want_q):
    if rope:
        h_ref, mod_ref, g_ref, w_ref, cos_ref, sa_ref, sb_ref = refs[:7]
        outs = refs[7:]
    else:
        h_ref, mod_ref, g_ref, w_ref = refs[:4]
        outs = refs[4:]
    xn = _rmsnorm_mod(h_ref[0], g_ref[...], mod_ref[0, 0:1, :], mod_ref[0, 1:2, :]).astype(BF16)

    def rotate(t):
        return (t * cos_ref[...] + pltpu.roll(t, LANES - 16, axis=1) * sa_ref[...]
                + pltpu.roll(t, 16, axis=1) * sb_ref[...])

    parts = ("q", "k", "v") if want_q else ("k", "v")
    for name, o_ref in zip(parts, outs):
        col = {"q": 0, "k": D_MODEL, "v": 2 * D_MODEL}[name]
        acc = _dot(xn, w_ref[:, col:col + D_MODEL])
        if name == "q":
            acc = acc * Q_SCALE
        if rope and name != "v":
            for c in range(N_LANE_TILES):
                sl = slice(c * LANES, (c + 1) * LANES)
                o_ref[0, :, sl] = rotate(acc[:, sl]).astype(BF16)
        else:
            o_ref[0] = acc.astype(BF16)


def _qkv(h, mod, g, w, rope_tabs, *, tm, want_q=True):
    G, R, _ = h.shape
    rope = rope_tabs is not None
    row_spec = pl.BlockSpec((1, tm, D_MODEL), lambda b, t: (b, t, 0))
    in_specs = [
        row_spec,
        pl.BlockSpec((1, 6, D_MODEL), lambda b, t: (b, 0, 0)),
        pl.BlockSpec((1, D_MODEL), lambda b, t: (0, 0)),
        pl.BlockSpec((D_MODEL, 3 * D_MODEL), lambda b, t: (0, 0)),
    ]
    args = [h, mod, g, w]
    if rope:
        in_specs += [pl.BlockSpec((tm, LANES), lambda b, t: (t, 0))] * 3
        args += list(rope_tabs)
    n_out = 3 if want_q else 2
    return pl.pallas_call(
        functools.partial(_qkv_kernel, rope=rope, want_q=want_q),
        grid=(G, R // tm),
        in_specs=in_specs,
        out_specs=[row_spec] * n_out,
        out_shape=[jax.ShapeDtypeStruct((G, R, D_MODEL), BF16)] * n_out,
        compiler_params=pltpu.CompilerParams(vmem_limit_bytes=VMEM_LIMIT),
        name="qkv_rope" if rope else "qkv",
    )(*args)


def _rope_tables():
    t = jnp.arange(SEQ)
    row = (t // GRID_W).astype(F32)
    col = (t % GRID_W).astype(F32)
    half = HEAD_DIM // 2
    freqs = 1.0 / (ROPE_THETA ** (jnp.arange(0, half, 2, dtype=F32) / half))
    ar = row[:, None] * freqs
    ac = col[:, None] * freqs
    ang = jnp.concatenate([ar, ar, ac, ac], axis=-1)
    cos, sin = jnp.cos(ang), jnp.sin(ang)
    first = (np.arange(HEAD_DIM) % half) < half // 2
    sa = jnp.where(first, -sin, 0.0)
    sb = jnp.where(first, 0.0, sin)
    rep = LANES // HEAD_DIM
    return tuple(jnp.tile(x, (1, rep)) for x in (cos, sa, sb))


def _stack_halves(q):
    lane = lax.broadcasted_iota(jnp.int32, q.shape, 1)
    zero = jnp.zeros_like(q)
    return jnp.concatenate([jnp.where(lane < HEAD_DIM, q, zero),
                            jnp.where(lane >= HEAD_DIM, q, zero)], axis=0)


def _with_ones(v):
    return jnp.concatenate([v, jnp.ones(v.shape, v.dtype)], axis=1)


def _diff_attn_kernel(*refs, nseg, lam_init, block_rows, heads):
    lam_ref, subln_ref, q_ref = refs[:3]
    k_refs = refs[3:3 + nseg]
    v_refs = refs[3 + nseg:3 + 2 * nseg]
    o_ref = refs[3 + 2 * nseg]

    lp = lam_ref[...]
    lam = (jnp.exp(jnp.sum(lp[0:1] * lp[1:2], axis=-1, keepdims=True))
           - jnp.exp(jnp.sum(lp[2:3] * lp[3:4], axis=-1, keepdims=True)) + lam_init)
    subln = subln_ref[...] * (1.0 - lam_init)

    v_aug = [[_with_ones(v_ref[0, :, head * LANES:(head + 1) * LANES]) for v_ref in v_refs]
             for head in range(heads)]

    def block(head, row0, tq):
        cols = slice(head * LANES, (head + 1) * LANES)
        qq = _stack_halves(q_ref[0, row0:row0 + tq, cols])
        s_parts = [_dot_nt(qq, k_ref[0, :, cols]) for k_ref in k_refs]
        m = s_parts[0].max(axis=-1, keepdims=True)
        for s in s_parts[1:]:
            m = jnp.maximum(m, s.max(axis=-1, keepdims=True))
        acc = None
        for s, v in zip(s_parts, v_aug[head]):
            pv = _dot(jnp.exp2(s - m).astype(BF16), v)
            acc = pv if acc is None else acc + pv
        w = 1.0 / acc[:, LANES:]
        o = acc[:tq, :LANES] * w[:tq] - acc[tq:, :LANES] * (w[tq:] * lam)
        on = o * lax.rsqrt(jnp.mean(o * o, axis=-1, keepdims=True) + EPS) * subln
        o_ref[0, row0:row0 + tq, cols] = on.astype(BF16)

    for head in range(heads):
        row0 = 0
        for tq in block_rows[head]:
            block(head, row0, tq)
            row0 += tq


def _diff_attn(lam_params, subln, q, ks, vs, *, block_rows, heads, lam_init):
    B, N, _ = q.shape
    nseg = len(ks)
    assert len(block_rows) == heads and all(sum(rows) == N for rows in block_rows)

    def head_spec(n):
        return pl.BlockSpec((1, n, heads * LANES), lambda b, h: (b, 0, h))

    kv_specs = [head_spec(k.shape[1]) for k in ks]
    return pl.pallas_call(
        functools.partial(_diff_attn_kernel, nseg=nseg, lam_init=lam_init, block_rows=block_rows,
                          heads=heads),
        grid=(B, N_LANE_TILES // heads),
        in_specs=[
            pl.BlockSpec((4, HEAD_DIM), lambda b, h: (0, 0)),
            pl.BlockSpec((1, LANES), lambda b, h: (0, 0)),
            head_spec(N),
        ] + kv_specs + kv_specs,
        out_specs=head_spec(N),
        out_shape=jax.ShapeDtypeStruct((B, N, D_MODEL), BF16),
        compiler_params=pltpu.CompilerParams(vmem_limit_bytes=VMEM_LIMIT),
        name="diff_attn_%dseg" % nseg,
    )(lam_params, subln, q, *ks, *vs)


def _na_window_start(j):
    return min(max(j - 1, 0), NA_BLOCKS - NA_KBLK)


def _na_bias_type(j):
    return 0 if j == 0 else (2 if j == NA_BLOCKS - 1 else 1)


def _na_block_layout():
    rows = SEQ // GRID_W
    per_block = []
    for j in range(NA_BLOCKS):
        blk = []
        for a in range(NA_Q_ROWS):
            r = NA_Q_ROWS * j + a
            rs = min(max(r - NA_WIN_ROWS // 2, 0), rows - NA_WIN_ROWS)
            row = []
            for u in range(NA_K_ROWS // 2):
                kr = NA_Q_ROWS * _na_window_start(j) + 2 * u
                ok = tuple(rs <= k < rs + NA_WIN_ROWS for k in (kr, kr + 1))
                row.append((kr - r + NA_WIN_ROWS - 1,) + ok)
            blk.append(tuple(row))
        per_block.append(tuple(blk))
    assert all(per_block[j] == per_block[1] for j in range(2, NA_BLOCKS - 1))
    return (per_block[0], per_block[1], per_block[NA_BLOCKS - 1])


NA_LAYOUT = _na_block_layout()
assert all(any(lo or ro for _, lo, ro in row[NA_Q_ROWS // 2:]) for kind in NA_LAYOUT for row in kind)
NA_N_DR = 2 * NA_WIN_ROWS - 1
NA_N_PAIR = NA_N_DR + 1


def _na_pair_table(rpb):
    n_dc = 2 * NA_WIN_COLS - 1
    qc = np.arange(GRID_W)
    cs = np.clip(qc - NA_WIN_COLS // 2, 0, GRID_W - NA_WIN_COLS)
    col_ok = (qc[None, :] >= cs[:, None]) & (qc[None, :] < cs[:, None] + NA_WIN_COLS)
    dc = np.clip(qc[None, :] - qc[:, None] + NA_WIN_COLS - 1, 0, n_dc - 1)
    pick_d = np.zeros((NA_N_DR, NA_N_PAIR, 2), np.float32)
    for i in range(NA_N_PAIR):
        for p in range(2):
            if 0 <= i - 1 + p < NA_N_DR:
                pick_d[i - 1 + p, i, p] = 1.0
    pick_c = np.zeros((2, n_dc, GRID_W, 2, GRID_W), np.float32)
    for p in range(2):
        pick_c[p, dc, qc[:, None], p, qc[None, :]] = 1.0
    lhs = jnp.einsum('hdj,dip->hipj', rpb, jnp.asarray(pick_d), precision=lax.Precision.HIGHEST)
    pairs = jnp.einsum('hic,cx->hix', lhs.reshape(-1, NA_N_PAIR, 2 * n_dc),
                       jnp.asarray(pick_c.reshape(2 * n_dc, GRID_W * LANES)),
                       precision=lax.Precision.HIGHEST).reshape(-1, NA_N_PAIR, GRID_W, LANES)
    valid = pick_d.sum(0)[:, None, :, None] * col_ok[None, :, None, :]
    pairs = jnp.where(jnp.asarray(valid.reshape(NA_N_PAIR, GRID_W, LANES) > 0), pairs * LOG2E, -jnp.inf)
    return pairs.reshape(N_LANE_TILES, 2, NA_N_PAIR, GRID_W, LANES)


def _na_fill_bias(pair_ref, bias_scr):
    lane = lax.broadcasted_iota(jnp.int32, (GRID_W, LANES), 1)
    neg = jnp.full((GRID_W, LANES), -jnp.inf, F32)
    for kind, layout in enumerate(NA_LAYOUT):
        for head in range(2):
            for a, row in enumerate(layout):
                r0 = head * NA_TQ + a * GRID_W
                for u, (d, left_ok, right_ok) in enumerate(row):
                    slab = neg
                    if left_ok or right_ok:
                        assert not left_ok or 0 <= d < NA_N_DR
                        assert not right_ok or 0 <= d + 1 < NA_N_DR
                        slab = pair_ref[0, head, d + 1]
                        if not left_ok:
                            slab = jnp.where(lane < HEAD_DIM, neg, slab)
                        if not right_ok:
                            slab = jnp.where(lane < HEAD_DIM, slab, neg)
                    bias_scr[kind, r0:r0 + GRID_W, u * LANES:(u + 1) * LANES] = slab


def _nbr_attn_kernel(q_ref, k_ref, v_ref, kc_ref, vc_ref, pair_ref, o_ref, bias_scr):
    n_win = NA_K_ROWS * GRID_W
    lane = lax.broadcasted_iota(jnp.int32, (NA_TQ, LANES), 1)

    @pl.when(pl.program_id(1) == 0)
    def _():
        _na_fill_bias(pair_ref, bias_scr)

    for bb in range(q_ref.shape[0]):
        v_aug = _with_ones(v_ref[bb])
        vc_aug = _with_ones(vc_ref[bb])
        for j in range(NA_BLOCKS):
            rows = slice(j * NA_TQ, (j + 1) * NA_TQ)
            win0 = _na_window_start(j) * NA_TQ
            win = slice(win0, win0 + n_win)
            qq = _stack_halves(q_ref[bb, rows, :])
            bias = bias_scr[_na_bias_type(j)]
            t1 = win0 + NA_TQ
            k_a = jnp.concatenate([kc_ref[bb], k_ref[bb, win0:t1, :]], axis=0)
            v_a = jnp.concatenate([vc_aug, v_aug[win0:t1]], axis=0)
            s_a = _dot_nt(qq, k_a)
            s_a = jnp.concatenate([s_a[:, :CTX_LEN], s_a[:, CTX_LEN:] + bias[:, :NA_TQ]], axis=1)
            m_a = s_a.max(axis=-1, keepdims=True)
            acc_a = _dot(jnp.exp2(s_a - m_a).astype(BF16), v_a)
            s_b = _dot_nt(qq, k_ref[bb, t1:win0 + n_win, :]) + bias[:, NA_TQ:]
            m_b = s_b.max(axis=-1, keepdims=True)
            acc_b = _dot(jnp.exp2(s_b - m_b).astype(BF16), v_aug[t1:win0 + n_win])
            m = jnp.maximum(m_a, m_b)
            acc = acc_a * jnp.exp2(m_a - m) + acc_b * jnp.exp2(m_b - m)
            o2 = acc[:, :LANES] / acc[:, LANES:]
            o_ref[bb, rows, :] = jnp.where(lane < HEAD_DIM, o2[:NA_TQ], o2[NA_TQ:]).astype(BF16)


def _nbr_attn(q, k, v, kc, vc, pair_tab):
    B, N, _ = q.shape

    def head_spec(n):
        return pl.BlockSpec((NA_BATCH_PER_STEP, n, LANES), lambda hp, b: (b, 0, hp))

    pair_spec = pl.BlockSpec((1, 2, NA_N_PAIR, GRID_W, LANES), lambda hp, b: (hp, 0, 0, 0, 0))
    return pl.pallas_call(
        _nbr_attn_kernel,
        grid=(N_LANE_TILES, B // NA_BATCH_PER_STEP),
        in_specs=[head_spec(N), head_spec(N), head_spec(N), head_spec(CTX_LEN), head_spec(CTX_LEN),
                  pair_spec],
        out_specs=head_spec(N),
        out_shape=jax.ShapeDtypeStruct(q.shape, BF16),
        scratch_shapes=[pltpu.VMEM((3, 2 * NA_TQ, NA_K_ROWS * GRID_W), F32)],
        compiler_params=pltpu.CompilerParams(vmem_limit_bytes=VMEM_LIMIT,
                                             dimension_semantics=("arbitrary", "arbitrary")),
        name="nbr_attn",
    )(q, k, v, kc, vc, pair_tab)


FFN_CHUNKS = ((0, 1024), (1024, 1024), (2048, 768))

def _out_ffn_kernel(*refs, final):
    o_ref, h_ref, mod_ref, g_ref, wo_ref, wgu_ref, wd_ref = refs[:7]
    if final:
        gfin_ref, out_ref = refs[7:]
    else:
        (out_ref,) = refs[7:]
    h1 = h_ref[0] + mod_ref[0, 2:3, :] * _dot(o_ref[0], wo_ref[...])
    xn = _rmsnorm_mod(h1, g_ref[...], mod_ref[0, 3:4, :], mod_ref[0, 4:5, :]).astype(BF16)
    acc = None
    for start, size in FFN_CHUNKS:
        gate = _dot(xn, wgu_ref[:, start:start + size])
        up = _dot(xn, wgu_ref[:, D_FF + start:D_FF + start + size])
        act = (gate * jax.nn.sigmoid(gate) * up).astype(BF16)
        part = _dot(act, wd_ref[start:start + size, :])
        acc = part if acc is None else acc + part
    h2 = h1 + mod_ref[0, 5:6, :] * acc
    if final:
        h2 = h2 * lax.rsqrt(jnp.mean(h2 * h2, axis=-1, keepdims=True) + EPS) * gfin_ref[...]
    out_ref[0] = h2


def _out_ffn(o, h, mod, g, wo, wgu, wd, layer, gfin, *, tm):
    G, R, _ = h.shape
    final = gfin is not None
    row_spec = pl.BlockSpec((1, tm, D_MODEL), lambda b, t: (b, t, 0))

    def resident(shape, index):
        return pl.BlockSpec(shape, lambda b, t: index, pipeline_mode=pl.Buffered(1))

    in_specs = [
        row_spec,
        row_spec,
        pl.BlockSpec((1, 6, D_MODEL), lambda b, t: (b, 0, 0)),
        pl.BlockSpec((1, D_MODEL), lambda b, t: (0, 0)),
        resident((D_MODEL, D_MODEL), (0, 0)),
        resident((None, D_MODEL, 2 * D_FF), (layer, 0, 0)),
        resident((None, D_FF, D_MODEL), (layer, 0, 0)),
    ]
    args = [o, h, mod, g, wo, wgu, wd]
    if final:
        in_specs.append(pl.BlockSpec((1, D_MODEL), lambda b, t: (0, 0)))
        args.append(gfin)
    return pl.pallas_call(
        functools.partial(_out_ffn_kernel, final=final),
        grid=(G, R // tm),
        in_specs=in_specs,
        out_specs=row_spec,
        out_shape=jax.ShapeDtypeStruct(h.shape, F32),
        compiler_params=pltpu.CompilerParams(vmem_limit_bytes=VMEM_LIMIT),
        name="out_ffn_final" if final else "out_ffn",
    )(*args)


def kernel(x, c, ctx, c_ctx, ada_w, ada_b, norm_mix, norm_ffn, da_wqkv, da_lambda_q1, da_lambda_k1,
           da_lambda_q2, da_lambda_k2, da_subln, da_wo, na_wqkv, na_rpb, na_wo, ffn_w_gate_up,
           ffn_w_down, norm_final):
    B, N, D = x.shape
    L = ctx.shape[1]
    TM = 1024
    assert (B, N, D, L) == (BATCH, SEQ, D_MODEL, CTX_LEN) and ctx.shape == (B, L, D)
    assert ada_w.shape == (DEPTH, D, 6 * D) and ffn_w_gate_up.shape == (DEPTH, D, 2 * D_FF)
    assert da_wqkv.shape == na_wqkv.shape == (1, D, 3 * D)
    assert na_rpb.shape == (1, D // HEAD_DIM, 2 * NA_WIN_ROWS - 1, 2 * NA_WIN_COLS - 1)

    cond = jnp.zeros((COND_ROWS, D), F32).at[:B].set(c).at[CTX_MOD_ROW].set(c_ctx)
    mods = _adaln(cond, ada_w, ada_b).reshape(DEPTH, COND_ROWS, 6, D)

    h = x
    hc = ctx.reshape(1, B * L, D)
    rope_tabs = _rope_tables()

    mod_l = mods[0, :B]
    mod_c = mods[0, CTX_MOD_ROW:CTX_MOD_ROW + 1]
    g_mix = norm_mix[0].reshape(1, D)
    g_ffn = norm_ffn[0].reshape(1, D)
    wqkv = da_wqkv[0].astype(BF16)
    q_l, k_l, v_l = _qkv(h, mod_l, g_mix, wqkv, rope_tabs, tm=TM)
    q_c, k_c, v_c = (t.reshape(B, L, D) for t in _qkv(hc, mod_c, g_mix, wqkv, None, tm=TM))
    lam_params = jnp.stack([da_lambda_q1[0], da_lambda_k1[0], da_lambda_q2[0], da_lambda_k2[0]])
    subln = da_subln[0].reshape(1, LANES)
    lam_init = 0.8 - 0.6 * float(np.exp(-0.3 * 0))
    o_l = _diff_attn(lam_params, subln, q_l, [k_c, k_l], [v_c, v_l], block_rows=DA_BLOCK_ROWS,
                     heads=DA_HEADS_PER_STEP, lam_init=lam_init)
    o_c = _diff_attn(lam_params, subln, q_c, [k_c], [v_c], block_rows=((L,),) * N_LANE_TILES,
                     heads=N_LANE_TILES,
                     lam_init=lam_init)
    wo = da_wo[0].astype(BF16)
    wgu = ffn_w_gate_up.astype(BF16)
    wd = ffn_w_down.astype(BF16)
    h = _out_ffn(o_l, h, mod_l, g_ffn, wo, wgu, wd, 0, None, tm=TM)
    hc = _out_ffn(o_c.reshape(1, B * L, D), hc, mod_c, g_ffn, wo, wgu, wd, 0, None, tm=TM)

    mod_l = mods[1, :B]
    mod_c = mods[1, CTX_MOD_ROW:CTX_MOD_ROW + 1]
    g_mix = norm_mix[1].reshape(1, D)
    g_ffn = norm_ffn[1].reshape(1, D)
    wqkv = na_wqkv[0].astype(BF16)
    q_l, k_l, v_l = _qkv(h, mod_l, g_mix, wqkv, None, tm=TM)
    k_c, v_c = (t.reshape(B, L, D) for t in _qkv(hc, mod_c, g_mix, wqkv, None, tm=TM, want_q=False))
    o_l = _nbr_attn(q_l, k_l, v_l, k_c, v_c, _na_pair_table(na_rpb[0]))
    return _out_ffn(o_l, h, mod_l, g_ffn, na_wo[0].astype(BF16), wgu, wd, 1,
                    norm_final.reshape(1, D), tm=TM)
```

```python
import functools

import numpy as np
import jax
import jax.numpy as jnp
from jax import lax
from jax.experimental import pallas as pl
from jax.experimental.pallas import tpu as pltpu

D_MODEL = 1024
BATCH = 16
SEQ = 2048
DEPTH = 2
GRID_W = 64
CTX_LEN = 256
HEAD_DIM = 64
NA_WIN_ROWS = 8
NA_WIN_COLS = 16
D_FF = 2816
ROPE_THETA = 10000.0
EPS = 1e-6
LOG2E = 1.4426950408889634
Q_SCALE = HEAD_DIM ** -0.5 * LOG2E

LANES = 128
BF16_ROWS = 16
N_LANE_TILES = D_MODEL // LANES
COND_ROWS = 24
CTX_MOD_ROW = BATCH
VMEM_LIMIT = 56 * 1024 * 1024

NA_Q_ROWS = 4
NA_K_ROWS = 12
NA_TQ = NA_Q_ROWS * GRID_W
NA_BLOCKS = SEQ // NA_TQ
NA_KBLK = NA_K_ROWS // NA_Q_ROWS
NA_BATCH_PER_STEP = 4
DA_HEADS_PER_STEP = 2
DA_BLOCK_ROWS = ((128,) + (256,) * 7 + (128,),) * DA_HEADS_PER_STEP
F32 = jnp.float32
BF16 = jnp.bfloat16


def _dot(a, b):
    return jnp.dot(a, b, preferred_element_type=F32)


def _dot_nt(a, b):
    return lax.dot_general(a, b, (((1,), (1,)), ((), ())), preferred_element_type=F32)


def _adaln_kernel(cond_ref, w_ref, b_ref, out_ref):
    cnd = cond_ref[...]
    a = cnd * jax.nn.sigmoid(cnd)
    w = w_ref[0]
    a_hi = a.astype(BF16)
    a_lo = (a - a_hi.astype(F32)).astype(BF16)
    w_hi = w.astype(BF16)
    w_lo = (w - w_hi.astype(F32)).astype(BF16)
    acc = _dot(a_hi, w_hi) + _dot(a_lo, w_hi) + _dot(a_hi, w_lo)
    out_ref[0] = acc + b_ref[0]


def _adaln(cond, ada_w, ada_b):
    n_out = ada_w.shape[-1]
    tn = 1024
    return pl.pallas_call(
        _adaln_kernel,
        grid=(DEPTH, n_out // tn),
        in_specs=[
            pl.BlockSpec((COND_ROWS, D_MODEL), lambda i, j: (0, 0)),
            pl.BlockSpec((1, D_MODEL, tn), lambda i, j: (i, 0, j)),
            pl.BlockSpec((1, 1, tn), lambda i, j: (i, 0, j)),
        ],
        out_specs=pl.BlockSpec((1, COND_ROWS, tn), lambda i, j: (i, 0, j)),
        out_shape=jax.ShapeDtypeStruct((DEPTH, COND_ROWS, n_out), F32),
        compiler_params=pltpu.CompilerParams(vmem_limit_bytes=VMEM_LIMIT),
        name="adaln",
    )(cond, ada_w, ada_b.reshape(DEPTH, 1, n_out))


def _rmsnorm_mod(h, g, shift, scale):
    y = h * lax.rsqrt(jnp.mean(h * h, axis=-1, keepdims=True) + EPS) * g
    return y * (1.0 + scale) + shift


def _qkv_kernel(*refs, rope, want_q):
    if rope:
        h_ref, mod_ref, g_ref, w_ref, cos_ref, sa_ref, sb_ref = refs[:7]
        outs = refs[7:]
    else:
        h_ref, mod_ref, g_ref, w_ref = refs[:4]
        outs = refs[4:]
    xn = _rmsnorm_mod(h_ref[0], g_ref[...], mod_ref[0, 0:1, :], mod_ref[0, 1:2, :]).astype(BF16)

    def rotate(t):
        return (t * cos_ref[...] + pltpu.roll(t, LANES - 16, axis=1) * sa_ref[...]
                + pltpu.roll(t, 16, axis=1) * sb_ref[...])

    parts = ("q", "k", "v") if want_q else ("k", "v")
    for name, o_ref in zip(parts, outs):
        col = {"q": 0, "k": D_MODEL, "v": 2 * D_MODEL}[name]
        acc = _dot(xn, w_ref[:, col:col + D_MODEL])
        if name == "q":
            acc = acc * Q_SCALE
        if rope and name != "v":
            for c in range(N_LANE_TILES):
                sl = slice(c * LANES, (c + 1) * LANES)
                o_ref[0, :, sl] = rotate(acc[:, sl]).astype(BF16)
        else:
            o_ref[0] = acc.astype(BF16)


def _qkv(h, mod, g, w, rope_tabs, *, tm, want_q=True):
    G, R, _ = h.shape
    rope = rope_tabs is not None
    row_spec = pl.BlockSpec((1, tm, D_MODEL), lambda b, t: (b, t, 0))
    in_specs = [
        row_spec,
        pl.BlockSpec((1, 6, D_MODEL), lambda b, t: (b, 0, 0)),
        pl.BlockSpec((1, D_MODEL), lambda b, t: (0, 0)),
        pl.BlockSpec((D_MODEL, 3 * D_MODEL), lambda b, t: (0, 0)),
    ]
    args = [h, mod, g, w]
    if rope:
        in_specs += [pl.BlockSpec((tm, LANES), lambda b, t: (t, 0))] * 3
        args += list(rope_tabs)
    n_out = 3 if want_q else 2
    return pl.pallas_call(
        functools.partial(_qkv_kernel, rope=rope, want_q=want_q),
        grid=(G, R // tm),
        in_specs=in_specs,
        out_specs=[row_spec] * n_out,
        out_shape=[jax.ShapeDtypeStruct((G, R, D_MODEL), BF16)] * n_out,
        compiler_params=pltpu.CompilerParams(vmem_limit_bytes=VMEM_LIMIT),
        name="qkv_rope" if rope else "qkv",
    )(*args)


def _rope_tables():
    t = jnp.arange(SEQ)
    row = (t // GRID_W).astype(F32)
    col = (t % GRID_W).astype(F32)
    half = HEAD_DIM // 2
    freqs = 1.0 / (ROPE_THETA ** (jnp.arange(0, half, 2, dtype=F32) / half))
    ar = row[:, None] * freqs
    ac = col[:, None] * freqs
    ang = jnp.concatenate([ar, ar, ac, ac], axis=-1)
    cos, sin = jnp.cos(ang), jnp.sin(ang)
    first = (np.arange(HEAD_DIM) % half) < half // 2
    sa = jnp.where(first, -sin, 0.0)
    sb = jnp.where(first, 0.0, sin)
    rep = LANES // HEAD_DIM
    return tuple(jnp.tile(x, (1, rep)) for x in (cos, sa, sb))


def _stack_halves(q):
    lane = lax.broadcasted_iota(jnp.int32, q.shape, 1)
    zero = jnp.zeros_like(q)
    return jnp.concatenate([jnp.where(lane < HEAD_DIM, q, zero),
                            jnp.where(lane >= HEAD_DIM, q, zero)], axis=0)


def _with_ones(v):
    return jnp.concatenate([v, jnp.ones(v.shape, v.dtype)], axis=1)


def _diff_attn_kernel(*refs, nseg, n_cast, lam_init, block_rows, heads):
    lam_ref, subln_ref, q_ref = refs[:3]
    k_refs = refs[3:3 + nseg]
    v_refs = refs[3 + nseg:3 + 2 * nseg]
    n_in = 3 + 2 * nseg + n_cast
    o_ref = refs[n_in]
    for src_ref, dst_ref in zip(refs[n_in - n_cast:n_in], refs[n_in + 1:]):
        dst_ref[...] = src_ref[...].astype(BF16)

    lp = lam_ref[...]
    lam = (jnp.exp(jnp.sum(lp[0:1] * lp[1:2], axis=-1, keepdims=True))
           - jnp.exp(jnp.sum(lp[2:3] * lp[3:4], axis=-1, keepdims=True)) + lam_init)
    subln = subln_ref[...] * (1.0 - lam_init)

    v_aug = [[_with_ones(v_ref[0, :, head * LANES:(head + 1) * LANES]) for v_ref in v_refs]
             for head in range(heads)]

    def block(head, row0, tq):
        cols = slice(head * LANES, (head + 1) * LANES)
        qq = _stack_halves(q_ref[0, row0:row0 + tq, cols])
        s_parts = [_dot_nt(qq, k_ref[0, :, cols]) for k_ref in k_refs]
        m = s_parts[0].max(axis=-1, keepdims=True)
        for s in s_parts[1:]:
            m = jnp.maximum(m, s.max(axis=-1, keepdims=True))
        acc = None
        for s, v in zip(s_parts, v_aug[head]):
            pv = _dot(jnp.exp2(s - m).astype(BF16), v)
            acc = pv if acc is None else acc + pv
        w = 1.0 / acc[:, LANES:]
        o = acc[:tq, :LANES] * w[:tq] - acc[tq:, :LANES] * (w[tq:] * lam)
        on = o * lax.rsqrt(jnp.mean(o * o, axis=-1, keepdims=True) + EPS) * subln
        o_ref[0, row0:row0 + tq, cols] = on.astype(BF16)

    for head in range(heads):
        row0 = 0
        for tq in block_rows[head]:
            block(head, row0, tq)
            row0 += tq


def _diff_attn(lam_params, subln, q, ks, vs, *, block_rows, heads, lam_init, to_cast=()):
    B, N, _ = q.shape
    nseg = len(ks)
    assert len(block_rows) == heads and all(sum(rows) == N for rows in block_rows)
    n_h = N_LANE_TILES // heads
    steps = B * n_h

    def head_spec(n):
        return pl.BlockSpec((1, n, heads * LANES), lambda b, h: (b, 0, h))

    def cast_spec(w):
        rows, cols = w.shape
        share = 1 if rows % (steps * BF16_ROWS) == 0 else 2
        assert rows % (steps // share * BF16_ROWS) == 0
        return pl.BlockSpec((rows * share // steps, cols), lambda b, h: ((b * n_h + h) // share, 0))

    kv_specs = [head_spec(k.shape[1]) for k in ks]
    cast_specs = [cast_spec(w) for w in to_cast]
    out = pl.pallas_call(
        functools.partial(_diff_attn_kernel, nseg=nseg, n_cast=len(to_cast), lam_init=lam_init,
                          block_rows=block_rows, heads=heads),
        grid=(B, n_h),
        in_specs=[
            pl.BlockSpec((4, HEAD_DIM), lambda b, h: (0, 0)),
            pl.BlockSpec((1, LANES), lambda b, h: (0, 0)),
            head_spec(N),
        ] + kv_specs + kv_specs + cast_specs,
        out_specs=[head_spec(N)] + cast_specs,
        out_shape=[jax.ShapeDtypeStruct((B, N, D_MODEL), BF16)]
        + [jax.ShapeDtypeStruct(w.shape, BF16) for w in to_cast],
        compiler_params=pltpu.CompilerParams(vmem_limit_bytes=VMEM_LIMIT),
        name="diff_attn_%dseg" % nseg,
    )(lam_params, subln, q, *ks, *vs, *to_cast)
    return out[0], out[1:]


def _na_window_start(j):
    return min(max(j - 1, 0), NA_BLOCKS - NA_KBLK)


def _na_bias_type(j):
    return 0 if j == 0 else (2 if j == NA_BLOCKS - 1 else 1)


def _na_block_layout():
    rows = SEQ // GRID_W
    per_block = []
    for j in range(NA_BLOCKS):
        blk = []
        for a in range(NA_Q_ROWS):
            r = NA_Q_ROWS * j + a
            rs = min(max(r - NA_WIN_ROWS // 2, 0), rows - NA_WIN_ROWS)
            row = []
            for u in range(NA_K_ROWS // 2):
                kr = NA_Q_ROWS * _na_window_start(j) + 2 * u
                ok = tuple(rs <= k < rs + NA_WIN_ROWS for k in (kr, kr + 1))
                row.append((kr - r + NA_WIN_ROWS - 1,) + ok)
            blk.append(tuple(row))
        per_block.append(tuple(blk))
    assert all(per_block[j] == per_block[1] for j in range(2, NA_BLOCKS - 1))
    return (per_block[0], per_block[1], per_block[NA_BLOCKS - 1])


NA_LAYOUT = _na_block_layout()
assert all(any(lo or ro for _, lo, ro in row[NA_Q_ROWS // 2:]) for kind in NA_LAYOUT for row in kind)
NA_N_DR = 2 * NA_WIN_ROWS - 1
NA_N_PAIR = NA_N_DR + 1


def _na_pair_table(rpb):
    n_dc = 2 * NA_WIN_COLS - 1
    qc = np.arange(GRID_W)
    cs = np.clip(qc - NA_WIN_COLS // 2, 0, GRID_W - NA_WIN_COLS)
    col_ok = (qc[None, :] >= cs[:, None]) & (qc[None, :] < cs[:, None] + NA_WIN_COLS)
    dc = np.clip(qc[None, :] - qc[:, None] + NA_WIN_COLS - 1, 0, n_dc - 1)
    pick_d = np.zeros((NA_N_DR, NA_N_PAIR, 2), np.float32)
    for i in range(NA_N_PAIR):
        for p in range(2):
            if 0 <= i - 1 + p < NA_N_DR:
                pick_d[i - 1 + p, i, p] = 1.0
    pick_c = np.zeros((2, n_dc, GRID_W, 2, GRID_W), np.float32)
    for p in range(2):
        pick_c[p, dc, qc[:, None], p, qc[None, :]] = 1.0
    lhs = jnp.einsum('hdj,dip->hipj', rpb, jnp.asarray(pick_d), precision=lax.Precision.HIGHEST)
    pairs = jnp.einsum('hic,cx->hix', lhs.reshape(-1, NA_N_PAIR, 2 * n_dc),
                       jnp.asarray(pick_c.reshape(2 * n_dc, GRID_W * LANES)),
                       precision=lax.Precision.HIGHEST).reshape(-1, NA_N_PAIR, GRID_W, LANES)
    valid = pick_d.sum(0)[:, None, :, None] * col_ok[None, :, None, :]
    pairs = jnp.where(jnp.asarray(valid.reshape(NA_N_PAIR, GRID_W, LANES) > 0), pairs * LOG2E, -jnp.inf)
    return pairs.reshape(N_LANE_TILES, 2, NA_N_PAIR, GRID_W, LANES)


def _na_fill_bias(pair_ref, bias_scr):
    lane = lax.broadcasted_iota(jnp.int32, (GRID_W, LANES), 1)
    neg = jnp.full((GRID_W, LANES), -jnp.inf, F32)
    for kind, layout in enumerate(NA_LAYOUT):
        for head in range(2):
            for a, row in enumerate(layout):
                r0 = head * NA_TQ + a * GRID_W
                for u, (d, left_ok, right_ok) in enumerate(row):
                    slab = neg
                    if left_ok or right_ok:
                        assert not left_ok or 0 <= d < NA_N_DR
                        assert not right_ok or 0 <= d + 1 < NA_N_DR
                        slab = pair_ref[0, head, d + 1]
                        if not left_ok:
                            slab = jnp.where(lane < HEAD_DIM, neg, slab)
                        if not right_ok:
                            slab = jnp.where(lane < HEAD_DIM, slab, neg)
                    bias_scr[kind, r0:r0 + GRID_W, u * LANES:(u + 1) * LANES] = slab


def _nbr_attn_kernel(q_ref, k_ref, v_ref, kc_ref, vc_ref, pair_ref, o_ref, bias_scr):
    n_win = NA_K_ROWS * GRID_W
    lane = lax.broadcasted_iota(jnp.int32, (NA_TQ, LANES), 1)

    @pl.when(pl.program_id(1) == 0)
    def _():
        _na_fill_bias(pair_ref, bias_scr)

    for bb in range(q_ref.shape[0]):
        v_aug = _with_ones(v_ref[bb])
        vc_aug = _with_ones(vc_ref[bb])
        for j in range(NA_BLOCKS):
            rows = slice(j * NA_TQ, (j + 1) * NA_TQ)
            win0 = _na_window_start(j) * NA_TQ
            win = slice(win0, win0 + n_win)
            qq = _stack_halves(q_ref[bb, rows, :])
            bias = bias_scr[_na_bias_type(j)]
            t1 = win0 + NA_TQ
            k_a = jnp.concatenate([kc_ref[bb], k_ref[bb, win0:t1, :]], axis=0)
            v_a = jnp.concatenate([vc_aug, v_aug[win0:t1]], axis=0)
            s_a = _dot_nt(qq, k_a)
            s_a = jnp.concatenate([s_a[:, :CTX_LEN], s_a[:, CTX_LEN:] + bias[:, :NA_TQ]], axis=1)
            m_a = s_a.max(axis=-1, keepdims=True)
            acc_a = _dot(jnp.exp2(s_a - m_a).astype(BF16), v_a)
            s_b = _dot_nt(qq, k_ref[bb, t1:win0 + n_win, :]) + bias[:, NA_TQ:]
            m_b = s_b.max(axis=-1, keepdims=True)
            acc_b = _dot(jnp.exp2(s_b - m_b).astype(BF16), v_aug[t1:win0 + n_win])
            m = jnp.maximum(m_a, m_b)
            acc = acc_a * jnp.exp2(m_a - m) + acc_b * jnp.exp2(m_b - m)
            o2 = acc[:, :LANES] / acc[:, LANES:]
            o_ref[bb, rows, :] = jnp.where(lane < HEAD_DIM, o2[:NA_TQ], o2[NA_TQ:]).astype(BF16)


def _nbr_attn(q, k, v, kc, vc, pair_tab):
    B, N, _ = q.shape

    def head_spec(n):
        return pl.BlockSpec((NA_BATCH_PER_STEP, n, LANES), lambda hp, b: (b, 0, hp))

    pair_spec = pl.BlockSpec((1, 2, NA_N_PAIR, GRID_W, LANES), lambda hp, b: (hp, 0, 0, 0, 0))
    return pl.pallas_call(
        _nbr_attn_kernel,
        grid=(N_LANE_TILES, B // NA_BATCH_PER_STEP),
        in_specs=[head_spec(N), head_spec(N), head_spec(N), head_spec(CTX_LEN), head_spec(CTX_LEN),
                  pair_spec],
        out_specs=head_spec(N),
        out_shape=jax.ShapeDtypeStruct(q.shape, BF16),
        scratch_shapes=[pltpu.VMEM((3, 2 * NA_TQ, NA_K_ROWS * GRID_W), F32)],
        compiler_params=pltpu.CompilerParams(vmem_limit_bytes=VMEM_LIMIT,
                                             dimension_semantics=("arbitrary", "arbitrary")),
        name="nbr_attn",
    )(q, k, v, kc, vc, pair_tab)


FFN_CHUNKS = ((0, 1024), (1024, 1024), (2048, 768))

def _out_ffn_kernel(*refs, final):
    o_ref, h_ref, mod_ref, g_ref, wo_ref, wgu_ref, wd_ref = refs[:7]
    if final:
        gfin_ref, out_ref = refs[7:]
    else:
        (out_ref,) = refs[7:]
    h1 = h_ref[0] + mod_ref[0, 2:3, :] * _dot(o_ref[0], wo_ref[...])
    xn = _rmsnorm_mod(h1, g_ref[...], mod_ref[0, 3:4, :], mod_ref[0, 4:5, :]).astype(BF16)
    acc = None
    for start, size in FFN_CHUNKS:
        gate = _dot(xn, wgu_ref[:, start:start + size])
        up = _dot(xn, wgu_ref[:, D_FF + start:D_FF + start + size])
        act = (gate * jax.nn.sigmoid(gate) * up).astype(BF16)
        part = _dot(act, wd_ref[start:start + size, :])
        acc = part if acc is None else acc + part
    h2 = h1 + mod_ref[0, 5:6, :] * acc
    if final:
        h2 = h2 * lax.rsqrt(jnp.mean(h2 * h2, axis=-1, keepdims=True) + EPS) * gfin_ref[...]
    out_ref[0] = h2


def _out_ffn(o, h, mod, g, wo, wgu, wd, layer, gfin, *, tm):
    G, R, _ = h.shape
    final = gfin is not None
    row_spec = pl.BlockSpec((1, tm, D_MODEL), lambda b, t: (b, t, 0))

    def resident(shape, index):
        return pl.BlockSpec(shape, lambda b, t: index, pipeline_mode=pl.Buffered(1))

    in_specs = [
        row_spec,
        row_spec,
        pl.BlockSpec((1, 6, D_MODEL), lambda b, t: (b, 0, 0)),
        pl.BlockSpec((1, D_MODEL), lambda b, t: (0, 0)),
        resident((D_MODEL, D_MODEL), (0, 0)),
        resident((None, D_MODEL, 2 * D_FF), (layer, 0, 0)),
        resident((None, D_FF, D_MODEL), (layer, 0, 0)),
    ]
    args = [o, h, mod, g, wo, wgu, wd]
    if final:
        in_specs.append(pl.BlockSpec((1, D_MODEL), lambda b, t: (0, 0)))
        args.append(gfin)
    return pl.pallas_call(
        functools.partial(_out_ffn_kernel, final=final),
        grid=(G, R // tm),
        in_specs=in_specs,
        out_specs=row_spec,
        out_shape=jax.ShapeDtypeStruct(h.shape, F32),
        compiler_params=pltpu.CompilerParams(vmem_limit_bytes=VMEM_LIMIT),
        name="out_ffn_final" if final else "out_ffn",
    )(*args)


def kernel(x, c, ctx, c_ctx, ada_w, ada_b, norm_mix, norm_ffn, da_wqkv, da_lambda_q1, da_lambda_k1,
           da_lambda_q2, da_lambda_k2, da_subln, da_wo, na_wqkv, na_rpb, na_wo, ffn_w_gate_up,
           ffn_w_down, norm_final):
    B, N, D = x.shape
    L = ctx.shape[1]
    TM = 1024
    assert (B, N, D, L) == (BATCH, SEQ, D_MODEL, CTX_LEN) and ctx.shape == (B, L, D)
    assert ada_w.shape == (DEPTH, D, 6 * D) and ffn_w_gate_up.shape == (DEPTH, D, 2 * D_FF)
    assert da_wqkv.shape == na_wqkv.shape == (1, D, 3 * D)
    assert na_rpb.shape == (1, D // HEAD_DIM, 2 * NA_WIN_ROWS - 1, 2 * NA_WIN_COLS - 1)

    cond = jnp.zeros((COND_ROWS, D), F32).at[:B].set(c).at[CTX_MOD_ROW].set(c_ctx)
    mods = _adaln(cond, ada_w, ada_b).reshape(DEPTH, COND_ROWS, 6, D)

    h = x
    hc = ctx.reshape(1, B * L, D)
    rope_tabs = _rope_tables()

    mod_l = mods[0, :B]
    mod_c = mods[0, CTX_MOD_ROW:CTX_MOD_ROW + 1]
    g_mix = norm_mix[0].reshape(1, D)
    g_ffn = norm_ffn[0].reshape(1, D)
    wqkv = da_wqkv[0].astype(BF16)
    q_l, k_l, v_l = _qkv(h, mod_l, g_mix, wqkv, rope_tabs, tm=TM)
    q_c, k_c, v_c = (t.reshape(B, L, D) for t in _qkv(hc, mod_c, g_mix, wqkv, None, tm=TM))
    lam_params = jnp.stack([da_lambda_q1[0], da_lambda_k1[0], da_lambda_q2[0], da_lambda_k2[0]])
    subln = da_subln[0].reshape(1, LANES)
    lam_init = 0.8 - 0.6 * float(np.exp(-0.3 * 0))
    later_weights = [ffn_w_gate_up.reshape(DEPTH * D, 2 * D_FF), ffn_w_down.reshape(DEPTH * D_FF, D),
                     da_wo[0], na_wqkv[0], na_wo[0]]
    o_l, (wgu, wd, wo, na_wqkv_bf, na_wo_bf) = _diff_attn(
        lam_params, subln, q_l, [k_c, k_l], [v_c, v_l], block_rows=DA_BLOCK_ROWS,
        heads=DA_HEADS_PER_STEP, lam_init=lam_init, to_cast=later_weights)
    wgu = wgu.reshape(DEPTH, D, 2 * D_FF)
    wd = wd.reshape(DEPTH, D_FF, D)
    o_c, _ = _diff_attn(lam_params, subln, q_c, [k_c], [v_c], block_rows=((L,),) * N_LANE_TILES,
                        heads=N_LANE_TILES, lam_init=lam_init)
    h = _out_ffn(o_l, h, mod_l, g_ffn, wo, wgu, wd, 0, None, tm=TM)
    hc = _out_ffn(o_c.reshape(1, B * L, D), hc, mod_c, g_ffn, wo, wgu, wd, 0, None, tm=TM)

    mod_l = mods[1, :B]
    mod_c = mods[1, CTX_MOD_ROW:CTX_MOD_ROW + 1]
    g_mix = norm_mix[1].reshape(1, D)
    g_ffn = norm_ffn[1].reshape(1, D)
    q_l, k_l, v_l = _qkv(h, mod_l, g_mix, na_wqkv_bf, None, tm=TM)
    k_c, v_c = (t.reshape(B, L, D)
                for t in _qkv(hc, mod_c, g_mix, na_wqkv_bf, None, tm=TM, want_q=False))
    o_l = _nbr_attn(q_l, k_l, v_l, k_c, v_c, _na_pair_table(na_rpb[0]))
    return _out_ffn(o_l, h, mod_l, g_ffn, na_wo_bf, wgu, wd, 1, norm_final.reshape(1, D), tm=TM)
```

```python
import functools

import numpy as np
import jax
import jax.numpy as jnp
from jax import lax
from jax.experimental import pallas as pl
from jax.experimental.pallas import tpu as pltpu

D_MODEL = 1024
BATCH = 16
SEQ = 2048
DEPTH = 2
GRID_W = 64
CTX_LEN = 256
HEAD_DIM = 64
NA_WIN_ROWS = 8
NA_WIN_COLS = 16
D_FF = 2816
ROPE_THETA = 10000.0
EPS = 1e-6
LOG2E = 1.4426950408889634
Q_SCALE = HEAD_DIM ** -0.5 * LOG2E

LANES = 128
BF16_ROWS = 16
N_LANE_TILES = D_MODEL // LANES
COND_ROWS = 32
CTX_MOD_ROW = BATCH
VMEM_LIMIT = 56 * 1024 * 1024

NA_Q_ROWS = 4
NA_K_ROWS = 12
NA_TQ = NA_Q_ROWS * GRID_W
NA_BLOCKS = SEQ // NA_TQ
NA_KBLK = NA_K_ROWS // NA_Q_ROWS
NA_BATCH_PER_STEP = 8
DA_HEADS_PER_STEP = 4
DA_BLOCK_ROWS = ((128,) + (256,) * 7 + (128,),) * DA_HEADS_PER_STEP
F32 = jnp.float32
BF16 = jnp.bfloat16


def _dot(a, b):
    return jnp.dot(a, b, preferred_element_type=F32)


def _dot_nt(a, b):
    return lax.dot_general(a, b, (((1,), (1,)), ((), ())), preferred_element_type=F32)


def _adaln_kernel(cond_ref, w_ref, b_ref, out_ref):
    cnd = cond_ref[...]
    a = cnd * jax.nn.sigmoid(cnd)
    w = w_ref[0]
    a_hi = a.astype(BF16)
    a_lo = (a - a_hi.astype(F32)).astype(BF16)
    w_hi = w.astype(BF16)
    w_lo = (w - w_hi.astype(F32)).astype(BF16)
    p_hi = _dot(jnp.concatenate([a_hi, a_lo], axis=0), w_hi)
    acc = p_hi[:COND_ROWS] + p_hi[COND_ROWS:] + _dot(a_hi, w_lo)
    out_ref[0] = acc + b_ref[0]


def _adaln(cond, ada_w, ada_b):
    n_out = ada_w.shape[-1]
    tn = 1024
    return pl.pallas_call(
        _adaln_kernel,
        grid=(DEPTH, n_out // tn),
        in_specs=[
            pl.BlockSpec((COND_ROWS, D_MODEL), lambda i, j: (0, 0)),
            pl.BlockSpec((1, D_MODEL, tn), lambda i, j: (i, 0, j)),
            pl.BlockSpec((1, 1, tn), lambda i, j: (i, 0, j)),
        ],
        out_specs=pl.BlockSpec((1, COND_ROWS, tn), lambda i, j: (i, 0, j)),
        out_shape=jax.ShapeDtypeStruct((DEPTH, COND_ROWS, n_out), F32),
        compiler_params=pltpu.CompilerParams(vmem_limit_bytes=VMEM_LIMIT),
        name="adaln",
    )(cond, ada_w, ada_b.reshape(DEPTH, 1, n_out))


def _rmsnorm_mod(h, g, shift, scale):
    y = h * lax.rsqrt(jnp.mean(h * h, axis=-1, keepdims=True) + EPS) * g
    return y * (1.0 + scale) + shift


def _qkv_kernel(*refs, rope, want_q):
    if rope:
        h_ref, mod_ref, g_ref, w_ref, cos_ref, sa_ref, sb_ref = refs[:7]
        outs = refs[7:]
    else:
        h_ref, mod_ref, g_ref, w_ref = refs[:4]
        outs = refs[4:]
    xn = _rmsnorm_mod(h_ref[0], g_ref[...], mod_ref[0, 0:1, :], mod_ref[0, 1:2, :]).astype(BF16)

    def rotate(t):
        return (t * cos_ref[...] + pltpu.roll(t, LANES - 16, axis=1) * sa_ref[...]
                + pltpu.roll(t, 16, axis=1) * sb_ref[...])

    parts = ("q", "k", "v") if want_q else ("k", "v")
    for name, o_ref in zip(parts, outs):
        col = {"q": 0, "k": D_MODEL, "v": 2 * D_MODEL}[name]
        acc = _dot(xn, w_ref[:, col:col + D_MODEL])
        if name == "q":
            acc = acc * Q_SCALE
        if rope and name != "v":
            for c in range(N_LANE_TILES):
                sl = slice(c * LANES, (c + 1) * LANES)
                o_ref[0, :, sl] = rotate(acc[:, sl]).astype(BF16)
        else:
            o_ref[0] = acc.astype(BF16)


def _qkv(h, mod, g, w, rope_tabs, *, tm, want_q=True):
    G, R, _ = h.shape
    rope = rope_tabs is not None
    row_spec = pl.BlockSpec((1, tm, D_MODEL), lambda b, t: (b, t, 0))
    in_specs = [
        row_spec,
        pl.BlockSpec((1, 6, D_MODEL), lambda b, t: (b, 0, 0)),
        pl.BlockSpec((1, D_MODEL), lambda b, t: (0, 0)),
        pl.BlockSpec((D_MODEL, 3 * D_MODEL), lambda b, t: (0, 0)),
    ]
    args = [h, mod, g, w]
    if rope:
        in_specs += [pl.BlockSpec((tm, LANES), lambda b, t: (t, 0))] * 3
        args += list(rope_tabs)
    n_out = 3 if want_q else 2
    return pl.pallas_call(
        functools.partial(_qkv_kernel, rope=rope, want_q=want_q),
        grid=(G, R // tm),
        in_specs=in_specs,
        out_specs=[row_spec] * n_out,
        out_shape=[jax.ShapeDtypeStruct((G, R, D_MODEL), BF16)] * n_out,
        compiler_params=pltpu.CompilerParams(vmem_limit_bytes=VMEM_LIMIT),
        name="qkv_rope" if rope else "qkv",
    )(*args)


def _rope_tables():
    t = jnp.arange(SEQ)
    row = (t // GRID_W).astype(F32)
    col = (t % GRID_W).astype(F32)
    half = HEAD_DIM // 2
    freqs = 1.0 / (ROPE_THETA ** (jnp.arange(0, half, 2, dtype=F32) / half))
    ar = row[:, None] * freqs
    ac = col[:, None] * freqs
    ang = jnp.concatenate([ar, ar, ac, ac], axis=-1)
    cos, sin = jnp.cos(ang), jnp.sin(ang)
    first = (np.arange(HEAD_DIM) % half) < half // 2
    sa = jnp.where(first, -sin, 0.0)
    sb = jnp.where(first, 0.0, sin)
    rep = LANES // HEAD_DIM
    return tuple(jnp.tile(x, (1, rep)) for x in (cos, sa, sb))


def _stack_halves(q):
    lane = lax.broadcasted_iota(jnp.int32, q.shape, 1)
    zero = jnp.zeros_like(q)
    return jnp.concatenate([jnp.where(lane < HEAD_DIM, q, zero),
                            jnp.where(lane >= HEAD_DIM, q, zero)], axis=0)


def _with_ones(v):
    return jnp.concatenate([v, jnp.ones(v.shape, v.dtype)], axis=1)


def _diff_attn_kernel(*refs, nseg, n_cast, lam_init, block_rows, heads):
    lam_ref, subln_ref, q_ref = refs[:3]
    k_refs = refs[3:3 + nseg]
    v_refs = refs[3 + nseg:3 + 2 * nseg]
    n_in = 3 + 2 * nseg + n_cast
    o_ref = refs[n_in]
    for src_ref, dst_ref in zip(refs[n_in - n_cast:n_in], refs[n_in + 1:]):
        dst_ref[...] = src_ref[...].astype(BF16)

    lp = lam_ref[...]
    lam = (jnp.exp(jnp.sum(lp[0:1] * lp[1:2], axis=-1, keepdims=True))
           - jnp.exp(jnp.sum(lp[2:3] * lp[3:4], axis=-1, keepdims=True)) + lam_init)
    subln = subln_ref[...] * (1.0 - lam_init)

    v_aug = [[_with_ones(v_ref[0, :, head * LANES:(head + 1) * LANES]) for v_ref in v_refs]
             for head in range(heads)]

    def block(head, row0, tq):
        cols = slice(head * LANES, (head + 1) * LANES)
        qq = _stack_halves(q_ref[0, row0:row0 + tq, cols])
        s_parts = [_dot_nt(qq, k_ref[0, :, cols]) for k_ref in k_refs]
        m = s_parts[0].max(axis=-1, keepdims=True)
        for s in s_parts[1:]:
            m = jnp.maximum(m, s.max(axis=-1, keepdims=True))
        acc = None
        for s, v in zip(s_parts, v_aug[head]):
            pv = _dot(jnp.exp2(s - m).astype(BF16), v)
            acc = pv if acc is None else acc + pv
        w = 1.0 / acc[:, LANES:]
        o = acc[:tq, :LANES] * w[:tq] - acc[tq:, :LANES] * (w[tq:] * lam)
        on = o * lax.rsqrt(jnp.mean(o * o, axis=-1, keepdims=True) + EPS) * subln
        o_ref[0, row0:row0 + tq, cols] = on.astype(BF16)

    for head in range(heads):
        row0 = 0
        for tq in block_rows[head]:
            block(head, row0, tq)
            row0 += tq


def _diff_attn(lam_params, subln, q, ks, vs, *, block_rows, heads, lam_init, to_cast=()):
    B, N, _ = q.shape
    nseg = len(ks)
    assert len(block_rows) == heads and all(sum(rows) == N for rows in block_rows)
    n_h = N_LANE_TILES // heads
    steps = B * n_h

    def head_spec(n):
        return pl.BlockSpec((1, n, heads * LANES), lambda b, h: (b, 0, h))

    def cast_spec(w):
        rows, cols = w.shape
        share = 1 if rows % (steps * BF16_ROWS) == 0 else 2
        assert rows % (steps // share * BF16_ROWS) == 0
        return pl.BlockSpec((rows * share // steps, cols), lambda b, h: ((b * n_h + h) // share, 0))

    kv_specs = [head_spec(k.shape[1]) for k in ks]
    cast_specs = [cast_spec(w) for w in to_cast]
    out = pl.pallas_call(
        functools.partial(_diff_attn_kernel, nseg=nseg, n_cast=len(to_cast), lam_init=lam_init,
                          block_rows=block_rows, heads=heads),
        grid=(B, n_h),
        in_specs=[
            pl.BlockSpec((4, HEAD_DIM), lambda b, h: (0, 0)),
            pl.BlockSpec((1, LANES), lambda b, h: (0, 0)),
            head_spec(N),
        ] + kv_specs + kv_specs + cast_specs,
        out_specs=[head_spec(N)] + cast_specs,
        out_shape=[jax.ShapeDtypeStruct((B, N, D_MODEL), BF16)]
        + [jax.ShapeDtypeStruct(w.shape, BF16) for w in to_cast],
        compiler_params=pltpu.CompilerParams(vmem_limit_bytes=VMEM_LIMIT),
        name="diff_attn_%dseg" % nseg,
    )(lam_params, subln, q, *ks, *vs, *to_cast)
    return out[0], out[1:]


def _na_window_start(j):
    return min(max(j - 1, 0), NA_BLOCKS - NA_KBLK)


def _na_bias_type(j):
    return 0 if j == 0 else (2 if j == NA_BLOCKS - 1 else 1)


def _na_block_layout():
    rows = SEQ // GRID_W
    per_block = []
    for j in range(NA_BLOCKS):
        blk = []
        for a in range(NA_Q_ROWS):
            r = NA_Q_ROWS * j + a
            rs = min(max(r - NA_WIN_ROWS // 2, 0), rows - NA_WIN_ROWS)
            row = []
            for u in range(NA_K_ROWS // 2):
                kr = NA_Q_ROWS * _na_window_start(j) + 2 * u
                ok = tuple(rs <= k < rs + NA_WIN_ROWS for k in (kr, kr + 1))
                row.append((kr - r + NA_WIN_ROWS - 1,) + ok)
            blk.append(tuple(row))
        per_block.append(tuple(blk))
    assert all(per_block[j] == per_block[1] for j in range(2, NA_BLOCKS - 1))
    return (per_block[0], per_block[1], per_block[NA_BLOCKS - 1])


NA_LAYOUT = _na_block_layout()
assert all(any(lo or ro for _, lo, ro in row[NA_Q_ROWS // 2:]) for kind in NA_LAYOUT for row in kind)
NA_N_DR = 2 * NA_WIN_ROWS - 1
NA_N_PAIR = NA_N_DR + 1


def _na_pair_table(rpb):
    n_dc = 2 * NA_WIN_COLS - 1
    qc = np.arange(GRID_W)
    cs = np.clip(qc - NA_WIN_COLS // 2, 0, GRID_W - NA_WIN_COLS)
    col_ok = (qc[None, :] >= cs[:, None]) & (qc[None, :] < cs[:, None] + NA_WIN_COLS)
    dc = np.clip(qc[None, :] - qc[:, None] + NA_WIN_COLS - 1, 0, n_dc - 1)
    pick_d = np.zeros((NA_N_DR, NA_N_PAIR, 2), np.float32)
    for i in range(NA_N_PAIR):
        for p in range(2):
            if 0 <= i - 1 + p < NA_N_DR:
                pick_d[i - 1 + p, i, p] = 1.0
    pick_c = np.zeros((2, n_dc, GRID_W, 2, GRID_W), np.float32)
    for p in range(2):
        pick_c[p, dc, qc[:, None], p, qc[None, :]] = 1.0
    lhs = jnp.einsum('hdj,dip->hipj', rpb, jnp.asarray(pick_d), precision=lax.Precision.HIGHEST)
    pairs = jnp.einsum('hic,cx->hix', lhs.reshape(-1, NA_N_PAIR, 2 * n_dc),
                       jnp.asarray(pick_c.reshape(2 * n_dc, GRID_W * LANES)),
                       precision=lax.Precision.HIGHEST).reshape(-1, NA_N_PAIR, GRID_W, LANES)
    valid = pick_d.sum(0)[:, None, :, None] * col_ok[None, :, None, :]
    pairs = jnp.where(jnp.asarray(valid.reshape(NA_N_PAIR, GRID_W, LANES) > 0), pairs * LOG2E, -jnp.inf)
    return pairs.reshape(N_LANE_TILES, 2, NA_N_PAIR, GRID_W, LANES)


def _na_fill_bias(pair_ref, bias_scr):
    lane = lax.broadcasted_iota(jnp.int32, (GRID_W, LANES), 1)
    neg = jnp.full((GRID_W, LANES), -jnp.inf, F32)
    for kind, layout in enumerate(NA_LAYOUT):
        for head in range(2):
            for a, row in enumerate(layout):
                r0 = head * NA_TQ + a * GRID_W
                for u, (d, left_ok, right_ok) in enumerate(row):
                    slab = neg
                    if left_ok or right_ok:
                        assert not left_ok or 0 <= d < NA_N_DR
                        assert not right_ok or 0 <= d + 1 < NA_N_DR
                        slab = pair_ref[0, head, d + 1]
                        if not left_ok:
                            slab = jnp.where(lane < HEAD_DIM, neg, slab)
                        if not right_ok:
                            slab = jnp.where(lane < HEAD_DIM, slab, neg)
                    bias_scr[kind, r0:r0 + GRID_W, u * LANES:(u + 1) * LANES] = slab


def _nbr_attn_kernel(q_ref, k_ref, v_ref, kc_ref, vc_ref, pair_ref, o_ref, bias_scr):
    n_win = NA_K_ROWS * GRID_W
    lane = lax.broadcasted_iota(jnp.int32, (NA_TQ, LANES), 1)

    @pl.when(pl.program_id(1) == 0)
    def _():
        _na_fill_bias(pair_ref, bias_scr)

    for bb in range(q_ref.shape[0]):
        v_aug = _with_ones(v_ref[bb])
        vc_aug = _with_ones(vc_ref[bb])
        for j in range(NA_BLOCKS):
            rows = slice(j * NA_TQ, (j + 1) * NA_TQ)
            win0 = _na_window_start(j) * NA_TQ
            win = slice(win0, win0 + n_win)
            qq = _stack_halves(q_ref[bb, rows, :])
            bias = bias_scr[_na_bias_type(j)]
            t1 = win0 + NA_TQ
            k_a = jnp.concatenate([kc_ref[bb], k_ref[bb, win0:t1, :]], axis=0)
            v_a = jnp.concatenate([vc_aug, v_aug[win0:t1]], axis=0)
            s_a = _dot_nt(qq, k_a)
            s_a = jnp.concatenate([s_a[:, :CTX_LEN], s_a[:, CTX_LEN:] + bias[:, :NA_TQ]], axis=1)
            m_a = s_a.max(axis=-1, keepdims=True)
            acc_a = _dot(jnp.exp2(s_a - m_a).astype(BF16), v_a)
            s_b = _dot_nt(qq, k_ref[bb, t1:win0 + n_win, :]) + bias[:, NA_TQ:]
            m_b = s_b.max(axis=-1, keepdims=True)
            acc_b = _dot(jnp.exp2(s_b - m_b).astype(BF16), v_aug[t1:win0 + n_win])
            m = jnp.maximum(m_a, m_b)
            acc = acc_a * jnp.exp2(m_a - m) + acc_b * jnp.exp2(m_b - m)
            o2 = acc[:, :LANES] / acc[:, LANES:]
            o_ref[bb, rows, :] = jnp.where(lane < HEAD_DIM, o2[:NA_TQ], o2[NA_TQ:]).astype(BF16)


def _nbr_attn(q, k, v, kc, vc, pair_tab):
    B, N, _ = q.shape

    def head_spec(n):
        return pl.BlockSpec((NA_BATCH_PER_STEP, n, LANES), lambda hp, b: (b, 0, hp))

    pair_spec = pl.BlockSpec((1, 2, NA_N_PAIR, GRID_W, LANES), lambda hp, b: (hp, 0, 0, 0, 0))
    return pl.pallas_call(
        _nbr_attn_kernel,
        grid=(N_LANE_TILES, B // NA_BATCH_PER_STEP),
        in_specs=[head_spec(N), head_spec(N), head_spec(N), head_spec(CTX_LEN), head_spec(CTX_LEN),
                  pair_spec],
        out_specs=head_spec(N),
        out_shape=jax.ShapeDtypeStruct(q.shape, BF16),
        scratch_shapes=[pltpu.VMEM((3, 2 * NA_TQ, NA_K_ROWS * GRID_W), F32)],
        compiler_params=pltpu.CompilerParams(vmem_limit_bytes=VMEM_LIMIT,
                                             dimension_semantics=("arbitrary", "arbitrary")),
        name="nbr_attn",
    )(q, k, v, kc, vc, pair_tab)


FFN_CHUNKS = ((0, 1024), (1024, 1024), (2048, 768))

def _out_ffn_kernel(*refs, final):
    o_ref, h_ref, mod_ref, g_ref, wo_ref, wgu_ref, wd_ref = refs[:7]
    if final:
        gfin_ref, out_ref = refs[7:]
    else:
        (out_ref,) = refs[7:]
    h1 = h_ref[0] + mod_ref[0, 2:3, :] * _dot(o_ref[0], wo_ref[...])
    xn = _rmsnorm_mod(h1, g_ref[...], mod_ref[0, 3:4, :], mod_ref[0, 4:5, :]).astype(BF16)
    acc = None
    for start, size in FFN_CHUNKS:
        gate = _dot(xn, wgu_ref[:, start:start + size])
        up = _dot(xn, wgu_ref[:, D_FF + start:D_FF + start + size])
        act = (gate * jax.nn.sigmoid(gate) * up).astype(BF16)
        part = _dot(act, wd_ref[start:start + size, :])
        acc = part if acc is None else acc + part
    h2 = h1 + mod_ref[0, 5:6, :] * acc
    if final:
        h2 = h2 * lax.rsqrt(jnp.mean(h2 * h2, axis=-1, keepdims=True) + EPS) * gfin_ref[...]
    out_ref[0] = h2


def _out_ffn(o, h, mod, g, wo, wgu, wd, layer, gfin, *, tm):
    G, R, _ = h.shape
    final = gfin is not None
    row_spec = pl.BlockSpec((1, tm, D_MODEL), lambda b, t: (b, t, 0))

    def resident(shape, index):
        return pl.BlockSpec(shape, lambda b, t: index, pipeline_mode=pl.Buffered(1))

    in_specs = [
        row_spec,
        row_spec,
        pl.BlockSpec((1, 6, D_MODEL), lambda b, t: (b, 0, 0)),
        pl.BlockSpec((1, D_MODEL), lambda b, t: (0, 0)),
        resident((D_MODEL, D_MODEL), (0, 0)),
        resident((None, D_MODEL, 2 * D_FF), (layer, 0, 0)),
        resident((None, D_FF, D_MODEL), (layer, 0, 0)),
    ]
    args = [o, h, mod, g, wo, wgu, wd]
    if final:
        in_specs.append(pl.BlockSpec((1, D_MODEL), lambda b, t: (0, 0)))
        args.append(gfin)
    return pl.pallas_call(
        functools.partial(_out_ffn_kernel, final=final),
        grid=(G, R // tm),
        in_specs=in_specs,
        out_specs=row_spec,
        out_shape=jax.ShapeDtypeStruct(h.shape, F32),
        compiler_params=pltpu.CompilerParams(vmem_limit_bytes=VMEM_LIMIT),
        name="out_ffn_final" if final else "out_ffn",
    )(*args)


def kernel(x, c, ctx, c_ctx, ada_w, ada_b, norm_mix, norm_ffn, da_wqkv, da_lambda_q1, da_lambda_k1,
           da_lambda_q2, da_lambda_k2, da_subln, da_wo, na_wqkv, na_rpb, na_wo, ffn_w_gate_up,
           ffn_w_down, norm_final):
    B, N, D = x.shape
    L = ctx.shape[1]
    TM = 1024
    assert (B, N, D, L) == (BATCH, SEQ, D_MODEL, CTX_LEN) and ctx.shape == (B, L, D)
    assert ada_w.shape == (DEPTH, D, 6 * D) and ffn_w_gate_up.shape == (DEPTH, D, 2 * D_FF)
    assert da_wqkv.shape == na_wqkv.shape == (1, D, 3 * D)
    assert na_rpb.shape == (1, D // HEAD_DIM, 2 * NA_WIN_ROWS - 1, 2 * NA_WIN_COLS - 1)

    cond = jnp.zeros((COND_ROWS, D), F32).at[:B].set(c).at[CTX_MOD_ROW].set(c_ctx)
    mods = _adaln(cond, ada_w, ada_b).reshape(DEPTH, COND_ROWS, 6, D)

    h = x
    hc = ctx.reshape(1, B * L, D)
    rope_tabs = _rope_tables()

    mod_l = mods[0, :B]
    mod_c = mods[0, CTX_MOD_ROW:CTX_MOD_ROW + 1]
    g_mix = norm_mix[0].reshape(1, D)
    g_ffn = norm_ffn[0].reshape(1, D)
    wqkv = da_wqkv[0].astype(BF16)
    q_l, k_l, v_l = _qkv(h, mod_l, g_mix, wqkv, rope_tabs, tm=TM)
    q_c, k_c, v_c = (t.reshape(B, L, D) for t in _qkv(hc, mod_c, g_mix, wqkv, None, tm=TM))
    lam_params = jnp.stack([da_lambda_q1[0], da_lambda_k1[0], da_lambda_q2[0], da_lambda_k2[0]])
    subln = da_subln[0].reshape(1, LANES)
    lam_init = 0.8 - 0.6 * float(np.exp(-0.3 * 0))
    later_weights = [ffn_w_gate_up.reshape(DEPTH * D, 2 * D_FF), ffn_w_down.reshape(DEPTH * D_FF, D),
                     da_wo[0], na_wqkv[0], na_wo[0]]
    o_l, (wgu, wd, wo, na_wqkv_bf, na_wo_bf) = _diff_attn(
        lam_params, subln, q_l, [k_c, k_l], [v_c, v_l], block_rows=DA_BLOCK_ROWS,
        heads=DA_HEADS_PER_STEP, lam_init=lam_init, to_cast=later_weights)
    wgu = wgu.reshape(DEPTH, D, 2 * D_FF)
    wd = wd.reshape(DEPTH, D_FF, D)
    o_c, _ = _diff_attn(lam_params, subln, q_c, [k_c], [v_c], block_rows=((L,),) * N_LANE_TILES,
                        heads=N_LANE_TILES, lam_init=lam_init)
    h = _out_ffn(o_l, h, mod_l, g_ffn, wo, wgu, wd, 0, None, tm=TM)
    hc = _out_ffn(o_c.reshape(1, B * L, D), hc, mod_c, g_ffn, wo, wgu, wd, 0, None, tm=TM)

    mod_l = mods[1, :B]
    mod_c = mods[1, CTX_MOD_ROW:CTX_MOD_ROW + 1]
    g_mix = norm_mix[1].reshape(1, D)
    g_ffn = norm_ffn[1].reshape(1, D)
    q_l, k_l, v_l = _qkv(h, mod_l, g_mix, na_wqkv_bf, None, tm=TM)
    k_c, v_c = (t.reshape(B, L, D)
                for t in _qkv(hc, mod_c, g_mix, na_wqkv_bf, None, tm=TM, want_q=False))
    o_l = _nbr_attn(q_l, k_l, v_l, k_c, v_c, _na_pair_table(na_rpb[0]))
    return _out_ffn(o_l, h, mod_l, g_ffn, na_wo_bf, wgu, wd, 1, norm_final.reshape(1, D), tm=TM)
```

```python
import functools

import numpy as np
import jax
import jax.numpy as jnp
from jax import lax
from jax.experimental import pallas as pl
from jax.experimental.pallas import tpu as pltpu

D_MODEL = 1024
BATCH = 16
SEQ = 2048
DEPTH = 2
GRID_W = 64
CTX_LEN = 256
HEAD_DIM = 64
NA_WIN_ROWS = 8
NA_WIN_COLS = 16
D_FF = 2816
ROPE_THETA = 10000.0
EPS = 1e-6
LOG2E = 1.4426950408889634
Q_SCALE = HEAD_DIM ** -0.5 * LOG2E

LANES = 128
BF16_ROWS = 16
N_LANE_TILES = D_MODEL // LANES
COND_ROWS = 32
CTX_MOD_ROW = BATCH
VMEM_LIMIT = 56 * 1024 * 1024

NA_Q_ROWS = 4
NA_K_ROWS = 12
NA_TQ = NA_Q_ROWS * GRID_W
NA_BLOCKS = SEQ // NA_TQ
NA_KBLK = NA_K_ROWS // NA_Q_ROWS
NA_BATCH_PER_STEP = 4
DA_HEADS_PER_STEP = 2
DA_BLOCK_ROWS = ((128,) + (256,) * 7 + (128,),) * DA_HEADS_PER_STEP
F32 = jnp.float32
BF16 = jnp.bfloat16


def _dot(a, b):
    return jnp.dot(a, b, preferred_element_type=F32)


def _dot_nt(a, b):
    return lax.dot_general(a, b, (((1,), (1,)), ((), ())), preferred_element_type=F32)


def _adaln_kernel(cond_ref, w_ref, b_ref, out_ref):
    cnd = cond_ref[...]
    a = cnd * jax.nn.sigmoid(cnd)
    w = w_ref[0]
    a_hi = a.astype(BF16)
    a_lo = (a - a_hi.astype(F32)).astype(BF16)
    w_hi = w.astype(BF16)
    w_lo = (w - w_hi.astype(F32)).astype(BF16)
    p_hi = _dot(jnp.concatenate([a_hi, a_lo], axis=0), w_hi)
    acc = p_hi[:COND_ROWS] + p_hi[COND_ROWS:] + _dot(a_hi, w_lo)
    out_ref[0] = acc + b_ref[0]


def _adaln(cond, ada_w, ada_b):
    n_out = ada_w.shape[-1]
    tn = 1024
    return pl.pallas_call(
        _adaln_kernel,
        grid=(DEPTH, n_out // tn),
        in_specs=[
            pl.BlockSpec((COND_ROWS, D_MODEL), lambda i, j: (0, 0)),
            pl.BlockSpec((1, D_MODEL, tn), lambda i, j: (i, 0, j)),
            pl.BlockSpec((1, 1, tn), lambda i, j: (i, 0, j)),
        ],
        out_specs=pl.BlockSpec((1, COND_ROWS, tn), lambda i, j: (i, 0, j)),
        out_shape=jax.ShapeDtypeStruct((DEPTH, COND_ROWS, n_out), F32),
        compiler_params=pltpu.CompilerParams(vmem_limit_bytes=VMEM_LIMIT),
        name="adaln",
    )(cond, ada_w, ada_b.reshape(DEPTH, 1, n_out))


def _rmsnorm_mod(h, g, shift, scale):
    y = h * lax.rsqrt(jnp.mean(h * h, axis=-1, keepdims=True) + EPS) * g
    return y * (1.0 + scale) + shift


def _qkv_kernel(*refs, rope, want_q):
    if rope:
        h_ref, mod_ref, g_ref, w_ref, cos_ref, sa_ref, sb_ref = refs[:7]
        outs = refs[7:]
    else:
        h_ref, mod_ref, g_ref, w_ref = refs[:4]
        outs = refs[4:]
    xn = _rmsnorm_mod(h_ref[0], g_ref[...], mod_ref[0, 0:1, :], mod_ref[0, 1:2, :]).astype(BF16)

    def rotate(t):
        return (t * cos_ref[...] + pltpu.roll(t, LANES - 16, axis=1) * sa_ref[...]
                + pltpu.roll(t, 16, axis=1) * sb_ref[...])

    parts = ("q", "k", "v") if want_q else ("k", "v")
    for name, o_ref in zip(parts, outs):
        col = {"q": 0, "k": D_MODEL, "v": 2 * D_MODEL}[name]
        acc = _dot(xn, w_ref[:, col:col + D_MODEL])
        if name == "q":
            acc = acc * Q_SCALE
        if rope and name != "v":
            for c in range(N_LANE_TILES):
                sl = slice(c * LANES, (c + 1) * LANES)
                o_ref[0, :, sl] = rotate(acc[:, sl]).astype(BF16)
        else:
            o_ref[0] = acc.astype(BF16)


def _qkv(h, mod, g, w, rope_tabs, *, tm, want_q=True):
    G, R, _ = h.shape
    rope = rope_tabs is not None
    row_spec = pl.BlockSpec((1, tm, D_MODEL), lambda b, t: (b, t, 0))
    in_specs = [
        row_spec,
        pl.BlockSpec((1, 6, D_MODEL), lambda b, t: (b, 0, 0)),
        pl.BlockSpec((1, D_MODEL), lambda b, t: (0, 0)),
        pl.BlockSpec((D_MODEL, 3 * D_MODEL), lambda b, t: (0, 0)),
    ]
    args = [h, mod, g, w]
    if rope:
        in_specs += [pl.BlockSpec((tm, LANES), lambda b, t: (t, 0))] * 3
        args += list(rope_tabs)
    n_out = 3 if want_q else 2
    return pl.pallas_call(
        functools.partial(_qkv_kernel, rope=rope, want_q=want_q),
        grid=(G, R // tm),
        in_specs=in_specs,
        out_specs=[row_spec] * n_out,
        out_shape=[jax.ShapeDtypeStruct((G, R, D_MODEL), BF16)] * n_out,
        compiler_params=pltpu.CompilerParams(vmem_limit_bytes=VMEM_LIMIT),
        name="qkv_rope" if rope else "qkv",
    )(*args)


def _rope_tables():
    t = jnp.arange(SEQ)
    row = (t // GRID_W).astype(F32)
    col = (t % GRID_W).astype(F32)
    half = HEAD_DIM // 2
    freqs = 1.0 / (ROPE_THETA ** (jnp.arange(0, half, 2, dtype=F32) / half))
    ar = row[:, None] * freqs
    ac = col[:, None] * freqs
    ang = jnp.concatenate([ar, ar, ac, ac], axis=-1)
    cos, sin = jnp.cos(ang), jnp.sin(ang)
    first = (np.arange(HEAD_DIM) % half) < half // 2
    sa = jnp.where(first, -sin, 0.0)
    sb = jnp.where(first, 0.0, sin)
    rep = LANES // HEAD_DIM
    return tuple(jnp.tile(x, (1, rep)) for x in (cos, sa, sb))


def _stack_halves(q):
    lane = lax.broadcasted_iota(jnp.int32, q.shape, 1)
    zero = jnp.zeros_like(q)
    return jnp.concatenate([jnp.where(lane < HEAD_DIM, q, zero),
                            jnp.where(lane >= HEAD_DIM, q, zero)], axis=0)


def _with_ones(v):
    return jnp.concatenate([v, jnp.ones(v.shape, v.dtype)], axis=1)


def _diff_attn_kernel(*refs, nseg, n_cast, lam_init, block_rows, heads):
    lam_ref, subln_ref, q_ref = refs[:3]
    k_refs = refs[3:3 + nseg]
    v_refs = refs[3 + nseg:3 + 2 * nseg]
    n_in = 3 + 2 * nseg + n_cast
    o_ref = refs[n_in]
    for src_ref, dst_ref in zip(refs[n_in - n_cast:n_in], refs[n_in + 1:]):
        dst_ref[...] = src_ref[...].astype(BF16)

    lp = lam_ref[...]
    lam = (jnp.exp(jnp.sum(lp[0:1] * lp[1:2], axis=-1, keepdims=True))
           - jnp.exp(jnp.sum(lp[2:3] * lp[3:4], axis=-1, keepdims=True)) + lam_init)
    subln = subln_ref[...] * (1.0 - lam_init)

    v_aug = [[_with_ones(v_ref[0, :, head * LANES:(head + 1) * LANES]) for v_ref in v_refs]
             for head in range(heads)]

    def block(head, row0, tq):
        cols = slice(head * LANES, (head + 1) * LANES)
        qq = _stack_halves(q_ref[0, row0:row0 + tq, cols])
        s_parts = [_dot_nt(qq, k_ref[0, :, cols]) for k_ref in k_refs]
        m = s_parts[0].max(axis=-1, keepdims=True)
        for s in s_parts[1:]:
            m = jnp.maximum(m, s.max(axis=-1, keepdims=True))
        acc = None
        for s, v in zip(s_parts, v_aug[head]):
            pv = _dot(jnp.exp2(s - m).astype(BF16), v)
            acc = pv if acc is None else acc + pv
        w = 1.0 / acc[:, LANES:]
        o = acc[:tq, :LANES] * w[:tq] - acc[tq:, :LANES] * (w[tq:] * lam)
        on = o * lax.rsqrt(jnp.mean(o * o, axis=-1, keepdims=True) + EPS) * subln
        o_ref[0, row0:row0 + tq, cols] = on.astype(BF16)

    for head in range(heads):
        row0 = 0
        for tq in block_rows[head]:
            block(head, row0, tq)
            row0 += tq


def _diff_attn(lam_params, subln, q, ks, vs, *, block_rows, heads, lam_init, to_cast=()):
    B, N, _ = q.shape
    nseg = len(ks)
    assert len(block_rows) == heads and all(sum(rows) == N for rows in block_rows)
    n_h = N_LANE_TILES // heads
    steps = B * n_h

    def head_spec(n):
        return pl.BlockSpec((1, n, heads * LANES), lambda b, h: (b, 0, h))

    def cast_spec(w):
        rows, cols = w.shape
        share = 1 if rows % (steps * BF16_ROWS) == 0 else 2
        assert rows % (steps // share * BF16_ROWS) == 0
        return pl.BlockSpec((rows * share // steps, cols), lambda b, h: ((b * n_h + h) // share, 0))

    kv_specs = [head_spec(k.shape[1]) for k in ks]
    cast_specs = [cast_spec(w) for w in to_cast]
    out = pl.pallas_call(
        functools.partial(_diff_attn_kernel, nseg=nseg, n_cast=len(to_cast), lam_init=lam_init,
                          block_rows=block_rows, heads=heads),
        grid=(B, n_h),
        in_specs=[
            pl.BlockSpec((4, HEAD_DIM), lambda b, h: (0, 0)),
            pl.BlockSpec((1, LANES), lambda b, h: (0, 0)),
            head_spec(N),
        ] + kv_specs + kv_specs + cast_specs,
        out_specs=[head_spec(N)] + cast_specs,
        out_shape=[jax.ShapeDtypeStruct((B, N, D_MODEL), BF16)]
        + [jax.ShapeDtypeStruct(w.shape, BF16) for w in to_cast],
        compiler_params=pltpu.CompilerParams(vmem_limit_bytes=VMEM_LIMIT),
        name="diff_attn_%dseg" % nseg,
    )(lam_params, subln, q, *ks, *vs, *to_cast)
    return out[0], out[1:]


def _na_window_start(j):
    return min(max(j - 1, 0), NA_BLOCKS - NA_KBLK)


def _na_bias_type(j):
    return 0 if j == 0 else (2 if j == NA_BLOCKS - 1 else 1)


def _na_block_layout():
    rows = SEQ // GRID_W
    per_block = []
    for j in range(NA_BLOCKS):
        blk = []
        for a in range(NA_Q_ROWS):
            r = NA_Q_ROWS * j + a
            rs = min(max(r - NA_WIN_ROWS // 2, 0), rows - NA_WIN_ROWS)
            row = []
            for u in range(NA_K_ROWS // 2):
                kr = NA_Q_ROWS * _na_window_start(j) + 2 * u
                ok = tuple(rs <= k < rs + NA_WIN_ROWS for k in (kr, kr + 1))
                row.append((kr - r + NA_WIN_ROWS - 1,) + ok)
            blk.append(tuple(row))
        per_block.append(tuple(blk))
    assert all(per_block[j] == per_block[1] for j in range(2, NA_BLOCKS - 1))
    return (per_block[0], per_block[1], per_block[NA_BLOCKS - 1])


NA_LAYOUT = _na_block_layout()
assert all(any(lo or ro for _, lo, ro in row[NA_Q_ROWS // 2:]) for kind in NA_LAYOUT for row in kind)
NA_N_DR = 2 * NA_WIN_ROWS - 1
NA_N_PAIR = NA_N_DR + 1


def _na_pair_table(rpb):
    n_dc = 2 * NA_WIN_COLS - 1
    qc = np.arange(GRID_W)
    cs = np.clip(qc - NA_WIN_COLS // 2, 0, GRID_W - NA_WIN_COLS)
    col_ok = (qc[None, :] >= cs[:, None]) & (qc[None, :] < cs[:, None] + NA_WIN_COLS)
    dc = np.clip(qc[None, :] - qc[:, None] + NA_WIN_COLS - 1, 0, n_dc - 1)
    pick_d = np.zeros((NA_N_DR, NA_N_PAIR, 2), np.float32)
    for i in range(NA_N_PAIR):
        for p in range(2):
            if 0 <= i - 1 + p < NA_N_DR:
                pick_d[i - 1 + p, i, p] = 1.0
    pick_c = np.zeros((2, n_dc, GRID_W, 2, GRID_W), np.float32)
    for p in range(2):
        pick_c[p, dc, qc[:, None], p, qc[None, :]] = 1.0
    lhs = jnp.einsum('hdj,dip->hipj', rpb, jnp.asarray(pick_d), precision=lax.Precision.HIGHEST)
    pairs = jnp.einsum('hic,cx->hix', lhs.reshape(-1, NA_N_PAIR, 2 * n_dc),
                       jnp.asarray(pick_c.reshape(2 * n_dc, GRID_W * LANES)),
                       precision=lax.Precision.HIGHEST).reshape(-1, NA_N_PAIR, GRID_W, LANES)
    valid = pick_d.sum(0)[:, None, :, None] * col_ok[None, :, None, :]
    pairs = jnp.where(jnp.asarray(valid.reshape(NA_N_PAIR, GRID_W, LANES) > 0), pairs * LOG2E, -jnp.inf)
    return pairs.reshape(N_LANE_TILES, 2, NA_N_PAIR, GRID_W, LANES)


def _na_fill_bias(pair_ref, bias_scr):
    lane = lax.broadcasted_iota(jnp.int32, (GRID_W, LANES), 1)
    neg = jnp.full((GRID_W, LANES), -jnp.inf, F32)
    for kind, layout in enumerate(NA_LAYOUT):
        for head in range(2):
            for a, row in enumerate(layout):
                r0 = head * NA_TQ + a * GRID_W
                for u, (d, left_ok, right_ok) in enumerate(row):
                    slab = neg
                    if left_ok or right_ok:
                        assert not left_ok or 0 <= d < NA_N_DR
                        assert not right_ok or 0 <= d + 1 < NA_N_DR
                        slab = pair_ref[0, head, d + 1]
                        if not left_ok:
                            slab = jnp.where(lane < HEAD_DIM, neg, slab)
                        if not right_ok:
                            slab = jnp.where(lane < HEAD_DIM, slab, neg)
                    bias_scr[kind, r0:r0 + GRID_W, u * LANES:(u + 1) * LANES] = slab


def _nbr_attn_kernel(q_ref, k_ref, v_ref, kc_ref, vc_ref, pair_ref, o_ref, bias_scr):
    n_win = NA_K_ROWS * GRID_W
    lane = lax.broadcasted_iota(jnp.int32, (NA_TQ, LANES), 1)

    @pl.when(pl.program_id(1) == 0)
    def _():
        _na_fill_bias(pair_ref, bias_scr)

    for bb in range(q_ref.shape[0]):
        v_aug = _with_ones(v_ref[bb])
        vc_aug = _with_ones(vc_ref[bb])
        for j in range(NA_BLOCKS):
            rows = slice(j * NA_TQ, (j + 1) * NA_TQ)
            win0 = _na_window_start(j) * NA_TQ
            win = slice(win0, win0 + n_win)
            qq = _stack_halves(q_ref[bb, rows, :])
            bias = bias_scr[_na_bias_type(j)]
            t1 = win0 + NA_TQ
            k_a = jnp.concatenate([kc_ref[bb], k_ref[bb, win0:t1, :]], axis=0)
            v_a = jnp.concatenate([vc_aug, v_aug[win0:t1]], axis=0)
            s_a = _dot_nt(qq, k_a)
            s_a = jnp.concatenate([s_a[:, :CTX_LEN], s_a[:, CTX_LEN:] + bias[:, :NA_TQ]], axis=1)
            m_a = s_a.max(axis=-1, keepdims=True)
            acc_a = _dot(jnp.exp2(s_a - m_a).astype(BF16), v_a)
            s_b = _dot_nt(qq, k_ref[bb, t1:win0 + n_win, :]) + bias[:, NA_TQ:]
            m_b = s_b.max(axis=-1, keepdims=True)
            acc_b = _dot(jnp.exp2(s_b - m_b).astype(BF16), v_aug[t1:win0 + n_win])
            m = jnp.maximum(m_a, m_b)
            acc = acc_a * jnp.exp2(m_a - m) + acc_b * jnp.exp2(m_b - m)
            o2 = acc[:, :LANES] / acc[:, LANES:]
            o_ref[bb, rows, :] = jnp.where(lane < HEAD_DIM, o2[:NA_TQ], o2[NA_TQ:]).astype(BF16)


def _nbr_attn(q, k, v, kc, vc, pair_tab):
    B, N, _ = q.shape

    def head_spec(n):
        return pl.BlockSpec((NA_BATCH_PER_STEP, n, LANES), lambda hp, b: (b, 0, hp))

    pair_spec = pl.BlockSpec((1, 2, NA_N_PAIR, GRID_W, LANES), lambda hp, b: (hp, 0, 0, 0, 0))
    return pl.pallas_call(
        _nbr_attn_kernel,
        grid=(N_LANE_TILES, B // NA_BATCH_PER_STEP),
        in_specs=[head_spec(N), head_spec(N), head_spec(N), head_spec(CTX_LEN), head_spec(CTX_LEN),
                  pair_spec],
        out_specs=head_spec(N),
        out_shape=jax.ShapeDtypeStruct(q.shape, BF16),
        scratch_shapes=[pltpu.VMEM((3, 2 * NA_TQ, NA_K_ROWS * GRID_W), F32)],
        compiler_params=pltpu.CompilerParams(vmem_limit_bytes=VMEM_LIMIT,
                                             dimension_semantics=("arbitrary", "arbitrary")),
        name="nbr_attn",
    )(q, k, v, kc, vc, pair_tab)


FFN_CHUNKS = ((0, 1024), (1024, 1024), (2048, 768))

def _out_ffn_kernel(*refs, final):
    o_ref, h_ref, mod_ref, g_ref, wo_ref, wgu_ref, wd_ref = refs[:7]
    if final:
        gfin_ref, out_ref = refs[7:]
    else:
        (out_ref,) = refs[7:]
    h1 = h_ref[0] + mod_ref[0, 2:3, :] * _dot(o_ref[0], wo_ref[...])
    xn = _rmsnorm_mod(h1, g_ref[...], mod_ref[0, 3:4, :], mod_ref[0, 4:5, :]).astype(BF16)
    acc = None
    for start, size in FFN_CHUNKS:
        gate = _dot(xn, wgu_ref[:, start:start + size])
        up = _dot(xn, wgu_ref[:, D_FF + start:D_FF + start + size])
        act = (gate * jax.nn.sigmoid(gate) * up).astype(BF16)
        part = _dot(act, wd_ref[start:start + size, :])
        acc = part if acc is None else acc + part
    h2 = h1 + mod_ref[0, 5:6, :] * acc
    if final:
        h2 = h2 * lax.rsqrt(jnp.mean(h2 * h2, axis=-1, keepdims=True) + EPS) * gfin_ref[...]
    out_ref[0] = h2


def _out_ffn(o, h, mod, g, wo, wgu, wd, layer, gfin, *, tm):
    G, R, _ = h.shape
    final = gfin is not None
    row_spec = pl.BlockSpec((1, tm, D_MODEL), lambda b, t: (b, t, 0))

    def resident(shape, index):
        return pl.BlockSpec(shape, lambda b, t: index, pipeline_mode=pl.Buffered(1))

    in_specs = [
        row_spec,
        row_spec,
        pl.BlockSpec((1, 6, D_MODEL), lambda b, t: (b, 0, 0)),
        pl.BlockSpec((1, D_MODEL), lambda b, t: (0, 0)),
        resident((D_MODEL, D_MODEL), (0, 0)),
        resident((None, D_MODEL, 2 * D_FF), (layer, 0, 0)),
        resident((None, D_FF, D_MODEL), (layer, 0, 0)),
    ]
    args = [o, h, mod, g, wo, wgu, wd]
    if final:
        in_specs.append(pl.BlockSpec((1, D_MODEL), lambda b, t: (0, 0)))
        args.append(gfin)
    return pl.pallas_call(
        functools.partial(_out_ffn_kernel, final=final),
        grid=(G, R // tm),
        in_specs=in_specs,
        out_specs=row_spec,
        out_shape=jax.ShapeDtypeStruct(h.shape, F32),
        compiler_params=pltpu.CompilerParams(vmem_limit_bytes=VMEM_LIMIT),
        name="out_ffn_final" if final else "out_ffn",
    )(*args)


def kernel(x, c, ctx, c_ctx, ada_w, ada_b, norm_mix, norm_ffn, da_wqkv, da_lambda_q1, da_lambda_k1,
           da_lambda_q2, da_lambda_k2, da_subln, da_wo, na_wqkv, na_rpb, na_wo, ffn_w_gate_up,
           ffn_w_down, norm_final):
    B, N, D = x.shape
    L = ctx.shape[1]
    TM = 1024
    assert (B, N, D, L) == (BATCH, SEQ, D_MODEL, CTX_LEN) and ctx.shape == (B, L, D)
    assert ada_w.shape == (DEPTH, D, 6 * D) and ffn_w_gate_up.shape == (DEPTH, D, 2 * D_FF)
    assert da_wqkv.shape == na_wqkv.shape == (1, D, 3 * D)
    assert na_rpb.shape == (1, D // HEAD_DIM, 2 * NA_WIN_ROWS - 1, 2 * NA_WIN_COLS - 1)

    cond = jnp.zeros((COND_ROWS, D), F32).at[:B].set(c).at[CTX_MOD_ROW].set(c_ctx)
    mods = _adaln(cond, ada_w, ada_b).reshape(DEPTH, COND_ROWS, 6, D)

    h = x
    hc = ctx.reshape(1, B * L, D)
    rope_tabs = _rope_tables()

    mod_l = mods[0, :B]
    mod_c = mods[0, CTX_MOD_ROW:CTX_MOD_ROW + 1]
    g_mix = norm_mix[0].reshape(1, D)
    g_ffn = norm_ffn[0].reshape(1, D)
    wqkv = da_wqkv[0].astype(BF16)
    q_l, k_l, v_l = _qkv(h, mod_l, g_mix, wqkv, rope_tabs, tm=TM)
    q_c, k_c, v_c = (t.reshape(B, L, D) for t in _qkv(hc, mod_c, g_mix, wqkv, None, tm=TM))
    lam_params = jnp.stack([da_lambda_q1[0], da_lambda_k1[0], da_lambda_q2[0], da_lambda_k2[0]])
    subln = da_subln[0].reshape(1, LANES)
    lam_init = 0.8 - 0.6 * float(np.exp(-0.3 * 0))
    later_weights = [ffn_w_gate_up.reshape(DEPTH * D, 2 * D_FF), ffn_w_down.reshape(DEPTH * D_FF, D),
                     da_wo[0], na_wqkv[0], na_wo[0]]
    o_l, (wgu, wd, wo, na_wqkv_bf, na_wo_bf) = _diff_attn(
        lam_params, subln, q_l, [k_c, k_l], [v_c, v_l], block_rows=DA_BLOCK_ROWS,
        heads=DA_HEADS_PER_STEP, lam_init=lam_init, to_cast=later_weights)
    wgu = wgu.reshape(DEPTH, D, 2 * D_FF)
    wd = wd.reshape(DEPTH, D_FF, D)
    o_c, _ = _diff_attn(lam_params, subln, q_c, [k_c], [v_c], block_rows=((L,),) * N_LANE_TILES,
                        heads=N_LANE_TILES, lam_init=lam_init)
    h = _out_ffn(o_l, h, mod_l, g_ffn, wo, wgu, wd, 0, None, tm=TM)
    hc = _out_ffn(o_c.reshape(1, B * L, D), hc, mod_c, g_ffn, wo, wgu, wd, 0, None, tm=TM)

    mod_l = mods[1, :B]
    mod_c = mods[1, CTX_MOD_ROW:CTX_MOD_ROW + 1]
    g_mix = norm_mix[1].reshape(1, D)
    g_ffn = norm_ffn[1].reshape(1, D)
    q_l, k_l, v_l = _qkv(h, mod_l, g_mix, na_wqkv_bf, None, tm=TM)
    k_c, v_c = (t.reshape(B, L, D)
                for t in _qkv(hc, mod_c, g_mix, na_wqkv_bf, None, tm=TM, want_q=False))
    o_l = _nbr_attn(q_l, k_l, v_l, k_c, v_c, _na_pair_table(na_rpb[0]))
    return _out_ffn(o_l, h, mod_l, g_ffn, na_wo_bf, wgu, wd, 1, norm_final.reshape(1, D), tm=TM)
```

```python
import functools

import numpy as np
import jax
import jax.numpy as jnp
from jax import lax
from jax.experimental import pallas as pl
from jax.experimental.pallas import tpu as pltpu

D_MODEL = 1024
BATCH = 16
SEQ = 2048
DEPTH = 2
GRID_W = 64
CTX_LEN = 256
HEAD_DIM = 64
NA_WIN_ROWS = 8
NA_WIN_COLS = 16
D_FF = 2816
ROPE_THETA = 10000.0
EPS = 1e-6
LOG2E = 1.4426950408889634
Q_SCALE = HEAD_DIM ** -0.5 * LOG2E

LANES = 128
BF16_ROWS = 16
N_LANE_TILES = D_MODEL // LANES
COND_ROWS = 32
CTX_MOD_ROW = BATCH
VMEM_LIMIT = 56 * 1024 * 1024

NA_Q_ROWS = 4
NA_K_ROWS = 12
NA_TQ = NA_Q_ROWS * GRID_W
NA_BLOCKS = SEQ // NA_TQ
NA_KBLK = NA_K_ROWS // NA_Q_ROWS
NA_BATCH_PER_STEP = 4
DA_HEADS_PER_STEP = 2
DA_BLOCK_ROWS = ((128,) + (256,) * 7 + (128,),) * DA_HEADS_PER_STEP
F32 = jnp.float32
BF16 = jnp.bfloat16


def _dot(a, b):
    return jnp.dot(a, b, preferred_element_type=F32)


def _dot_nt(a, b):
    return lax.dot_general(a, b, (((1,), (1,)), ((), ())), preferred_element_type=F32)


def _adaln_kernel(cond_ref, w_ref, b_ref, out_ref):
    cnd = cond_ref[...]
    a = cnd * jax.nn.sigmoid(cnd)
    w = w_ref[0]
    a_hi = a.astype(BF16)
    a_lo = (a - a_hi.astype(F32)).astype(BF16)
    w_hi = w.astype(BF16)
    w_lo = (w - w_hi.astype(F32)).astype(BF16)
    p_hi = _dot(jnp.concatenate([a_hi, a_lo], axis=0), w_hi)
    acc = p_hi[:COND_ROWS] + p_hi[COND_ROWS:] + _dot(a_hi, w_lo)
    out_ref[0] = acc + b_ref[0]


def _adaln(cond, ada_w, ada_b):
    n_out = ada_w.shape[-1]
    tn = 1024
    return pl.pallas_call(
        _adaln_kernel,
        grid=(DEPTH, n_out // tn),
        in_specs=[
            pl.BlockSpec((COND_ROWS, D_MODEL), lambda i, j: (0, 0)),
            pl.BlockSpec((1, D_MODEL, tn), lambda i, j: (i, 0, j)),
            pl.BlockSpec((1, 1, tn), lambda i, j: (i, 0, j)),
        ],
        out_specs=pl.BlockSpec((1, COND_ROWS, tn), lambda i, j: (i, 0, j)),
        out_shape=jax.ShapeDtypeStruct((DEPTH, COND_ROWS, n_out), F32),
        compiler_params=pltpu.CompilerParams(vmem_limit_bytes=VMEM_LIMIT),
        name="adaln",
    )(cond, ada_w, ada_b.reshape(DEPTH, 1, n_out))


def _rmsnorm_mod(h, g, shift, scale):
    y = h * lax.rsqrt(jnp.mean(h * h, axis=-1, keepdims=True) + EPS) * g
    return y * (1.0 + scale) + shift


def _qkv_kernel(*refs, rope, want_q):
    if rope:
        h_ref, mod_ref, g_ref, w_ref, cos_ref, sa_ref, sb_ref = refs[:7]
        outs = refs[7:]
    else:
        h_ref, mod_ref, g_ref, w_ref = refs[:4]
        outs = refs[4:]
    xn = _rmsnorm_mod(h_ref[0], g_ref[...], mod_ref[0, 0:1, :], mod_ref[0, 1:2, :]).astype(BF16)

    def rotate(t):
        return (t * cos_ref[...] + pltpu.roll(t, LANES - 16, axis=1) * sa_ref[...]
                + pltpu.roll(t, 16, axis=1) * sb_ref[...])

    parts = ("q", "k", "v") if want_q else ("k", "v")
    for name, o_ref in zip(parts, outs):
        col = {"q": 0, "k": D_MODEL, "v": 2 * D_MODEL}[name]
        acc = _dot(xn, w_ref[:, col:col + D_MODEL])
        if name == "q":
            acc = acc * Q_SCALE
        if rope and name != "v":
            for c in range(N_LANE_TILES):
                sl = slice(c * LANES, (c + 1) * LANES)
                o_ref[0, :, sl] = rotate(acc[:, sl]).astype(BF16)
        else:
            o_ref[0] = acc.astype(BF16)


def _qkv(h, mod, g, w, rope_tabs, *, tm, want_q=True):
    G, R, _ = h.shape
    rope = rope_tabs is not None
    row_spec = pl.BlockSpec((1, tm, D_MODEL), lambda b, t: (b, t, 0))
    in_specs = [
        row_spec,
        pl.BlockSpec((1, 6, D_MODEL), lambda b, t: (b, 0, 0)),
        pl.BlockSpec((1, D_MODEL), lambda b, t: (0, 0)),
        pl.BlockSpec((D_MODEL, 3 * D_MODEL), lambda b, t: (0, 0)),
    ]
    args = [h, mod, g, w]
    if rope:
        in_specs += [pl.BlockSpec((tm, LANES), lambda b, t: (t, 0))] * 3
        args += list(rope_tabs)
    n_out = 3 if want_q else 2
    return pl.pallas_call(
        functools.partial(_qkv_kernel, rope=rope, want_q=want_q),
        grid=(G, R // tm),
        in_specs=in_specs,
        out_specs=[row_spec] * n_out,
        out_shape=[jax.ShapeDtypeStruct((G, R, D_MODEL), BF16)] * n_out,
        compiler_params=pltpu.CompilerParams(vmem_limit_bytes=VMEM_LIMIT),
        name="qkv_rope" if rope else "qkv",
    )(*args)


def _rope_tables():
    t = jnp.arange(SEQ)
    row = (t // GRID_W).astype(F32)
    col = (t % GRID_W).astype(F32)
    half = HEAD_DIM // 2
    freqs = 1.0 / (ROPE_THETA ** (jnp.arange(0, half, 2, dtype=F32) / half))
    ar = row[:, None] * freqs
    ac = col[:, None] * freqs
    ang = jnp.concatenate([ar, ar, ac, ac], axis=-1)
    cos, sin = jnp.cos(ang), jnp.sin(ang)
    first = (np.arange(HEAD_DIM) % half) < half // 2
    sa = jnp.where(first, -sin, 0.0)
    sb = jnp.where(first, 0.0, sin)
    rep = LANES // HEAD_DIM
    return tuple(jnp.tile(x, (1, rep)) for x in (cos, sa, sb))


def _stack_halves(q):
    lane = lax.broadcasted_iota(jnp.int32, q.shape, 1)
    zero = jnp.zeros_like(q)
    return jnp.concatenate([jnp.where(lane < HEAD_DIM, q, zero),
                            jnp.where(lane >= HEAD_DIM, q, zero)], axis=0)


def _with_ones(v):
    return jnp.concatenate([v, jnp.ones(v.shape, v.dtype)], axis=1)


def _diff_attn_kernel(*refs, nseg, n_cast, lam_init, block_rows, heads):
    lam_ref, subln_ref, q_ref = refs[:3]
    k_refs = refs[3:3 + nseg]
    v_refs = refs[3 + nseg:3 + 2 * nseg]
    n_in = 3 + 2 * nseg + n_cast
    o_ref = refs[n_in]
    for src_ref, dst_ref in zip(refs[n_in - n_cast:n_in], refs[n_in + 1:]):
        dst_ref[...] = src_ref[...].astype(BF16)

    lp = lam_ref[...]
    lam = (jnp.exp(jnp.sum(lp[0:1] * lp[1:2], axis=-1, keepdims=True))
           - jnp.exp(jnp.sum(lp[2:3] * lp[3:4], axis=-1, keepdims=True)) + lam_init)
    subln = subln_ref[...] * (1.0 - lam_init)

    v_aug = [[_with_ones(v_ref[0, :, head * LANES:(head + 1) * LANES]) for v_ref in v_refs]
             for head in range(heads)]

    def block(head, row0, tq):
        cols = slice(head * LANES, (head + 1) * LANES)
        qq = _stack_halves(q_ref[0, row0:row0 + tq, cols])
        s_parts = [_dot_nt(qq, k_ref[0, :, cols]) for k_ref in k_refs]
        m = s_parts[0].max(axis=-1, keepdims=True)
        for s in s_parts[1:]:
            m = jnp.maximum(m, s.max(axis=-1, keepdims=True))
        acc = None
        for s, v in zip(s_parts, v_aug[head]):
            pv = _dot(jnp.exp2(s - m).astype(BF16), v)
            acc = pv if acc is None else acc + pv
        w = 1.0 / acc[:, LANES:]
        o = acc[:tq, :LANES] * w[:tq] - acc[tq:, :LANES] * (w[tq:] * lam)
        on = o * lax.rsqrt(jnp.mean(o * o, axis=-1, keepdims=True) + EPS) * subln
        o_ref[0, row0:row0 + tq, cols] = on.astype(BF16)

    for head in range(heads):
        row0 = 0
        for tq in block_rows[head]:
            block(head, row0, tq)
            row0 += tq


def _diff_attn(lam_params, subln, q, ks, vs, *, block_rows, heads, lam_init, to_cast=()):
    B, N, _ = q.shape
    nseg = len(ks)
    assert len(block_rows) == heads and all(sum(rows) == N for rows in block_rows)
    n_h = N_LANE_TILES // heads
    steps = B * n_h

    def head_spec(n):
        return pl.BlockSpec((1, n, heads * LANES), lambda b, h: (b, 0, h))

    def cast_spec(w):
        rows, cols = w.shape
        share = 1 if rows % (steps * BF16_ROWS) == 0 else 2
        assert rows % (steps // share * BF16_ROWS) == 0
        return pl.BlockSpec((rows * share // steps, cols), lambda b, h: ((b * n_h + h) // share, 0))

    kv_specs = [head_spec(k.shape[1]) for k in ks]
    cast_specs = [cast_spec(w) for w in to_cast]
    out = pl.pallas_call(
        functools.partial(_diff_attn_kernel, nseg=nseg, n_cast=len(to_cast), lam_init=lam_init,
                          block_rows=block_rows, heads=heads),
        grid=(B, n_h),
        in_specs=[
            pl.BlockSpec((4, HEAD_DIM), lambda b, h: (0, 0)),
            pl.BlockSpec((1, LANES), lambda b, h: (0, 0)),
            head_spec(N),
        ] + kv_specs + kv_specs + cast_specs,
        out_specs=[head_spec(N)] + cast_specs,
        out_shape=[jax.ShapeDtypeStruct((B, N, D_MODEL), BF16)]
        + [jax.ShapeDtypeStruct(w.shape, BF16) for w in to_cast],
        compiler_params=pltpu.CompilerParams(vmem_limit_bytes=VMEM_LIMIT),
        name="diff_attn_%dseg" % nseg,
    )(lam_params, subln, q, *ks, *vs, *to_cast)
    return out[0], out[1:]


def _na_window_start(j):
    return min(max(j - 1, 0), NA_BLOCKS - NA_KBLK)


def _na_bias_type(j):
    return 0 if j == 0 else (2 if j == NA_BLOCKS - 1 else 1)


def _na_block_layout():
    rows = SEQ // GRID_W
    per_block = []
    for j in range(NA_BLOCKS):
        blk = []
        for a in range(NA_Q_ROWS):
            r = NA_Q_ROWS * j + a
            rs = min(max(r - NA_WIN_ROWS // 2, 0), rows - NA_WIN_ROWS)
            row = []
            for u in range(NA_K_ROWS // 2):
                kr = NA_Q_ROWS * _na_window_start(j) + 2 * u
                ok = tuple(rs <= k < rs + NA_WIN_ROWS for k in (kr, kr + 1))
                row.append((kr - r + NA_WIN_ROWS - 1,) + ok)
            blk.append(tuple(row))
        per_block.append(tuple(blk))
    assert all(per_block[j] == per_block[1] for j in range(2, NA_BLOCKS - 1))
    return (per_block[0], per_block[1], per_block[NA_BLOCKS - 1])


NA_LAYOUT = _na_block_layout()
assert all(any(lo or ro for _, lo, ro in row[NA_Q_ROWS // 2:]) for kind in NA_LAYOUT for row in kind)
NA_N_DR = 2 * NA_WIN_ROWS - 1
NA_N_PAIR = NA_N_DR + 1


def _na_pair_table(rpb):
    n_dc = 2 * NA_WIN_COLS - 1
    qc = np.arange(GRID_W)
    cs = np.clip(qc - NA_WIN_COLS // 2, 0, GRID_W - NA_WIN_COLS)
    col_ok = (qc[None, :] >= cs[:, None]) & (qc[None, :] < cs[:, None] + NA_WIN_COLS)
    dc = np.clip(qc[None, :] - qc[:, None] + NA_WIN_COLS - 1, 0, n_dc - 1)
    pick_d = np.zeros((NA_N_DR, NA_N_PAIR, 2), np.float32)
    for i in range(NA_N_PAIR):
        for p in range(2):
            if 0 <= i - 1 + p < NA_N_DR:
                pick_d[i - 1 + p, i, p] = 1.0
    pick_c = np.zeros((2, n_dc, GRID_W, 2, GRID_W), np.float32)
    for p in range(2):
        pick_c[p, dc, qc[:, None], p, qc[None, :]] = 1.0
    lhs = jnp.einsum('hdj,dip->hipj', rpb, jnp.asarray(pick_d), precision=lax.Precision.HIGHEST)
    pairs = jnp.einsum('hic,cx->hix', lhs.reshape(-1, NA_N_PAIR, 2 * n_dc),
                       jnp.asarray(pick_c.reshape(2 * n_dc, GRID_W * LANES)),
                       precision=lax.Precision.HIGHEST).reshape(-1, NA_N_PAIR, GRID_W, LANES)
    valid = pick_d.sum(0)[:, None, :, None] * col_ok[None, :, None, :]
    pairs = jnp.where(jnp.asarray(valid.reshape(NA_N_PAIR, GRID_W, LANES) > 0), pairs * LOG2E, -jnp.inf)
    return pairs.reshape(N_LANE_TILES, 2, NA_N_PAIR, GRID_W, LANES)


def _na_fill_bias(pair_ref, bias_scr):
    lane = lax.broadcasted_iota(jnp.int32, (GRID_W, LANES), 1)
    neg = jnp.full((GRID_W, LANES), -jnp.inf, F32)
    bias_scr[:, :, :CTX_LEN] = jnp.zeros((len(NA_LAYOUT), 2 * NA_TQ, CTX_LEN), F32)
    for kind, layout in enumerate(NA_LAYOUT):
        for head in range(2):
            for a, row in enumerate(layout):
                r0 = head * NA_TQ + a * GRID_W
                for u, (d, left_ok, right_ok) in enumerate(row):
                    slab = neg
                    if left_ok or right_ok:
                        assert not left_ok or 0 <= d < NA_N_DR
                        assert not right_ok or 0 <= d + 1 < NA_N_DR
                        slab = pair_ref[0, head, d + 1]
                        if not left_ok:
                            slab = jnp.where(lane < HEAD_DIM, neg, slab)
                        if not right_ok:
                            slab = jnp.where(lane < HEAD_DIM, slab, neg)
                    c0 = CTX_LEN + u * LANES
                    bias_scr[kind, r0:r0 + GRID_W, c0:c0 + LANES] = slab


def _nbr_attn_kernel(q_ref, k_ref, v_ref, kc_ref, vc_ref, pair_ref, o_ref, bias_scr):
    n_win = NA_K_ROWS * GRID_W
    lane = lax.broadcasted_iota(jnp.int32, (NA_TQ, LANES), 1)

    @pl.when(pl.program_id(1) == 0)
    def _():
        _na_fill_bias(pair_ref, bias_scr)

    for bb in range(q_ref.shape[0]):
        v_aug = _with_ones(v_ref[bb])
        vc_aug = _with_ones(vc_ref[bb])
        for j in range(NA_BLOCKS):
            rows = slice(j * NA_TQ, (j + 1) * NA_TQ)
            win0 = _na_window_start(j) * NA_TQ
            win = slice(win0, win0 + n_win)
            qq = _stack_halves(q_ref[bb, rows, :])
            bias = bias_scr[_na_bias_type(j)]
            t1 = win0 + NA_TQ
            k_a = jnp.concatenate([kc_ref[bb], k_ref[bb, win0:t1, :]], axis=0)
            v_a = jnp.concatenate([vc_aug, v_aug[win0:t1]], axis=0)
            s_a = _dot_nt(qq, k_a) + bias[:, :CTX_LEN + NA_TQ]
            m_a = s_a.max(axis=-1, keepdims=True)
            acc_a = _dot(jnp.exp2(s_a - m_a).astype(BF16), v_a)
            s_b = _dot_nt(qq, k_ref[bb, t1:win0 + n_win, :]) + bias[:, CTX_LEN + NA_TQ:]
            m_b = s_b.max(axis=-1, keepdims=True)
            acc_b = _dot(jnp.exp2(s_b - m_b).astype(BF16), v_aug[t1:win0 + n_win])
            m = jnp.maximum(m_a, m_b)
            acc = acc_a * jnp.exp2(m_a - m) + acc_b * jnp.exp2(m_b - m)
            o2 = acc[:, :LANES] / acc[:, LANES:]
            o_ref[bb, rows, :] = jnp.where(lane < HEAD_DIM, o2[:NA_TQ], o2[NA_TQ:]).astype(BF16)


def _nbr_attn(q, k, v, kc, vc, pair_tab):
    B, N, _ = q.shape

    def head_spec(n):
        return pl.BlockSpec((NA_BATCH_PER_STEP, n, LANES), lambda hp, b: (b, 0, hp))

    pair_spec = pl.BlockSpec((1, 2, NA_N_PAIR, GRID_W, LANES), lambda hp, b: (hp, 0, 0, 0, 0))
    return pl.pallas_call(
        _nbr_attn_kernel,
        grid=(N_LANE_TILES, B // NA_BATCH_PER_STEP),
        in_specs=[head_spec(N), head_spec(N), head_spec(N), head_spec(CTX_LEN), head_spec(CTX_LEN),
                  pair_spec],
        out_specs=head_spec(N),
        out_shape=jax.ShapeDtypeStruct(q.shape, BF16),
        scratch_shapes=[pltpu.VMEM((3, 2 * NA_TQ, CTX_LEN + NA_K_ROWS * GRID_W), F32)],
        compiler_params=pltpu.CompilerParams(vmem_limit_bytes=VMEM_LIMIT,
                                             dimension_semantics=("arbitrary", "arbitrary")),
        name="nbr_attn",
    )(q, k, v, kc, vc, pair_tab)


FFN_CHUNKS = ((0, 1024), (1024, 1024), (2048, 768))

def _out_ffn_kernel(*refs, final):
    o_ref, h_ref, mod_ref, g_ref, wo_ref, wgu_ref, wd_ref = refs[:7]
    if final:
        gfin_ref, out_ref = refs[7:]
    else:
        (out_ref,) = refs[7:]
    h1 = h_ref[0] + mod_ref[0, 2:3, :] * _dot(o_ref[0], wo_ref[...])
    xn = _rmsnorm_mod(h1, g_ref[...], mod_ref[0, 3:4, :], mod_ref[0, 4:5, :]).astype(BF16)
    acc = None
    for start, size in FFN_CHUNKS:
        gate = _dot(xn, wgu_ref[:, start:start + size])
        up = _dot(xn, wgu_ref[:, D_FF + start:D_FF + start + size])
        act = (gate * jax.nn.sigmoid(gate) * up).astype(BF16)
        part = _dot(act, wd_ref[start:start + size, :])
        acc = part if acc is None else acc + part
    h2 = h1 + mod_ref[0, 5:6, :] * acc
    if final:
        h2 = h2 * lax.rsqrt(jnp.mean(h2 * h2, axis=-1, keepdims=True) + EPS) * gfin_ref[...]
    out_ref[0] = h2


def _out_ffn(o, h, mod, g, wo, wgu, wd, layer, gfin, *, tm):
    G, R, _ = h.shape
    final = gfin is not None
    row_spec = pl.BlockSpec((1, tm, D_MODEL), lambda b, t: (b, t, 0))

    def resident(shape, index):
        return pl.BlockSpec(shape, lambda b, t: index, pipeline_mode=pl.Buffered(1))

    in_specs = [
        row_spec,
        row_spec,
        pl.BlockSpec((1, 6, D_MODEL), lambda b, t: (b, 0, 0)),
        pl.BlockSpec((1, D_MODEL), lambda b, t: (0, 0)),
        resident((D_MODEL, D_MODEL), (0, 0)),
        resident((None, D_MODEL, 2 * D_FF), (layer, 0, 0)),
        resident((None, D_FF, D_MODEL), (layer, 0, 0)),
    ]
    args = [o, h, mod, g, wo, wgu, wd]
    if final:
        in_specs.append(pl.BlockSpec((1, D_MODEL), lambda b, t: (0, 0)))
        args.append(gfin)
    return pl.pallas_call(
        functools.partial(_out_ffn_kernel, final=final),
        grid=(G, R // tm),
        in_specs=in_specs,
        out_specs=row_spec,
        out_shape=jax.ShapeDtypeStruct(h.shape, F32),
        compiler_params=pltpu.CompilerParams(vmem_limit_bytes=VMEM_LIMIT),
        name="out_ffn_final" if final else "out_ffn",
    )(*args)


def kernel(x, c, ctx, c_ctx, ada_w, ada_b, norm_mix, norm_ffn, da_wqkv, da_lambda_q1, da_lambda_k1,
           da_lambda_q2, da_lambda_k2, da_subln, da_wo, na_wqkv, na_rpb, na_wo, ffn_w_gate_up,
           ffn_w_down, norm_final):
    B, N, D = x.shape
    L = ctx.shape[1]
    TM = 1024
    assert (B, N, D, L) == (BATCH, SEQ, D_MODEL, CTX_LEN) and ctx.shape == (B, L, D)
    assert ada_w.shape == (DEPTH, D, 6 * D) and ffn_w_gate_up.shape == (DEPTH, D, 2 * D_FF)
    assert da_wqkv.shape == na_wqkv.shape == (1, D, 3 * D)
    assert na_rpb.shape == (1, D // HEAD_DIM, 2 * NA_WIN_ROWS - 1, 2 * NA_WIN_COLS - 1)

    cond = jnp.zeros((COND_ROWS, D), F32).at[:B].set(c).at[CTX_MOD_ROW].set(c_ctx)
    mods = _adaln(cond, ada_w, ada_b).reshape(DEPTH, COND_ROWS, 6, D)

    h = x
    hc = ctx.reshape(1, B * L, D)
    rope_tabs = _rope_tables()

    mod_l = mods[0, :B]
    mod_c = mods[0, CTX_MOD_ROW:CTX_MOD_ROW + 1]
    g_mix = norm_mix[0].reshape(1, D)
    g_ffn = norm_ffn[0].reshape(1, D)
    wqkv = da_wqkv[0].astype(BF16)
    q_l, k_l, v_l = _qkv(h, mod_l, g_mix, wqkv, rope_tabs, tm=TM)
    q_c, k_c, v_c = (t.reshape(B, L, D) for t in _qkv(hc, mod_c, g_mix, wqkv, None, tm=TM))
    lam_params = jnp.stack([da_lambda_q1[0], da_lambda_k1[0], da_lambda_q2[0], da_lambda_k2[0]])
    subln = da_subln[0].reshape(1, LANES)
    lam_init = 0.8 - 0.6 * float(np.exp(-0.3 * 0))
    later_weights = [ffn_w_gate_up.reshape(DEPTH * D, 2 * D_FF), ffn_w_down.reshape(DEPTH * D_FF, D),
                     da_wo[0], na_wqkv[0], na_wo[0]]
    o_l, (wgu, wd, wo, na_wqkv_bf, na_wo_bf) = _diff_attn(
        lam_params, subln, q_l, [k_c, k_l], [v_c, v_l], block_rows=DA_BLOCK_ROWS,
        heads=DA_HEADS_PER_STEP, lam_init=lam_init, to_cast=later_weights)
    wgu = wgu.reshape(DEPTH, D, 2 * D_FF)
    wd = wd.reshape(DEPTH, D_FF, D)
    o_c, _ = _diff_attn(lam_params, subln, q_c, [k_c], [v_c], block_rows=((L,),) * N_LANE_TILES,
                        heads=N_LANE_TILES, lam_init=lam_init)
    h = _out_ffn(o_l, h, mod_l, g_ffn, wo, wgu, wd, 0, None, tm=TM)
    hc = _out_ffn(o_c.reshape(1, B * L, D), hc, mod_c, g_ffn, wo, wgu, wd, 0, None, tm=TM)

    mod_l = mods[1, :B]
    mod_c = mods[1, CTX_MOD_ROW:CTX_MOD_ROW + 1]
    g_mix = norm_mix[1].reshape(1, D)
    g_ffn = norm_ffn[1].reshape(1, D)
    q_l, k_l, v_l = _qkv(h, mod_l, g_mix, na_wqkv_bf, None, tm=TM)
    k_c, v_c = (t.reshape(B, L, D)
                for t in _qkv(hc, mod_c, g_mix, na_wqkv_bf, None, tm=TM, want_q=False))
    o_l = _nbr_attn(q_l, k_l, v_l, k_c, v_c, _na_pair_table(na_rpb[0]))
    return _out_ffn(o_l, h, mod_l, g_ffn, na_wo_bf, wgu, wd, 1, norm_final.reshape(1, D), tm=TM)
```

```python
import functools

import numpy as np
import jax
import jax.numpy as jnp
from jax import lax
from jax.experimental import pallas as pl
from jax.experimental.pallas import tpu as pltpu

D_MODEL = 1024
BATCH = 16
SEQ = 2048
DEPTH = 2
GRID_W = 64
CTX_LEN = 256
HEAD_DIM = 64
NA_WIN_ROWS = 8
NA_WIN_COLS = 16
D_FF = 2816
ROPE_THETA = 10000.0
EPS = 1e-6
LOG2E = 1.4426950408889634
Q_SCALE = HEAD_DIM ** -0.5 * LOG2E

LANES = 128
BF16_ROWS = 16
N_LANE_TILES = D_MODEL // LANES
COND_ROWS = 32
CTX_MOD_ROW = BATCH
VMEM_LIMIT = 56 * 1024 * 1024

NA_Q_ROWS = 4
NA_K_ROWS = 12
NA_TQ = NA_Q_ROWS * GRID_W
NA_BLOCKS = SEQ // NA_TQ
NA_KBLK = NA_K_ROWS // NA_Q_ROWS
NA_BATCH_PER_STEP = 2
DA_HEADS_PER_STEP = 2
DA_BLOCK_ROWS = ((128,) + (256,) * 7 + (128,),) * DA_HEADS_PER_STEP
F32 = jnp.float32
BF16 = jnp.bfloat16


def _dot(a, b):
    return jnp.dot(a, b, preferred_element_type=F32)


def _dot_nt(a, b):
    return lax.dot_general(a, b, (((1,), (1,)), ((), ())), preferred_element_type=F32)


def _adaln_kernel(cond_ref, w_ref, b_ref, out_ref):
    cnd = cond_ref[...]
    a = cnd * jax.nn.sigmoid(cnd)
    w = w_ref[0]
    a_hi = a.astype(BF16)
    a_lo = (a - a_hi.astype(F32)).astype(BF16)
    w_hi = w.astype(BF16)
    w_lo = (w - w_hi.astype(F32)).astype(BF16)
    p_hi = _dot(jnp.concatenate([a_hi, a_lo], axis=0), w_hi)
    acc = p_hi[:COND_ROWS] + p_hi[COND_ROWS:] + _dot(a_hi, w_lo)
    out_ref[0] = acc + b_ref[0]


def _adaln(cond, ada_w, ada_b):
    n_out = ada_w.shape[-1]
    tn = 1024
    return pl.pallas_call(
        _adaln_kernel,
        grid=(DEPTH, n_out // tn),
        in_specs=[
            pl.BlockSpec((COND_ROWS, D_MODEL), lambda i, j: (0, 0)),
            pl.BlockSpec((1, D_MODEL, tn), lambda i, j: (i, 0, j)),
            pl.BlockSpec((1, 1, tn), lambda i, j: (i, 0, j)),
        ],
        out_specs=pl.BlockSpec((1, COND_ROWS, tn), lambda i, j: (i, 0, j)),
        out_shape=jax.ShapeDtypeStruct((DEPTH, COND_ROWS, n_out), F32),
        compiler_params=pltpu.CompilerParams(vmem_limit_bytes=VMEM_LIMIT),
        name="adaln",
    )(cond, ada_w, ada_b.reshape(DEPTH, 1, n_out))


def _rmsnorm_mod(h, g, shift, scale):
    y = h * lax.rsqrt(jnp.mean(h * h, axis=-1, keepdims=True) + EPS) * g
    return y * (1.0 + scale) + shift


def _qkv_kernel(*refs, rope, want_q):
    if rope:
        h_ref, mod_ref, g_ref, w_ref, cos_ref, sa_ref, sb_ref = refs[:7]
        outs = refs[7:]
    else:
        h_ref, mod_ref, g_ref, w_ref = refs[:4]
        outs = refs[4:]
    xn = _rmsnorm_mod(h_ref[0], g_ref[...], mod_ref[0, 0:1, :], mod_ref[0, 1:2, :]).astype(BF16)

    def rotate(t):
        return (t * cos_ref[...] + pltpu.roll(t, LANES - 16, axis=1) * sa_ref[...]
                + pltpu.roll(t, 16, axis=1) * sb_ref[...])

    parts = ("q", "k", "v") if want_q else ("k", "v")
    for name, o_ref in zip(parts, outs):
        col = {"q": 0, "k": D_MODEL, "v": 2 * D_MODEL}[name]
        acc = _dot(xn, w_ref[:, col:col + D_MODEL])
        if name == "q":
            acc = acc * Q_SCALE
        if rope and name != "v":
            for c in range(N_LANE_TILES):
                sl = slice(c * LANES, (c + 1) * LANES)
                o_ref[0, :, sl] = rotate(acc[:, sl]).astype(BF16)
        else:
            o_ref[0] = acc.astype(BF16)


def _qkv(h, mod, g, w, rope_tabs, *, tm, want_q=True):
    G, R, _ = h.shape
    rope = rope_tabs is not None
    row_spec = pl.BlockSpec((1, tm, D_MODEL), lambda b, t: (b, t, 0))
    in_specs = [
        row_spec,
        pl.BlockSpec((1, 6, D_MODEL), lambda b, t: (b, 0, 0)),
        pl.BlockSpec((1, D_MODEL), lambda b, t: (0, 0)),
        pl.BlockSpec((D_MODEL, 3 * D_MODEL), lambda b, t: (0, 0)),
    ]
    args = [h, mod, g, w]
    if rope:
        in_specs += [pl.BlockSpec((tm, LANES), lambda b, t: (t, 0))] * 3
        args += list(rope_tabs)
    n_out = 3 if want_q else 2
    return pl.pallas_call(
        functools.partial(_qkv_kernel, rope=rope, want_q=want_q),
        grid=(G, R // tm),
        in_specs=in_specs,
        out_specs=[row_spec] * n_out,
        out_shape=[jax.ShapeDtypeStruct((G, R, D_MODEL), BF16)] * n_out,
        compiler_params=pltpu.CompilerParams(vmem_limit_bytes=VMEM_LIMIT),
        name="qkv_rope" if rope else "qkv",
    )(*args)


def _rope_tables():
    t = jnp.arange(SEQ)
    row = (t // GRID_W).astype(F32)
    col = (t % GRID_W).astype(F32)
    half = HEAD_DIM // 2
    freqs = 1.0 / (ROPE_THETA ** (jnp.arange(0, half, 2, dtype=F32) / half))
    ar = row[:, None] * freqs
    ac = col[:, None] * freqs
    ang = jnp.concatenate([ar, ar, ac, ac], axis=-1)
    cos, sin = jnp.cos(ang), jnp.sin(ang)
    first = (np.arange(HEAD_DIM) % half) < half // 2
    sa = jnp.where(first, -sin, 0.0)
    sb = jnp.where(first, 0.0, sin)
    rep = LANES // HEAD_DIM
    return tuple(jnp.tile(x, (1, rep)) for x in (cos, sa, sb))


def _stack_halves(q):
    lane = lax.broadcasted_iota(jnp.int32, q.shape, 1)
    zero = jnp.zeros_like(q)
    return jnp.concatenate([jnp.where(lane < HEAD_DIM, q, zero),
                            jnp.where(lane >= HEAD_DIM, q, zero)], axis=0)


def _with_ones(v):
    return jnp.concatenate([v, jnp.ones(v.shape, v.dtype)], axis=1)


def _diff_attn_kernel(*refs, nseg, n_cast, lam_init, block_rows, heads):
    lam_ref, subln_ref, q_ref = refs[:3]
    k_refs = refs[3:3 + nseg]
    v_refs = refs[3 + nseg:3 + 2 * nseg]
    n_in = 3 + 2 * nseg + n_cast
    o_ref = refs[n_in]
    for src_ref, dst_ref in zip(refs[n_in - n_cast:n_in], refs[n_in + 1:]):
        dst_ref[...] = src_ref[...].astype(BF16)

    lp = lam_ref[...]
    lam = (jnp.exp(jnp.sum(lp[0:1] * lp[1:2], axis=-1, keepdims=True))
           - jnp.exp(jnp.sum(lp[2:3] * lp[3:4], axis=-1, keepdims=True)) + lam_init)
    subln = subln_ref[...] * (1.0 - lam_init)

    v_aug = [[_with_ones(v_ref[0, :, head * LANES:(head + 1) * LANES]) for v_ref in v_refs]
             for head in range(heads)]

    def block(head, row0, tq):
        cols = slice(head * LANES, (head + 1) * LANES)
        qq = _stack_halves(q_ref[0, row0:row0 + tq, cols])
        s_parts = [_dot_nt(qq, k_ref[0, :, cols]) for k_ref in k_refs]
        m = s_parts[0].max(axis=-1, keepdims=True)
        for s in s_parts[1:]:
            m = jnp.maximum(m, s.max(axis=-1, keepdims=True))
        acc = None
        for s, v in zip(s_parts, v_aug[head]):
            pv = _dot(jnp.exp2(s - m).astype(BF16), v)
            acc = pv if acc is None else acc + pv
        w = 1.0 / acc[:, LANES:]
        o = acc[:tq, :LANES] * w[:tq] - acc[tq:, :LANES] * (w[tq:] * lam)
        on = o * lax.rsqrt(jnp.mean(o * o, axis=-1, keepdims=True) + EPS) * subln
        o_ref[0, row0:row0 + tq, cols] = on.astype(BF16)

    for head in range(heads):
        row0 = 0
        for tq in block_rows[head]:
            block(head, row0, tq)
            row0 += tq


def _diff_attn(lam_params, subln, q, ks, vs, *, block_rows, heads, lam_init, to_cast=()):
    B, N, _ = q.shape
    nseg = len(ks)
    assert len(block_rows) == heads and all(sum(rows) == N for rows in block_rows)
    n_h = N_LANE_TILES // heads
    steps = B * n_h

    def head_spec(n):
        return pl.BlockSpec((1, n, heads * LANES), lambda b, h: (b, 0, h))

    def cast_spec(w):
        rows, cols = w.shape
        share = 1 if rows % (steps * BF16_ROWS) == 0 else 2
        assert rows % (steps // share * BF16_ROWS) == 0
        return pl.BlockSpec((rows * share // steps, cols), lambda b, h: ((b * n_h + h) // share, 0))

    kv_specs = [head_spec(k.shape[1]) for k in ks]
    cast_specs = [cast_spec(w) for w in to_cast]
    out = pl.pallas_call(
        functools.partial(_diff_attn_kernel, nseg=nseg, n_cast=len(to_cast), lam_init=lam_init,
                          block_rows=block_rows, heads=heads),
        grid=(B, n_h),
        in_specs=[
            pl.BlockSpec((4, HEAD_DIM), lambda b, h: (0, 0)),
            pl.BlockSpec((1, LANES), lambda b, h: (0, 0)),
            head_spec(N),
        ] + kv_specs + kv_specs + cast_specs,
        out_specs=[head_spec(N)] + cast_specs,
        out_shape=[jax.ShapeDtypeStruct((B, N, D_MODEL), BF16)]
        + [jax.ShapeDtypeStruct(w.shape, BF16) for w in to_cast],
        compiler_params=pltpu.CompilerParams(vmem_limit_bytes=VMEM_LIMIT),
        name="diff_attn_%dseg" % nseg,
    )(lam_params, subln, q, *ks, *vs, *to_cast)
    return out[0], out[1:]


def _na_window_start(j):
    return min(max(j - 1, 0), NA_BLOCKS - NA_KBLK)


def _na_bias_type(j):
    return 0 if j == 0 else (2 if j == NA_BLOCKS - 1 else 1)


def _na_block_layout():
    rows = SEQ // GRID_W
    per_block = []
    for j in range(NA_BLOCKS):
        blk = []
        for a in range(NA_Q_ROWS):
            r = NA_Q_ROWS * j + a
            rs = min(max(r - NA_WIN_ROWS // 2, 0), rows - NA_WIN_ROWS)
            row = []
            for u in range(NA_K_ROWS // 2):
                kr = NA_Q_ROWS * _na_window_start(j) + 2 * u
                ok = tuple(rs <= k < rs + NA_WIN_ROWS for k in (kr, kr + 1))
                row.append((kr - r + NA_WIN_ROWS - 1,) + ok)
            blk.append(tuple(row))
        per_block.append(tuple(blk))
    assert all(per_block[j] == per_block[1] for j in range(2, NA_BLOCKS - 1))
    return (per_block[0], per_block[1], per_block[NA_BLOCKS - 1])


NA_LAYOUT = _na_block_layout()
assert all(any(lo or ro for _, lo, ro in row[NA_Q_ROWS // 2:]) for kind in NA_LAYOUT for row in kind)
NA_N_DR = 2 * NA_WIN_ROWS - 1
NA_N_PAIR = NA_N_DR + 1


def _na_pair_table(rpb):
    n_dc = 2 * NA_WIN_COLS - 1
    qc = np.arange(GRID_W)
    cs = np.clip(qc - NA_WIN_COLS // 2, 0, GRID_W - NA_WIN_COLS)
    col_ok = (qc[None, :] >= cs[:, None]) & (qc[None, :] < cs[:, None] + NA_WIN_COLS)
    dc = np.clip(qc[None, :] - qc[:, None] + NA_WIN_COLS - 1, 0, n_dc - 1)
    pick_d = np.zeros((NA_N_DR, NA_N_PAIR, 2), np.float32)
    for i in range(NA_N_PAIR):
        for p in range(2):
            if 0 <= i - 1 + p < NA_N_DR:
                pick_d[i - 1 + p, i, p] = 1.0
    pick_c = np.zeros((2, n_dc, GRID_W, 2, GRID_W), np.float32)
    for p in range(2):
        pick_c[p, dc, qc[:, None], p, qc[None, :]] = 1.0
    lhs = jnp.einsum('hdj,dip->hipj', rpb, jnp.asarray(pick_d), precision=lax.Precision.HIGHEST)
    pairs = jnp.einsum('hic,cx->hix', lhs.reshape(-1, NA_N_PAIR, 2 * n_dc),
                       jnp.asarray(pick_c.reshape(2 * n_dc, GRID_W * LANES)),
                       precision=lax.Precision.HIGHEST).reshape(-1, NA_N_PAIR, GRID_W, LANES)
    valid = pick_d.sum(0)[:, None, :, None] * col_ok[None, :, None, :]
    pairs = jnp.where(jnp.asarray(valid.reshape(NA_N_PAIR, GRID_W, LANES) > 0), pairs * LOG2E, -jnp.inf)
    return pairs.reshape(N_LANE_TILES, 2, NA_N_PAIR, GRID_W, LANES)


def _na_fill_bias(pair_ref, bias_scr):
    lane = lax.broadcasted_iota(jnp.int32, (GRID_W, LANES), 1)
    neg = jnp.full((GRID_W, LANES), -jnp.inf, F32)
    for kind, layout in enumerate(NA_LAYOUT):
        for head in range(2):
            for a, row in enumerate(layout):
                r0 = head * NA_TQ + a * GRID_W
                for u, (d, left_ok, right_ok) in enumerate(row):
                    slab = neg
                    if left_ok or right_ok:
                        assert not left_ok or 0 <= d < NA_N_DR
                        assert not right_ok or 0 <= d + 1 < NA_N_DR
                        slab = pair_ref[0, head, d + 1]
                        if not left_ok:
                            slab = jnp.where(lane < HEAD_DIM, neg, slab)
                        if not right_ok:
                            slab = jnp.where(lane < HEAD_DIM, slab, neg)
                    bias_scr[kind, r0:r0 + GRID_W, u * LANES:(u + 1) * LANES] = slab


def _nbr_attn_kernel(q_ref, k_ref, v_ref, kc_ref, vc_ref, pair_ref, o_ref, bias_scr):
    n_win = NA_K_ROWS * GRID_W
    lane = lax.broadcasted_iota(jnp.int32, (NA_TQ, LANES), 1)

    @pl.when(pl.program_id(1) == 0)
    def _():
        _na_fill_bias(pair_ref, bias_scr)

    for bb in range(q_ref.shape[0]):
        v_aug = _with_ones(v_ref[bb])
        vc_aug = _with_ones(vc_ref[bb])
        for j in range(NA_BLOCKS):
            rows = slice(j * NA_TQ, (j + 1) * NA_TQ)
            win0 = _na_window_start(j) * NA_TQ
            win = slice(win0, win0 + n_win)
            qq = _stack_halves(q_ref[bb, rows, :])
            bias = bias_scr[_na_bias_type(j)]
            t1 = win0 + NA_TQ
            k_a = jnp.concatenate([kc_ref[bb], k_ref[bb, win0:t1, :]], axis=0)
            v_a = jnp.concatenate([vc_aug, v_aug[win0:t1]], axis=0)
            s_a = _dot_nt(qq, k_a)
            s_a = jnp.concatenate([s_a[:, :CTX_LEN], s_a[:, CTX_LEN:] + bias[:, :NA_TQ]], axis=1)
            m_a = s_a.max(axis=-1, keepdims=True)
            acc_a = _dot(jnp.exp2(s_a - m_a).astype(BF16), v_a)
            s_b = _dot_nt(qq, k_ref[bb, t1:win0 + n_win, :]) + bias[:, NA_TQ:]
            m_b = s_b.max(axis=-1, keepdims=True)
            acc_b = _dot(jnp.exp2(s_b - m_b).astype(BF16), v_aug[t1:win0 + n_win])
            m = jnp.maximum(m_a, m_b)
            acc = acc_a * jnp.exp2(m_a - m) + acc_b * jnp.exp2(m_b - m)
            o2 = acc[:, :LANES] / acc[:, LANES:]
            o_ref[bb, rows, :] = jnp.where(lane < HEAD_DIM, o2[:NA_TQ], o2[NA_TQ:]).astype(BF16)


def _nbr_attn(q, k, v, kc, vc, pair_tab):
    B, N, _ = q.shape

    def head_spec(n):
        return pl.BlockSpec((NA_BATCH_PER_STEP, n, LANES), lambda hp, b: (b, 0, hp))

    pair_spec = pl.BlockSpec((1, 2, NA_N_PAIR, GRID_W, LANES), lambda hp, b: (hp, 0, 0, 0, 0))
    return pl.pallas_call(
        _nbr_attn_kernel,
        grid=(N_LANE_TILES, B // NA_BATCH_PER_STEP),
        in_specs=[head_spec(N), head_spec(N), head_spec(N), head_spec(CTX_LEN), head_spec(CTX_LEN),
                  pair_spec],
        out_specs=head_spec(N),
        out_shape=jax.ShapeDtypeStruct(q.shape, BF16),
        scratch_shapes=[pltpu.VMEM((3, 2 * NA_TQ, NA_K_ROWS * GRID_W), F32)],
        compiler_params=pltpu.CompilerParams(vmem_limit_bytes=VMEM_LIMIT,
                                             dimension_semantics=("arbitrary", "arbitrary")),
        name="nbr_attn",
    )(q, k, v, kc, vc, pair_tab)


FFN_CHUNKS = ((0, 1024), (1024, 1024), (2048, 768))

def _out_ffn_kernel(*refs, final):
    o_ref, h_ref, mod_ref, g_ref, wo_ref, wgu_ref, wd_ref = refs[:7]
    if final:
        gfin_ref, out_ref = refs[7:]
    else:
        (out_ref,) = refs[7:]
    h1 = h_ref[0] + mod_ref[0, 2:3, :] * _dot(o_ref[0], wo_ref[...])
    xn = _rmsnorm_mod(h1, g_ref[...], mod_ref[0, 3:4, :], mod_ref[0, 4:5, :]).astype(BF16)
    acc = None
    for start, size in FFN_CHUNKS:
        gate = _dot(xn, wgu_ref[:, start:start + size])
        up = _dot(xn, wgu_ref[:, D_FF + start:D_FF + start + size])
        act = (gate * jax.nn.sigmoid(gate) * up).astype(BF16)
        part = _dot(act, wd_ref[start:start + size, :])
        acc = part if acc is None else acc + part
    h2 = h1 + mod_ref[0, 5:6, :] * acc
    if final:
        h2 = h2 * lax.rsqrt(jnp.mean(h2 * h2, axis=-1, keepdims=True) + EPS) * gfin_ref[...]
    out_ref[0] = h2


def _out_ffn(o, h, mod, g, wo, wgu, wd, layer, gfin, *, tm):
    G, R, _ = h.shape
    final = gfin is not None
    row_spec = pl.BlockSpec((1, tm, D_MODEL), lambda b, t: (b, t, 0))

    def resident(shape, index):
        return pl.BlockSpec(shape, lambda b, t: index, pipeline_mode=pl.Buffered(1))

    in_specs = [
        row_spec,
        row_spec,
        pl.BlockSpec((1, 6, D_MODEL), lambda b, t: (b, 0, 0)),
        pl.BlockSpec((1, D_MODEL), lambda b, t: (0, 0)),
        resident((D_MODEL, D_MODEL), (0, 0)),
        resident((None, D_MODEL, 2 * D_FF), (layer, 0, 0)),
        resident((None, D_FF, D_MODEL), (layer, 0, 0)),
    ]
    args = [o, h, mod, g, wo, wgu, wd]
    if final:
        in_specs.append(pl.BlockSpec((1, D_MODEL), lambda b, t: (0, 0)))
        args.append(gfin)
    return pl.pallas_call(
        functools.partial(_out_ffn_kernel, final=final),
        grid=(G, R // tm),
        in_specs=in_specs,
        out_specs=row_spec,
        out_shape=jax.ShapeDtypeStruct(h.shape, F32),
        compiler_params=pltpu.CompilerParams(vmem_limit_bytes=VMEM_LIMIT),
        name="out_ffn_final" if final else "out_ffn",
    )(*args)


def kernel(x, c, ctx, c_ctx, ada_w, ada_b, norm_mix, norm_ffn, da_wqkv, da_lambda_q1, da_lambda_k1,
           da_lambda_q2, da_lambda_k2, da_subln, da_wo, na_wqkv, na_rpb, na_wo, ffn_w_gate_up,
           ffn_w_down, norm_final):
    B, N, D = x.shape
    L = ctx.shape[1]
    TM = 1024
    assert (B, N, D, L) == (BATCH, SEQ, D_MODEL, CTX_LEN) and ctx.shape == (B, L, D)
    assert ada_w.shape == (DEPTH, D, 6 * D) and ffn_w_gate_up.shape == (DEPTH, D, 2 * D_FF)
    assert da_wqkv.shape == na_wqkv.shape == (1, D, 3 * D)
    assert na_rpb.shape == (1, D // HEAD_DIM, 2 * NA_WIN_ROWS - 1, 2 * NA_WIN_COLS - 1)

    cond = jnp.zeros((COND_ROWS, D), F32).at[:B].set(c).at[CTX_MOD_ROW].set(c_ctx)
    mods = _adaln(cond, ada_w, ada_b).reshape(DEPTH, COND_ROWS, 6, D)

    h = x
    hc = ctx.reshape(1, B * L, D)
    rope_tabs = _rope_tables()

    mod_l = mods[0, :B]
    mod_c = mods[0, CTX_MOD_ROW:CTX_MOD_ROW + 1]
    g_mix = norm_mix[0].reshape(1, D)
    g_ffn = norm_ffn[0].reshape(1, D)
    wqkv = da_wqkv[0].astype(BF16)
    q_l, k_l, v_l = _qkv(h, mod_l, g_mix, wqkv, rope_tabs, tm=TM)
    q_c, k_c, v_c = (t.reshape(B, L, D) for t in _qkv(hc, mod_c, g_mix, wqkv, None, tm=TM))
    lam_params = jnp.stack([da_lambda_q1[0], da_lambda_k1[0], da_lambda_q2[0], da_lambda_k2[0]])
    subln = da_subln[0].reshape(1, LANES)
    lam_init = 0.8 - 0.6 * float(np.exp(-0.3 * 0))
    later_weights = [ffn_w_gate_up.reshape(DEPTH * D, 2 * D_FF), ffn_w_down.reshape(DEPTH * D_FF, D),
                     da_wo[0], na_wqkv[0], na_wo[0]]
    o_l, (wgu, wd, wo, na_wqkv_bf, na_wo_bf) = _diff_attn(
        lam_params, subln, q_l, [k_c, k_l], [v_c, v_l], block_rows=DA_BLOCK_ROWS,
        heads=DA_HEADS_PER_STEP, lam_init=lam_init, to_cast=later_weights)
    wgu = wgu.reshape(DEPTH, D, 2 * D_FF)
    wd = wd.reshape(DEPTH, D_FF, D)
    o_c, _ = _diff_attn(lam_params, subln, q_c, [k_c], [v_c], block_rows=((L,),) * N_LANE_TILES,
                        heads=N_LANE_TILES, lam_init=lam_init)
    h = _out_ffn(o_l, h, mod_l, g_ffn, wo, wgu, wd, 0, None, tm=TM)
    hc = _out_ffn(o_c.reshape(1, B * L, D), hc, mod_c, g_ffn, wo, wgu, wd, 0, None, tm=TM)

    mod_l = mods[1, :B]
    mod_c = mods[1, CTX_MOD_ROW:CTX_MOD_ROW + 1]
    g_mix = norm_mix[1].reshape(1, D)
    g_ffn = norm_ffn[1].reshape(1, D)
    q_l, k_l, v_l = _qkv(h, mod_l, g_mix, na_wqkv_bf, None, tm=TM)
    k_c, v_c = (t.reshape(B, L, D)
                for t in _qkv(hc, mod_c, g_mix, na_wqkv_bf, None, tm=TM, want_q=False))
    o_l = _nbr_attn(q_l, k_l, v_l, k_c, v_c, _na_pair_table(na_rpb[0]))
    return _out_ffn(o_l, h, mod_l, g_ffn, na_wo_bf, wgu, wd, 1, norm_final.reshape(1, D), tm=TM)
```

```python
import functools

import numpy as np
import jax
import jax.numpy as jnp
from jax import lax
from jax.experimental import pallas as pl
from jax.experimental.pallas import tpu as pltpu

D_MODEL = 1024
BATCH = 16
SEQ = 2048
DEPTH = 2
GRID_W = 64
CTX_LEN = 256
HEAD_DIM = 64
NA_WIN_ROWS = 8
NA_WIN_COLS = 16
D_FF = 2816
ROPE_THETA = 10000.0
EPS = 1e-6
LOG2E = 1.4426950408889634
Q_SCALE = HEAD_DIM ** -0.5 * LOG2E

LANES = 128
BF16_ROWS = 16
N_LANE_TILES = D_MODEL // LANES
COND_ROWS = 32
CTX_MOD_ROW = BATCH
VMEM_LIMIT = 56 * 1024 * 1024

NA_Q_ROWS = 4
NA_K_ROWS = 12
NA_TQ = NA_Q_ROWS * GRID_W
NA_BLOCKS = SEQ // NA_TQ
NA_KBLK = NA_K_ROWS // NA_Q_ROWS
NA_BATCH_PER_STEP = 4
DA_HEADS_PER_STEP = 2
DA_BLOCK_ROWS = ((128,) + (256,) * 7 + (128,),) * DA_HEADS_PER_STEP
F32 = jnp.float32
BF16 = jnp.bfloat16


def _dot(a, b):
    return jnp.dot(a, b, preferred_element_type=F32)


def _dot_nt(a, b):
    return lax.dot_general(a, b, (((1,), (1,)), ((), ())), preferred_element_type=F32)


def _adaln_kernel(cond_ref, w_ref, b_ref, out_ref):
    cnd = cond_ref[...]
    a = cnd * jax.nn.sigmoid(cnd)
    w = w_ref[0]
    a_hi = a.astype(BF16)
    a_lo = (a - a_hi.astype(F32)).astype(BF16)
    w_hi = w.astype(BF16)
    w_lo = (w - w_hi.astype(F32)).astype(BF16)
    p_hi = _dot(jnp.concatenate([a_hi, a_lo], axis=0), w_hi)
    acc = p_hi[:COND_ROWS] + p_hi[COND_ROWS:] + _dot(a_hi, w_lo)
    out_ref[0] = acc + b_ref[0]


def _adaln(cond, ada_w, ada_b):
    n_out = ada_w.shape[-1]
    tn = 1024
    return pl.pallas_call(
        _adaln_kernel,
        grid=(DEPTH, n_out // tn),
        in_specs=[
            pl.BlockSpec((COND_ROWS, D_MODEL), lambda i, j: (0, 0)),
            pl.BlockSpec((1, D_MODEL, tn), lambda i, j: (i, 0, j)),
            pl.BlockSpec((1, 1, tn), lambda i, j: (i, 0, j)),
        ],
        out_specs=pl.BlockSpec((1, COND_ROWS, tn), lambda i, j: (i, 0, j)),
        out_shape=jax.ShapeDtypeStruct((DEPTH, COND_ROWS, n_out), F32),
        compiler_params=pltpu.CompilerParams(vmem_limit_bytes=VMEM_LIMIT),
        name="adaln",
    )(cond, ada_w, ada_b.reshape(DEPTH, 1, n_out))


def _rmsnorm_mod(h, g, shift, scale):
    y = h * lax.rsqrt(jnp.mean(h * h, axis=-1, keepdims=True) + EPS) * g
    return y * (1.0 + scale) + shift


def _qkv_kernel(*refs, rope, want_q):
    if rope:
        h_ref, mod_ref, g_ref, w_ref, cos_ref, sa_ref, sb_ref = refs[:7]
        outs = refs[7:]
    else:
        h_ref, mod_ref, g_ref, w_ref = refs[:4]
        outs = refs[4:]
    xn = _rmsnorm_mod(h_ref[0], g_ref[...], mod_ref[0, 0:1, :], mod_ref[0, 1:2, :]).astype(BF16)

    def rotate(t):
        return (t * cos_ref[...] + pltpu.roll(t, LANES - 16, axis=1) * sa_ref[...]
                + pltpu.roll(t, 16, axis=1) * sb_ref[...])

    parts = ("q", "k", "v") if want_q else ("k", "v")
    for name, o_ref in zip(parts, outs):
        col = {"q": 0, "k": D_MODEL, "v": 2 * D_MODEL}[name]
        acc = _dot(xn, w_ref[:, col:col + D_MODEL])
        if name == "q":
            acc = acc * Q_SCALE
        if rope and name != "v":
            for c in range(N_LANE_TILES):
                sl = slice(c * LANES, (c + 1) * LANES)
                o_ref[0, :, sl] = rotate(acc[:, sl]).astype(BF16)
        else:
            o_ref[0] = acc.astype(BF16)


def _qkv(h, mod, g, w, rope_tabs, *, tm, want_q=True):
    G, R, _ = h.shape
    rope = rope_tabs is not None
    row_spec = pl.BlockSpec((1, tm, D_MODEL), lambda b, t: (b, t, 0))
    in_specs = [
        row_spec,
        pl.BlockSpec((1, 6, D_MODEL), lambda b, t: (b, 0, 0)),
        pl.BlockSpec((1, D_MODEL), lambda b, t: (0, 0)),
        pl.BlockSpec((D_MODEL, 3 * D_MODEL), lambda b, t: (0, 0)),
    ]
    args = [h, mod, g, w]
    if rope:
        in_specs += [pl.BlockSpec((tm, LANES), lambda b, t: (t, 0))] * 3
        args += list(rope_tabs)
    n_out = 3 if want_q else 2
    return pl.pallas_call(
        functools.partial(_qkv_kernel, rope=rope, want_q=want_q),
        grid=(G, R // tm),
        in_specs=in_specs,
        out_specs=[row_spec] * n_out,
        out_shape=[jax.ShapeDtypeStruct((G, R, D_MODEL), BF16)] * n_out,
        compiler_params=pltpu.CompilerParams(vmem_limit_bytes=VMEM_LIMIT),
        name="qkv_rope" if rope else "qkv",
    )(*args)


def _rope_tables():
    t = jnp.arange(SEQ)
    row = (t // GRID_W).astype(F32)
    col = (t % GRID_W).astype(F32)
    half = HEAD_DIM // 2
    freqs = 1.0 / (ROPE_THETA ** (jnp.arange(0, half, 2, dtype=F32) / half))
    ar = row[:, None] * freqs
    ac = col[:, None] * freqs
    ang = jnp.concatenate([ar, ar, ac, ac], axis=-1)
    cos, sin = jnp.cos(ang), jnp.sin(ang)
    first = (np.arange(HEAD_DIM) % half) < half // 2
    sa = jnp.where(first, -sin, 0.0)
    sb = jnp.where(first, 0.0, sin)
    rep = LANES // HEAD_DIM
    return tuple(jnp.tile(x, (1, rep)) for x in (cos, sa, sb))


def _stack_halves(q):
    lane = lax.broadcasted_iota(jnp.int32, q.shape, 1)
    zero = jnp.zeros_like(q)
    return jnp.concatenate([jnp.where(lane < HEAD_DIM, q, zero),
                            jnp.where(lane >= HEAD_DIM, q, zero)], axis=0)


def _with_ones(v):
    return jnp.concatenate([v, jnp.ones(v.shape, v.dtype)], axis=1)


def _diff_attn_kernel(*refs, nseg, n_cast, lam_init, block_rows, heads):
    lam_ref, subln_ref, q_ref = refs[:3]
    k_refs = refs[3:3 + nseg]
    v_refs = refs[3 + nseg:3 + 2 * nseg]
    n_in = 3 + 2 * nseg + n_cast
    o_ref = refs[n_in]
    for src_ref, dst_ref in zip(refs[n_in - n_cast:n_in], refs[n_in + 1:]):
        dst_ref[...] = src_ref[...].astype(BF16)

    lp = lam_ref[...]
    lam = (jnp.exp(jnp.sum(lp[0:1] * lp[1:2], axis=-1, keepdims=True))
           - jnp.exp(jnp.sum(lp[2:3] * lp[3:4], axis=-1, keepdims=True)) + lam_init)
    subln = subln_ref[...] * (1.0 - lam_init)

    v_aug = [[_with_ones(v_ref[0, :, head * LANES:(head + 1) * LANES]) for v_ref in v_refs]
             for head in range(heads)]

    def block(head, row0, tq):
        cols = slice(head * LANES, (head + 1) * LANES)
        qq = _stack_halves(q_ref[0, row0:row0 + tq, cols])
        s_parts = [_dot_nt(qq, k_ref[0, :, cols]) for k_ref in k_refs]
        m = s_parts[0].max(axis=-1, keepdims=True)
        for s in s_parts[1:]:
            m = jnp.maximum(m, s.max(axis=-1, keepdims=True))
        acc = None
        for s, v in zip(s_parts, v_aug[head]):
            pv = _dot(jnp.exp2(s - m).astype(BF16), v)
            acc = pv if acc is None else acc + pv
        w = 1.0 / acc[:, LANES:]
        o = acc[:tq, :LANES] * w[:tq] - acc[tq:, :LANES] * (w[tq:] * lam)
        on = o * lax.rsqrt(jnp.mean(o * o, axis=-1, keepdims=True) + EPS) * subln
        o_ref[0, row0:row0 + tq, cols] = on.astype(BF16)

    for head in range(heads):
        row0 = 0
        for tq in block_rows[head]:
            block(head, row0, tq)
            row0 += tq


def _diff_attn(lam_params, subln, q, ks, vs, *, block_rows, heads, lam_init, to_cast=()):
    B, N, _ = q.shape
    nseg = len(ks)
    assert len(block_rows) == heads and all(sum(rows) == N for rows in block_rows)
    n_h = N_LANE_TILES // heads
    steps = B * n_h

    def head_spec(n):
        return pl.BlockSpec((1, n, heads * LANES), lambda b, h: (b, 0, h))

    def cast_spec(w):
        rows, cols = w.shape
        share = 1 if rows % (steps * BF16_ROWS) == 0 else 2
        assert rows % (steps // share * BF16_ROWS) == 0
        return pl.BlockSpec((rows * share // steps, cols), lambda b, h: ((b * n_h + h) // share, 0))

    kv_specs = [head_spec(k.shape[1]) for k in ks]
    cast_specs = [cast_spec(w) for w in to_cast]
    out = pl.pallas_call(
        functools.partial(_diff_attn_kernel, nseg=nseg, n_cast=len(to_cast), lam_init=lam_init,
                          block_rows=block_rows, heads=heads),
        grid=(B, n_h),
        in_specs=[
            pl.BlockSpec((4, HEAD_DIM), lambda b, h: (0, 0)),
            pl.BlockSpec((1, LANES), lambda b, h: (0, 0)),
            head_spec(N),
        ] + kv_specs + kv_specs + cast_specs,
        out_specs=[head_spec(N)] + cast_specs,
        out_shape=[jax.ShapeDtypeStruct((B, N, D_MODEL), BF16)]
        + [jax.ShapeDtypeStruct(w.shape, BF16) for w in to_cast],
        compiler_params=pltpu.CompilerParams(vmem_limit_bytes=VMEM_LIMIT),
        name="diff_attn_%dseg" % nseg,
    )(lam_params, subln, q, *ks, *vs, *to_cast)
    return out[0], out[1:]


def _na_window_start(j):
    return min(max(j - 1, 0), NA_BLOCKS - NA_KBLK)


def _na_bias_type(j):
    return 0 if j == 0 else (2 if j == NA_BLOCKS - 1 else 1)


def _na_block_layout():
    rows = SEQ // GRID_W
    per_block = []
    for j in range(NA_BLOCKS):
        blk = []
        for a in range(NA_Q_ROWS):
            r = NA_Q_ROWS * j + a
            rs = min(max(r - NA_WIN_ROWS // 2, 0), rows - NA_WIN_ROWS)
            row = []
            for u in range(NA_K_ROWS // 2):
                kr = NA_Q_ROWS * _na_window_start(j) + 2 * u
                ok = tuple(rs <= k < rs + NA_WIN_ROWS for k in (kr, kr + 1))
                row.append((kr - r + NA_WIN_ROWS - 1,) + ok)
            blk.append(tuple(row))
        per_block.append(tuple(blk))
    assert all(per_block[j] == per_block[1] for j in range(2, NA_BLOCKS - 1))
    return (per_block[0], per_block[1], per_block[NA_BLOCKS - 1])


NA_LAYOUT = _na_block_layout()
assert all(any(lo or ro for _, lo, ro in row[NA_Q_ROWS // 2:]) for kind in NA_LAYOUT for row in kind)
NA_N_DR = 2 * NA_WIN_ROWS - 1
NA_N_PAIR = NA_N_DR + 1


def _na_pair_table(rpb):
    n_dc = 2 * NA_WIN_COLS - 1
    qc = np.arange(GRID_W)
    cs = np.clip(qc - NA_WIN_COLS // 2, 0, GRID_W - NA_WIN_COLS)
    col_ok = (qc[None, :] >= cs[:, None]) & (qc[None, :] < cs[:, None] + NA_WIN_COLS)
    dc = np.clip(qc[None, :] - qc[:, None] + NA_WIN_COLS - 1, 0, n_dc - 1)
    pick_d = np.zeros((NA_N_DR, NA_N_PAIR, 2), np.float32)
    for i in range(NA_N_PAIR):
        for p in range(2):
            if 0 <= i - 1 + p < NA_N_DR:
                pick_d[i - 1 + p, i, p] = 1.0
    pick_c = np.zeros((2, n_dc, GRID_W, 2, GRID_W), np.float32)
    for p in range(2):
        pick_c[p, dc, qc[:, None], p, qc[None, :]] = 1.0
    lhs = jnp.einsum('hdj,dip->hipj', rpb, jnp.asarray(pick_d), precision=lax.Precision.HIGHEST)
    pairs = jnp.einsum('hic,cx->hix', lhs.reshape(-1, NA_N_PAIR, 2 * n_dc),
                       jnp.asarray(pick_c.reshape(2 * n_dc, GRID_W * LANES)),
                       precision=lax.Precision.HIGHEST).reshape(-1, NA_N_PAIR, GRID_W, LANES)
    valid = pick_d.sum(0)[:, None, :, None] * col_ok[None, :, None, :]
    pairs = jnp.where(jnp.asarray(valid.reshape(NA_N_PAIR, GRID_W, LANES) > 0), pairs * LOG2E, -jnp.inf)
    return pairs.reshape(N_LANE_TILES, 2, NA_N_PAIR, GRID_W, LANES)


def _na_fill_bias(pair_ref, bias_scr):
    lane = lax.broadcasted_iota(jnp.int32, (GRID_W, LANES), 1)
    neg = jnp.full((GRID_W, LANES), -jnp.inf, F32)
    for kind, layout in enumerate(NA_LAYOUT):
        for head in range(2):
            for a, row in enumerate(layout):
                r0 = head * NA_TQ + a * GRID_W
                for u, (d, left_ok, right_ok) in enumerate(row):
                    slab = neg
                    if left_ok or right_ok:
                        assert not left_ok or 0 <= d < NA_N_DR
                        assert not right_ok or 0 <= d + 1 < NA_N_DR
                        slab = pair_ref[0, head, d + 1]
                        if not left_ok:
                            slab = jnp.where(lane < HEAD_DIM, neg, slab)
                        if not right_ok:
                            slab = jnp.where(lane < HEAD_DIM, slab, neg)
                    bias_scr[kind, r0:r0 + GRID_W, u * LANES:(u + 1) * LANES] = slab


def _nbr_attn_kernel(q_ref, k_ref, v_ref, kc_ref, vc_ref, pair_ref, o_ref, bias_scr):
    n_win = NA_K_ROWS * GRID_W
    lane = lax.broadcasted_iota(jnp.int32, (NA_TQ, LANES), 1)

    @pl.when(pl.program_id(1) == 0)
    def _():
        _na_fill_bias(pair_ref, bias_scr)

    for bb in range(q_ref.shape[0]):
        v_aug = _with_ones(v_ref[bb])
        vc_aug = _with_ones(vc_ref[bb])
        for j in range(NA_BLOCKS):
            rows = slice(j * NA_TQ, (j + 1) * NA_TQ)
            win0 = _na_window_start(j) * NA_TQ
            win = slice(win0, win0 + n_win)
            qq = _stack_halves(q_ref[bb, rows, :])
            bias = bias_scr[_na_bias_type(j)]
            t1 = win0 + NA_TQ
            k_a = jnp.concatenate([kc_ref[bb], k_ref[bb, win0:t1, :]], axis=0)
            v_a = jnp.concatenate([vc_aug, v_aug[win0:t1]], axis=0)
            s_a = _dot_nt(qq, k_a)
            s_a = jnp.concatenate([s_a[:, :CTX_LEN], s_a[:, CTX_LEN:] + bias[:, :NA_TQ]], axis=1)
            m_a = s_a.max(axis=-1, keepdims=True)
            acc_a = _dot(jnp.exp2(s_a - m_a).astype(BF16), v_a)
            s_b = _dot_nt(qq, k_ref[bb, t1:win0 + n_win, :]) + bias[:, NA_TQ:]
            m_b = s_b.max(axis=-1, keepdims=True)
            acc_b = _dot(jnp.exp2(s_b - m_b).astype(BF16), v_aug[t1:win0 + n_win])
            m = jnp.maximum(m_a, m_b)
            acc = acc_a * jnp.exp2(m_a - m) + acc_b * jnp.exp2(m_b - m)
            o2 = acc[:, :LANES] / acc[:, LANES:]
            o_ref[bb, rows, :] = jnp.where(lane < HEAD_DIM, o2[:NA_TQ], o2[NA_TQ:]).astype(BF16)


def _nbr_attn(q, k, v, kc, vc, pair_tab):
    B, N, _ = q.shape

    def head_spec(n):
        return pl.BlockSpec((NA_BATCH_PER_STEP, n, LANES), lambda hp, b: (b, 0, hp))

    pair_spec = pl.BlockSpec((1, 2, NA_N_PAIR, GRID_W, LANES), lambda hp, b: (hp, 0, 0, 0, 0))
    return pl.pallas_call(
        _nbr_attn_kernel,
        grid=(N_LANE_TILES, B // NA_BATCH_PER_STEP),
        in_specs=[head_spec(N), head_spec(N), head_spec(N), head_spec(CTX_LEN), head_spec(CTX_LEN),
                  pair_spec],
        out_specs=head_spec(N),
        out_shape=jax.ShapeDtypeStruct(q.shape, BF16),
        scratch_shapes=[pltpu.VMEM((3, 2 * NA_TQ, NA_K_ROWS * GRID_W), F32)],
        compiler_params=pltpu.CompilerParams(vmem_limit_bytes=VMEM_LIMIT,
                                             dimension_semantics=("arbitrary", "arbitrary")),
        name="nbr_attn",
    )(q, k, v, kc, vc, pair_tab)


FFN_CHUNKS = ((0, 1536), (1536, 1280))

def _out_ffn_kernel(*refs, final):
    o_ref, h_ref, mod_ref, g_ref, wo_ref, wgu_ref, wd_ref = refs[:7]
    if final:
        gfin_ref, out_ref = refs[7:]
    else:
        (out_ref,) = refs[7:]
    h1 = h_ref[0] + mod_ref[0, 2:3, :] * _dot(o_ref[0], wo_ref[...])
    xn = _rmsnorm_mod(h1, g_ref[...], mod_ref[0, 3:4, :], mod_ref[0, 4:5, :]).astype(BF16)
    acc = None
    for start, size in FFN_CHUNKS:
        gate = _dot(xn, wgu_ref[:, start:start + size])
        up = _dot(xn, wgu_ref[:, D_FF + start:D_FF + start + size])
        act = (gate * jax.nn.sigmoid(gate) * up).astype(BF16)
        part = _dot(act, wd_ref[start:start + size, :])
        acc = part if acc is None else acc + part
    h2 = h1 + mod_ref[0, 5:6, :] * acc
    if final:
        h2 = h2 * lax.rsqrt(jnp.mean(h2 * h2, axis=-1, keepdims=True) + EPS) * gfin_ref[...]
    out_ref[0] = h2


def _out_ffn(o, h, mod, g, wo, wgu, wd, layer, gfin, *, tm):
    G, R, _ = h.shape
    final = gfin is not None
    row_spec = pl.BlockSpec((1, tm, D_MODEL), lambda b, t: (b, t, 0))

    def resident(shape, index):
        return pl.BlockSpec(shape, lambda b, t: index, pipeline_mode=pl.Buffered(1))

    in_specs = [
        row_spec,
        row_spec,
        pl.BlockSpec((1, 6, D_MODEL), lambda b, t: (b, 0, 0)),
        pl.BlockSpec((1, D_MODEL), lambda b, t: (0, 0)),
        resident((D_MODEL, D_MODEL), (0, 0)),
        resident((None, D_MODEL, 2 * D_FF), (layer, 0, 0)),
        resident((None, D_FF, D_MODEL), (layer, 0, 0)),
    ]
    args = [o, h, mod, g, wo, wgu, wd]
    if final:
        in_specs.append(pl.BlockSpec((1, D_MODEL), lambda b, t: (0, 0)))
        args.append(gfin)
    return pl.pallas_call(
        functools.partial(_out_ffn_kernel, final=final),
        grid=(G, R // tm),
        in_specs=in_specs,
        out_specs=row_spec,
        out_shape=jax.ShapeDtypeStruct(h.shape, F32),
        compiler_params=pltpu.CompilerParams(vmem_limit_bytes=VMEM_LIMIT),
        name="out_ffn_final" if final else "out_ffn",
    )(*args)


def kernel(x, c, ctx, c_ctx, ada_w, ada_b, norm_mix, norm_ffn, da_wqkv, da_lambda_q1, da_lambda_k1,
           da_lambda_q2, da_lambda_k2, da_subln, da_wo, na_wqkv, na_rpb, na_wo, ffn_w_gate_up,
           ffn_w_down, norm_final):
    B, N, D = x.shape
    L = ctx.shape[1]
    TM = 1024
    assert (B, N, D, L) == (BATCH, SEQ, D_MODEL, CTX_LEN) and ctx.shape == (B, L, D)
    assert ada_w.shape == (DEPTH, D, 6 * D) and ffn_w_gate_up.shape == (DEPTH, D, 2 * D_FF)
    assert da_wqkv.shape == na_wqkv.shape == (1, D, 3 * D)
    assert na_rpb.shape == (1, D // HEAD_DIM, 2 * NA_WIN_ROWS - 1, 2 * NA_WIN_COLS - 1)

    cond = jnp.zeros((COND_ROWS, D), F32).at[:B].set(c).at[CTX_MOD_ROW].set(c_ctx)
    mods = _adaln(cond, ada_w, ada_b).reshape(DEPTH, COND_ROWS, 6, D)

    h = x
    hc = ctx.reshape(1, B * L, D)
    rope_tabs = _rope_tables()

    mod_l = mods[0, :B]
    mod_c = mods[0, CTX_MOD_ROW:CTX_MOD_ROW + 1]
    g_mix = norm_mix[0].reshape(1, D)
    g_ffn = norm_ffn[0].reshape(1, D)
    wqkv = da_wqkv[0].astype(BF16)
    q_l, k_l, v_l = _qkv(h, mod_l, g_mix, wqkv, rope_tabs, tm=TM)
    q_c, k_c, v_c = (t.reshape(B, L, D) for t in _qkv(hc, mod_c, g_mix, wqkv, None, tm=TM))
    lam_params = jnp.stack([da_lambda_q1[0], da_lambda_k1[0], da_lambda_q2[0], da_lambda_k2[0]])
    subln = da_subln[0].reshape(1, LANES)
    lam_init = 0.8 - 0.6 * float(np.exp(-0.3 * 0))
    later_weights = [ffn_w_gate_up.reshape(DEPTH * D, 2 * D_FF), ffn_w_down.reshape(DEPTH * D_FF, D),
                     da_wo[0], na_wqkv[0], na_wo[0]]
    o_l, (wgu, wd, wo, na_wqkv_bf, na_wo_bf) = _diff_attn(
        lam_params, subln, q_l, [k_c, k_l], [v_c, v_l], block_rows=DA_BLOCK_ROWS,
        heads=DA_HEADS_PER_STEP, lam_init=lam_init, to_cast=later_weights)
    wgu = wgu.reshape(DEPTH, D, 2 * D_FF)
    wd = wd.reshape(DEPTH, D_FF, D)
    o_c, _ = _diff_attn(lam_params, subln, q_c, [k_c], [v_c], block_rows=((L,),) * N_LANE_TILES,
                        heads=N_LANE_TILES, lam_init=lam_init)
    h = _out_ffn(o_l, h, mod_l, g_ffn, wo, wgu, wd, 0, None, tm=TM)
    hc = _out_ffn(o_c.reshape(1, B * L, D), hc, mod_c, g_ffn, wo, wgu, wd, 0, None, tm=TM)

    mod_l = mods[1, :B]
    mod_c = mods[1, CTX_MOD_ROW:CTX_MOD_ROW + 1]
    g_mix = norm_mix[1].reshape(1, D)
    g_ffn = norm_ffn[1].reshape(1, D)
    q_l, k_l, v_l = _qkv(h, mod_l, g_mix, na_wqkv_bf, None, tm=TM)
    k_c, v_c = (t.reshape(B, L, D)
                for t in _qkv(hc, mod_c, g_mix, na_wqkv_bf, None, tm=TM, want_q=False))
    o_l = _nbr_attn(q_l, k_l, v_l, k_c, v_c, _na_pair_table(na_rpb[0]))
    return _out_ffn(o_l, h, mod_l, g_ffn, na_wo_bf, wgu, wd, 1, norm_final.reshape(1, D), tm=TM)
```

```python
import functools

import numpy as np
import jax
import jax.numpy as jnp
from jax import lax
from jax.experimental import pallas as pl
from jax.experimental.pallas import tpu as pltpu

D_MODEL = 1024
BATCH = 16
SEQ = 2048
DEPTH = 2
GRID_W = 64
CTX_LEN = 256
HEAD_DIM = 64
NA_WIN_ROWS = 8
NA_WIN_COLS = 16
D_FF = 2816
ROPE_THETA = 10000.0
EPS = 1e-6
LOG2E = 1.4426950408889634
Q_SCALE = HEAD_DIM ** -0.5 * LOG2E

LANES = 128
BF16_ROWS = 16
N_LANE_TILES = D_MODEL // LANES
COND_ROWS = 32
CTX_MOD_ROW = BATCH
VMEM_LIMIT = 56 * 1024 * 1024

NA_Q_ROWS = 4
NA_K_ROWS = 12
NA_TQ = NA_Q_ROWS * GRID_W
NA_BLOCKS = SEQ // NA_TQ
NA_KBLK = NA_K_ROWS // NA_Q_ROWS
NA_BATCH_PER_STEP = 4
DA_HEADS_PER_STEP = 2
DA_BLOCK_ROWS = ((128,) + (256,) * 7 + (128,),) * DA_HEADS_PER_STEP
F32 = jnp.float32
BF16 = jnp.bfloat16


def _dot(a, b):
    return jnp.dot(a, b, preferred_element_type=F32)


def _dot_nt(a, b):
    return lax.dot_general(a, b, (((1,), (1,)), ((), ())), preferred_element_type=F32)


def _adaln_kernel(cond_ref, w_ref, b_ref, out_ref):
    cnd = cond_ref[...]
    a = cnd * jax.nn.sigmoid(cnd)
    w = w_ref[0]
    a_hi = a.astype(BF16)
    a_lo = (a - a_hi.astype(F32)).astype(BF16)
    w_hi = w.astype(BF16)
    w_lo = (w - w_hi.astype(F32)).astype(BF16)
    p_hi = _dot(jnp.concatenate([a_hi, a_lo], axis=0), w_hi)
    acc = p_hi[:COND_ROWS] + p_hi[COND_ROWS:] + _dot(a_hi, w_lo)
    out_ref[0] = acc + b_ref[0]


def _adaln(cond, ada_w, ada_b):
    n_out = ada_w.shape[-1]
    tn = 1024
    return pl.pallas_call(
        _adaln_kernel,
        grid=(DEPTH, n_out // tn),
        in_specs=[
            pl.BlockSpec((COND_ROWS, D_MODEL), lambda i, j: (0, 0)),
            pl.BlockSpec((1, D_MODEL, tn), lambda i, j: (i, 0, j)),
            pl.BlockSpec((1, 1, tn), lambda i, j: (i, 0, j)),
        ],
        out_specs=pl.BlockSpec((1, COND_ROWS, tn), lambda i, j: (i, 0, j)),
        out_shape=jax.ShapeDtypeStruct((DEPTH, COND_ROWS, n_out), F32),
        compiler_params=pltpu.CompilerParams(vmem_limit_bytes=VMEM_LIMIT),
        name="adaln",
    )(cond, ada_w, ada_b.reshape(DEPTH, 1, n_out))


def _rmsnorm_mod(h, g, shift, scale):
    y = h * lax.rsqrt(jnp.mean(h * h, axis=-1, keepdims=True) + EPS) * g
    return y * (1.0 + scale) + shift


def _qkv_kernel(*refs, rope, want_q):
    if rope:
        h_ref, mod_ref, g_ref, w_ref, cos_ref, sa_ref, sb_ref = refs[:7]
        outs = refs[7:]
    else:
        h_ref, mod_ref, g_ref, w_ref = refs[:4]
        outs = refs[4:]
    xn = _rmsnorm_mod(h_ref[0], g_ref[...], mod_ref[0, 0:1, :], mod_ref[0, 1:2, :]).astype(BF16)

    def rotate(t):
        return (t * cos_ref[...] + pltpu.roll(t, LANES - 16, axis=1) * sa_ref[...]
                + pltpu.roll(t, 16, axis=1) * sb_ref[...])

    parts = ("q", "k", "v") if want_q else ("k", "v")
    for name, o_ref in zip(parts, outs):
        col = {"q": 0, "k": D_MODEL, "v": 2 * D_MODEL}[name]
        acc = _dot(xn, w_ref[:, col:col + D_MODEL])
        if name == "q":
            acc = acc * Q_SCALE
        if rope and name != "v":
            for c in range(N_LANE_TILES):
                sl = slice(c * LANES, (c + 1) * LANES)
                o_ref[0, :, sl] = rotate(acc[:, sl]).astype(BF16)
        else:
            o_ref[0] = acc.astype(BF16)


def _qkv(h, mod, g, w, rope_tabs, *, tm, want_q=True):
    G, R, _ = h.shape
    rope = rope_tabs is not None
    row_spec = pl.BlockSpec((1, tm, D_MODEL), lambda b, t: (b, t, 0))
    in_specs = [
        row_spec,
        pl.BlockSpec((1, 6, D_MODEL), lambda b, t: (b, 0, 0)),
        pl.BlockSpec((1, D_MODEL), lambda b, t: (0, 0)),
        pl.BlockSpec((D_MODEL, 3 * D_MODEL), lambda b, t: (0, 0)),
    ]
    args = [h, mod, g, w]
    if rope:
        in_specs += [pl.BlockSpec((tm, LANES), lambda b, t: (t, 0))] * 3
        args += list(rope_tabs)
    n_out = 3 if want_q else 2
    return pl.pallas_call(
        functools.partial(_qkv_kernel, rope=rope, want_q=want_q),
        grid=(G, R // tm),
        in_specs=in_specs,
        out_specs=[row_spec] * n_out,
        out_shape=[jax.ShapeDtypeStruct((G, R, D_MODEL), BF16)] * n_out,
        compiler_params=pltpu.CompilerParams(vmem_limit_bytes=VMEM_LIMIT),
        name="qkv_rope" if rope else "qkv",
    )(*args)


def _rope_tables():
    t = jnp.arange(SEQ)
    row = (t // GRID_W).astype(F32)
    col = (t % GRID_W).astype(F32)
    half = HEAD_DIM // 2
    freqs = 1.0 / (ROPE_THETA ** (jnp.arange(0, half, 2, dtype=F32) / half))
    ar = row[:, None] * freqs
    ac = col[:, None] * freqs
    ang = jnp.concatenate([ar, ar, ac, ac], axis=-1)
    cos, sin = jnp.cos(ang), jnp.sin(ang)
    first = (np.arange(HEAD_DIM) % half) < half // 2
    sa = jnp.where(first, -sin, 0.0)
    sb = jnp.where(first, 0.0, sin)
    rep = LANES // HEAD_DIM
    return tuple(jnp.tile(x, (1, rep)) for x in (cos, sa, sb))


def _stack_halves(q):
    lane = lax.broadcasted_iota(jnp.int32, q.shape, 1)
    zero = jnp.zeros_like(q)
    return jnp.concatenate([jnp.where(lane < HEAD_DIM, q, zero),
                            jnp.where(lane >= HEAD_DIM, q, zero)], axis=0)


def _with_ones(v):
    return jnp.concatenate([v, jnp.ones(v.shape, v.dtype)], axis=1)


def _diff_attn_kernel(*refs, nseg, n_cast, lam_init, block_rows, heads):
    lam_ref, subln_ref, q_ref = refs[:3]
    k_refs = refs[3:3 + nseg]
    v_refs = refs[3 + nseg:3 + 2 * nseg]
    n_in = 3 + 2 * nseg + n_cast
    o_ref = refs[n_in]
    for src_ref, dst_ref in zip(refs[n_in - n_cast:n_in], refs[n_in + 1:]):
        dst_ref[...] = src_ref[...].astype(BF16)

    lp = lam_ref[...]
    lam = (jnp.exp(jnp.sum(lp[0:1] * lp[1:2], axis=-1, keepdims=True))
           - jnp.exp(jnp.sum(lp[2:3] * lp[3:4], axis=-1, keepdims=True)) + lam_init)
    subln = subln_ref[...] * (1.0 - lam_init)

    v_aug = [[_with_ones(v_ref[0, :, head * LANES:(head + 1) * LANES]) for v_ref in v_refs]
             for head in range(heads)]

    def block(head, row0, tq):
        cols = slice(head * LANES, (head + 1) * LANES)
        qq = _stack_halves(q_ref[0, row0:row0 + tq, cols])
        s_parts = [_dot_nt(qq, k_ref[0, :, cols]) for k_ref in k_refs]
        m = s_parts[0].max(axis=-1, keepdims=True)
        for s in s_parts[1:]:
            m = jnp.maximum(m, s.max(axis=-1, keepdims=True))
        acc = None
        for s, v in zip(s_parts, v_aug[head]):
            pv = _dot(jnp.exp2(s - m).astype(BF16), v)
            acc = pv if acc is None else acc + pv
        w = 1.0 / acc[:, LANES:]
        o = acc[:tq, :LANES] * w[:tq] - acc[tq:, :LANES] * (w[tq:] * lam)
        on = o * lax.rsqrt(jnp.mean(o * o, axis=-1, keepdims=True) + EPS) * subln
        o_ref[0, row0:row0 + tq, cols] = on.astype(BF16)

    for head in range(heads):
        row0 = 0
        for tq in block_rows[head]:
            block(head, row0, tq)
            row0 += tq


def _diff_attn(lam_params, subln, q, ks, vs, *, block_rows, heads, lam_init, to_cast=()):
    B, N, _ = q.shape
    nseg = len(ks)
    assert len(block_rows) == heads and all(sum(rows) == N for rows in block_rows)
    n_h = N_LANE_TILES // heads
    steps = B * n_h

    def head_spec(n):
        return pl.BlockSpec((1, n, heads * LANES), lambda b, h: (b, 0, h))

    def cast_spec(w):
        rows, cols = w.shape
        share = 1 if rows % (steps * BF16_ROWS) == 0 else 2
        assert rows % (steps // share * BF16_ROWS) == 0
        return pl.BlockSpec((rows * share // steps, cols), lambda b, h: ((b * n_h + h) // share, 0))

    kv_specs = [head_spec(k.shape[1]) for k in ks]
    cast_specs = [cast_spec(w) for w in to_cast]
    out = pl.pallas_call(
        functools.partial(_diff_attn_kernel, nseg=nseg, n_cast=len(to_cast), lam_init=lam_init,
                          block_rows=block_rows, heads=heads),
        grid=(B, n_h),
        in_specs=[
            pl.BlockSpec((4, HEAD_DIM), lambda b, h: (0, 0)),
            pl.BlockSpec((1, LANES), lambda b, h: (0, 0)),
            head_spec(N),
        ] + kv_specs + kv_specs + cast_specs,
        out_specs=[head_spec(N)] + cast_specs,
        out_shape=[jax.ShapeDtypeStruct((B, N, D_MODEL), BF16)]
        + [jax.ShapeDtypeStruct(w.shape, BF16) for w in to_cast],
        compiler_params=pltpu.CompilerParams(vmem_limit_bytes=VMEM_LIMIT),
        name="diff_attn_%dseg" % nseg,
    )(lam_params, subln, q, *ks, *vs, *to_cast)
    return out[0], out[1:]


def _na_window_start(j):
    return min(max(j - 1, 0), NA_BLOCKS - NA_KBLK)


def _na_bias_type(j):
    return 0 if j == 0 else (2 if j == NA_BLOCKS - 1 else 1)


def _na_block_layout():
    rows = SEQ // GRID_W
    per_block = []
    for j in range(NA_BLOCKS):
        blk = []
        for a in range(NA_Q_ROWS):
            r = NA_Q_ROWS * j + a
            rs = min(max(r - NA_WIN_ROWS // 2, 0), rows - NA_WIN_ROWS)
            row = []
            for u in range(NA_K_ROWS // 2):
                kr = NA_Q_ROWS * _na_window_start(j) + 2 * u
                ok = tuple(rs <= k < rs + NA_WIN_ROWS for k in (kr, kr + 1))
                row.append((kr - r + NA_WIN_ROWS - 1,) + ok)
            blk.append(tuple(row))
        per_block.append(tuple(blk))
    assert all(per_block[j] == per_block[1] for j in range(2, NA_BLOCKS - 1))
    return (per_block[0], per_block[1], per_block[NA_BLOCKS - 1])


NA_LAYOUT = _na_block_layout()
assert all(any(lo or ro for _, lo, ro in row[NA_Q_ROWS // 2:]) for kind in NA_LAYOUT for row in kind)
NA_N_DR = 2 * NA_WIN_ROWS - 1
NA_N_PAIR = NA_N_DR + 1


def _na_pair_table(rpb):
    n_dc = 2 * NA_WIN_COLS - 1
    qc = np.arange(GRID_W)
    cs = np.clip(qc - NA_WIN_COLS // 2, 0, GRID_W - NA_WIN_COLS)
    col_ok = (qc[None, :] >= cs[:, None]) & (qc[None, :] < cs[:, None] + NA_WIN_COLS)
    dc = np.clip(qc[None, :] - qc[:, None] + NA_WIN_COLS - 1, 0, n_dc - 1)
    pick_d = np.zeros((NA_N_DR, NA_N_PAIR, 2), np.float32)
    for i in range(NA_N_PAIR):
        for p in range(2):
            if 0 <= i - 1 + p < NA_N_DR:
                pick_d[i - 1 + p, i, p] = 1.0
    pick_c = np.zeros((2, n_dc, GRID_W, 2, GRID_W), np.float32)
    for p in range(2):
        pick_c[p, dc, qc[:, None], p, qc[None, :]] = 1.0
    lhs = jnp.einsum('hdj,dip->hipj', rpb, jnp.asarray(pick_d), precision=lax.Precision.HIGHEST)
    pairs = jnp.einsum('hic,cx->hix', lhs.reshape(-1, NA_N_PAIR, 2 * n_dc),
                       jnp.asarray(pick_c.reshape(2 * n_dc, GRID_W * LANES)),
                       precision=lax.Precision.HIGHEST).reshape(-1, NA_N_PAIR, GRID_W, LANES)
    valid = pick_d.sum(0)[:, None, :, None] * col_ok[None, :, None, :]
    pairs = jnp.where(jnp.asarray(valid.reshape(NA_N_PAIR, GRID_W, LANES) > 0), pairs * LOG2E, -jnp.inf)
    return pairs.reshape(N_LANE_TILES, 2, NA_N_PAIR, GRID_W, LANES)


def _na_fill_bias(pair_ref, bias_scr):
    lane = lax.broadcasted_iota(jnp.int32, (GRID_W, LANES), 1)
    neg = jnp.full((GRID_W, LANES), -jnp.inf, F32)
    for kind, layout in enumerate(NA_LAYOUT):
        for head in range(2):
            for a, row in enumerate(layout):
                r0 = head * NA_TQ + a * GRID_W
                for u, (d, left_ok, right_ok) in enumerate(row):
                    slab = neg
                    if left_ok or right_ok:
                        assert not left_ok or 0 <= d < NA_N_DR
                        assert not right_ok or 0 <= d + 1 < NA_N_DR
                        slab = pair_ref[0, head, d + 1]
                        if not left_ok:
                            slab = jnp.where(lane < HEAD_DIM, neg, slab)
                        if not right_ok:
                            slab = jnp.where(lane < HEAD_DIM, slab, neg)
                    bias_scr[kind, r0:r0 + GRID_W, u * LANES:(u + 1) * LANES] = slab


def _nbr_attn_kernel(q_ref, k_ref, v_ref, kc_ref, vc_ref, pair_ref, o_ref, bias_scr):
    n_win = NA_K_ROWS * GRID_W
    lane = lax.broadcasted_iota(jnp.int32, (NA_TQ, LANES), 1)

    @pl.when(pl.program_id(1) == 0)
    def _():
        _na_fill_bias(pair_ref, bias_scr)

    for bb in range(q_ref.shape[0]):
        v_aug = _with_ones(v_ref[bb])
        vc_aug = _with_ones(vc_ref[bb])
        for j in range(NA_BLOCKS):
            rows = slice(j * NA_TQ, (j + 1) * NA_TQ)
            win0 = _na_window_start(j) * NA_TQ
            win = slice(win0, win0 + n_win)
            qq = _stack_halves(q_ref[bb, rows, :])
            bias = bias_scr[_na_bias_type(j)]
            t1 = win0 + NA_TQ
            k_a = jnp.concatenate([kc_ref[bb], k_ref[bb, win0:t1, :]], axis=0)
            v_a = jnp.concatenate([vc_aug, v_aug[win0:t1]], axis=0)
            s_a = _dot_nt(qq, k_a)
            s_a = jnp.concatenate([s_a[:, :CTX_LEN], s_a[:, CTX_LEN:] + bias[:, :NA_TQ]], axis=1)
            m_a = s_a.max(axis=-1, keepdims=True)
            acc_a = _dot(jnp.exp2(s_a - m_a).astype(BF16), v_a)
            s_b = _dot_nt(qq, k_ref[bb, t1:win0 + n_win, :]) + bias[:, NA_TQ:]
            m_b = s_b.max(axis=-1, keepdims=True)
            acc_b = _dot(jnp.exp2(s_b - m_b).astype(BF16), v_aug[t1:win0 + n_win])
            m = jnp.maximum(m_a, m_b)
            acc = acc_a * jnp.exp2(m_a - m) + acc_b * jnp.exp2(m_b - m)
            o2 = acc[:, :LANES] / acc[:, LANES:]
            o_ref[bb, rows, :] = jnp.where(lane < HEAD_DIM, o2[:NA_TQ], o2[NA_TQ:]).astype(BF16)


def _nbr_attn(q, k, v, kc, vc, pair_tab):
    B, N, _ = q.shape

    def head_spec(n):
        return pl.BlockSpec((NA_BATCH_PER_STEP, n, LANES), lambda hp, b: (b, 0, hp))

    pair_spec = pl.BlockSpec((1, 2, NA_N_PAIR, GRID_W, LANES), lambda hp, b: (hp, 0, 0, 0, 0))
    return pl.pallas_call(
        _nbr_attn_kernel,
        grid=(N_LANE_TILES, B // NA_BATCH_PER_STEP),
        in_specs=[head_spec(N), head_spec(N), head_spec(N), head_spec(CTX_LEN), head_spec(CTX_LEN),
                  pair_spec],
        out_specs=head_spec(N),
        out_shape=jax.ShapeDtypeStruct(q.shape, BF16),
        scratch_shapes=[pltpu.VMEM((3, 2 * NA_TQ, NA_K_ROWS * GRID_W), F32)],
        compiler_params=pltpu.CompilerParams(vmem_limit_bytes=VMEM_LIMIT,
                                             dimension_semantics=("arbitrary", "arbitrary")),
        name="nbr_attn",
    )(q, k, v, kc, vc, pair_tab)


FFN_CHUNKS = ((0, 2816),)

def _out_ffn_kernel(*refs, final):
    o_ref, h_ref, mod_ref, g_ref, wo_ref, wgu_ref, wd_ref = refs[:7]
    if final:
        gfin_ref, out_ref = refs[7:]
    else:
        (out_ref,) = refs[7:]
    h1 = h_ref[0] + mod_ref[0, 2:3, :] * _dot(o_ref[0], wo_ref[...])
    xn = _rmsnorm_mod(h1, g_ref[...], mod_ref[0, 3:4, :], mod_ref[0, 4:5, :]).astype(BF16)
    acc = None
    for start, size in FFN_CHUNKS:
        gate = _dot(xn, wgu_ref[:, start:start + size])
        up = _dot(xn, wgu_ref[:, D_FF + start:D_FF + start + size])
        act = (gate * jax.nn.sigmoid(gate) * up).astype(BF16)
        part = _dot(act, wd_ref[start:start + size, :])
        acc = part if acc is None else acc + part
    h2 = h1 + mod_ref[0, 5:6, :] * acc
    if final:
        h2 = h2 * lax.rsqrt(jnp.mean(h2 * h2, axis=-1, keepdims=True) + EPS) * gfin_ref[...]
    out_ref[0] = h2


def _out_ffn(o, h, mod, g, wo, wgu, wd, layer, gfin, *, tm):
    G, R, _ = h.shape
    final = gfin is not None
    row_spec = pl.BlockSpec((1, tm, D_MODEL), lambda b, t: (b, t, 0))

    def resident(shape, index):
        return pl.BlockSpec(shape, lambda b, t: index, pipeline_mode=pl.Buffered(1))

    in_specs = [
        row_spec,
        row_spec,
        pl.BlockSpec((1, 6, D_MODEL), lambda b, t: (b, 0, 0)),
        pl.BlockSpec((1, D_MODEL), lambda b, t: (0, 0)),
        resident((D_MODEL, D_MODEL), (0, 0)),
        resident((None, D_MODEL, 2 * D_FF), (layer, 0, 0)),
        resident((None, D_FF, D_MODEL), (layer, 0, 0)),
    ]
    args = [o, h, mod, g, wo, wgu, wd]
    if final:
        in_specs.append(pl.BlockSpec((1, D_MODEL), lambda b, t: (0, 0)))
        args.append(gfin)
    return pl.pallas_call(
        functools.partial(_out_ffn_kernel, final=final),
        grid=(G, R // tm),
        in_specs=in_specs,
        out_specs=row_spec,
        out_shape=jax.ShapeDtypeStruct(h.shape, F32),
        compiler_params=pltpu.CompilerParams(vmem_limit_bytes=VMEM_LIMIT),
        name="out_ffn_final" if final else "out_ffn",
    )(*args)


def kernel(x, c, ctx, c_ctx, ada_w, ada_b, norm_mix, norm_ffn, da_wqkv, da_lambda_q1, da_lambda_k1,
           da_lambda_q2, da_lambda_k2, da_subln, da_wo, na_wqkv, na_rpb, na_wo, ffn_w_gate_up,
           ffn_w_down, norm_final):
    B, N, D = x.shape
    L = ctx.shape[1]
    TM = 1024
    assert (B, N, D, L) == (BATCH, SEQ, D_MODEL, CTX_LEN) and ctx.shape == (B, L, D)
    assert ada_w.shape == (DEPTH, D, 6 * D) and ffn_w_gate_up.shape == (DEPTH, D, 2 * D_FF)
    assert da_wqkv.shape == na_wqkv.shape == (1, D, 3 * D)
    assert na_rpb.shape == (1, D // HEAD_DIM, 2 * NA_WIN_ROWS - 1, 2 * NA_WIN_COLS - 1)

    cond = jnp.zeros((COND_ROWS, D), F32).at[:B].set(c).at[CTX_MOD_ROW].set(c_ctx)
    mods = _adaln(cond, ada_w, ada_b).reshape(DEPTH, COND_ROWS, 6, D)

    h = x
    hc = ctx.reshape(1, B * L, D)
    rope_tabs = _rope_tables()

    mod_l = mods[0, :B]
    mod_c = mods[0, CTX_MOD_ROW:CTX_MOD_ROW + 1]
    g_mix = norm_mix[0].reshape(1, D)
    g_ffn = norm_ffn[0].reshape(1, D)
    wqkv = da_wqkv[0].astype(BF16)
    q_l, k_l, v_l = _qkv(h, mod_l, g_mix, wqkv, rope_tabs, tm=TM)
    q_c, k_c, v_c = (t.reshape(B, L, D) for t in _qkv(hc, mod_c, g_mix, wqkv, None, tm=TM))
    lam_params = jnp.stack([da_lambda_q1[0], da_lambda_k1[0], da_lambda_q2[0], da_lambda_k2[0]])
    subln = da_subln[0].reshape(1, LANES)
    lam_init = 0.8 - 0.6 * float(np.exp(-0.3 * 0))
    later_weights = [ffn_w_gate_up.reshape(DEPTH * D, 2 * D_FF), ffn_w_down.reshape(DEPTH * D_FF, D),
                     da_wo[0], na_wqkv[0], na_wo[0]]
    o_l, (wgu, wd, wo, na_wqkv_bf, na_wo_bf) = _diff_attn(
        lam_params, subln, q_l, [k_c, k_l], [v_c, v_l], block_rows=DA_BLOCK_ROWS,
        heads=DA_HEADS_PER_STEP, lam_init=lam_init, to_cast=later_weights)
    wgu = wgu.reshape(DEPTH, D, 2 * D_FF)
    wd = wd.reshape(DEPTH, D_FF, D)
    o_c, _ = _diff_attn(lam_params, subln, q_c, [k_c], [v_c], block_rows=((L,),) * N_LANE_TILES,
                        heads=N_LANE_TILES, lam_init=lam_init)
    h = _out_ffn(o_l, h, mod_l, g_ffn, wo, wgu, wd, 0, None, tm=TM)
    hc = _out_ffn(o_c.reshape(1, B * L, D), hc, mod_c, g_ffn, wo, wgu, wd, 0, None, tm=TM)

    mod_l = mods[1, :B]
    mod_c = mods[1, CTX_MOD_ROW:CTX_MOD_ROW + 1]
    g_mix = norm_mix[1].reshape(1, D)
    g_ffn = norm_ffn[1].reshape(1, D)
    q_l, k_l, v_l = _qkv(h, mod_l, g_mix, na_wqkv_bf, None, tm=TM)
    k_c, v_c = (t.reshape(B, L, D)
                for t in _qkv(hc, mod_c, g_mix, na_wqkv_bf, None, tm=TM, want_q=False))
    o_l = _nbr_attn(q_l, k_l, v_l, k_c, v_c, _na_pair_table(na_rpb[0]))
    return _out_ffn(o_l, h, mod_l, g_ffn, na_wo_bf, wgu, wd, 1, norm_final.reshape(1, D), tm=TM)
```
